```python
import math
import jax
import jax.numpy as jnp
from jax import lax
import numpy as np

D_MODEL = 1024
BATCH = 1
SEQ = 16384
DEPTH = 4

GRID_W = 64
CTX_LEN = 256

S5_WIDTH = 512
S5_GROUP = 16
S5_GROUPS = S5_WIDTH // S5_GROUP
S5_STATE = 64
S5_DT_MIN = 0.001
S5_DT_MAX = 0.1
FNET_WIDTH = D_MODEL - S5_WIDTH
FNET_GROUPS = 4
FNET_GROUP = FNET_WIDTH // FNET_GROUPS
MIX_WIDTH = S5_WIDTH + FNET_WIDTH

DA_HEADS = 8
DA_HEAD_DIM = 64
DA_V_DIM = 2 * DA_HEAD_DIM
QK_WIDTH = DA_HEADS * 2 * DA_HEAD_DIM
V_WIDTH = DA_HEADS * DA_V_DIM
Q_BLOCK = 128
ROPE_BASE = 10000.0
ROPE_FREQS = DA_HEAD_DIM // 4

N_EXPERTS = 32
TOP_K = 4
D_FF = D_MODEL
SWIGLU_LIMIT = 7.0
SWIGLU_ALPHA = 1.702
EXPERT_BLOCK = 128

N_EVEN = (DEPTH + 1) // 2
N_ODD = DEPTH // 2
DEEPNORM_ALPHA = (2.0 * DEPTH) ** 0.25
DEEPNORM_BETA = (8.0 * DEPTH) ** -0.25
LN_EPS = 1e-5

kernel_name = 'hybrid_s5_fnet_diffattn_moe_dit'


def _layer_norm(x, g, b):
    xf = x.astype(jnp.float32)
    xc = xf - jnp.mean(xf, axis=-1, keepdims=True)
    var = jnp.mean(xc * xc, axis=-1, keepdims=True)
    y = xc * lax.rsqrt(var + LN_EPS) * g.astype(jnp.float32) + b.astype(jnp.float32)
    return y.astype(x.dtype)


def _modulate(x, shift, scale):
    return x * (1.0 + scale) + shift


def _axial_rope_tables(rows, dtype):
    row = jnp.broadcast_to(jnp.arange(rows, dtype=jnp.float32)[:, None], (rows, GRID_W)).reshape(-1)
    col = jnp.broadcast_to(jnp.arange(GRID_W, dtype=jnp.float32)[None, :], (rows, GRID_W)).reshape(-1)
    theta = ROPE_BASE ** (-jnp.arange(ROPE_FREQS, dtype=jnp.float32) / ROPE_FREQS)
    ang = jnp.stack([row[:, None] * theta, col[:, None] * theta], axis=1)
    ang = jnp.stack([ang, ang], axis=2).reshape(rows * GRID_W, DA_HEAD_DIM)
    return jnp.cos(ang).astype(dtype), jnp.sin(ang).astype(dtype)


def _rope_2d(x, cos, sin):
    xr = x.reshape(x.shape[:-1] + (2, 2, ROPE_FREQS))
    rot = jnp.stack([-xr[..., 1, :], xr[..., 0, :]], axis=-2).reshape(x.shape)
    return x * cos + rot * sin


def _ssm_combine(left, right):
    a_l, b_l = left
    a_r, b_r = right
    return a_l * a_r, a_r * b_l + b_r


def _diag_scan(lam_bar, bu, h0, reverse):
    if reverse:
        bu = jnp.flip(bu, axis=1)
    if h0 is not None:
        bu = bu.at[:, 0].add(lam_bar * h0)
    a = jnp.broadcast_to(lam_bar, bu.shape)
    _, h = lax.associative_scan(_ssm_combine, (a, bu), axis=1)
    return jnp.flip(h, axis=1) if reverse else h


def _s5_discretize(lam_re, lam_im, log_dt, b_re, b_im):
    lam = lax.complex(lam_re.astype(jnp.float32), lam_im.astype(jnp.float32))
    dt = jnp.exp(log_dt.astype(jnp.float32))[..., None]
    lam_bar = jnp.exp(lam * dt)
    b = lax.complex(b_re.astype(jnp.float32), b_im.astype(jnp.float32))
    b_bar = ((lam_bar - 1.0) / lam)[..., None] * b
    return lam_bar, b_bar


def _s5_states(u, lam_bar, b_bar, h0_f, h0_b):
    uc = u.astype(jnp.float32).astype(jnp.complex64)
    hf = _diag_scan(lam_bar[0], jnp.einsum('blgc,gnc->blgn', uc, b_bar[0]), h0_f, False)
    hb = _diag_scan(lam_bar[1], jnp.einsum('blgc,gnc->blgn', uc, b_bar[1]), h0_b, True)
    return hf, hb


def _s5_readout(u, hf, hb, c_re, c_im, d_skip):
    c = lax.complex(c_re.astype(jnp.float32), c_im.astype(jnp.float32))
    y = jnp.real(jnp.einsum('blgn,gcn->blgc', hf, c[0]) + jnp.einsum('blgn,gcn->blgc', hb, c[1]))
    return (y + d_skip.astype(jnp.float32) * u.astype(jnp.float32)).astype(u.dtype)


def _fnet_mix(f, w, b):
    ff = f.astype(jnp.float32)
    fc = ff - jnp.mean(ff, axis=-1, keepdims=True)
    fn = fc * lax.rsqrt(jnp.mean(fc * fc, axis=-1, keepdims=True) + LN_EPS)
    spec = jnp.real(jnp.fft.fft2(fn, axes=(1, 3), norm='ortho')).astype(f.dtype)
    return jnp.einsum('blgc,gcd->blgd', spec, w) + b


def _s5_fnet_mixer(h_ctx, h_lat, w_in, lam_re, lam_im, log_dt, b_re, b_im, c_re, c_im, d_skip,
                   w_glu, b_glu, w_fnet, b_fnet, w_out, with_ctx_out):
    lam_bar, b_bar = _s5_discretize(lam_re, lam_im, log_dt, b_re, b_im)

    def split(h):
        z = h @ w_in
        bsz, length = z.shape[:2]
        u = z[..., :S5_WIDTH].reshape(bsz, length, S5_GROUPS, S5_GROUP)
        f = z[..., S5_WIDTH:].reshape(bsz, length, FNET_GROUPS, FNET_GROUP)
        return u, f

    def merge(u, hf, hb, f):
        bsz, length = u.shape[:2]
        y = jax.nn.gelu(_s5_readout(u, hf, hb, c_re, c_im, d_skip).reshape(bsz, length, S5_WIDTH))
        y = y * jax.nn.sigmoid(y @ w_glu + b_glu)
        g = _fnet_mix(f, w_fnet, b_fnet).reshape(bsz, length, FNET_WIDTH)
        return jnp.concatenate([y, g], axis=-1) @ w_out

    u_c, f_c = split(h_ctx)
    u_l, f_l = split(h_lat)
    hf_c, hb_c = _s5_states(u_c, lam_bar, b_bar, None, None)
    hf_l, hb_l = _s5_states(u_l, lam_bar, b_bar, hf_c[:, -1], hb_c[:, 0])
    y_lat = merge(u_l, hf_l, hb_l, f_l)
    y_ctx = merge(u_c, hf_c, hb_c, f_c) if with_ctx_out else None
    return y_ctx, y_lat


def _diff_attend(q, k, v, lam):
    s = jnp.einsum('bhiqd,bhikd->bhiqk', q, k, preferred_element_type=jnp.float32) * (DA_HEAD_DIM ** -0.5)
    p = jax.nn.softmax(s, axis=-1)
    a = p[:, :, 0] - lam * p[:, :, 1]
    return jnp.einsum('bhqk,bhkd->bhqd', a.astype(v.dtype), v)


def _diff_attention_mixer(h_ctx, h_lat, w_qkv, w_o, lq1, lk1, lq2, lk2, subln_g, lam_init,
                          cos, sin, with_ctx_out):
    def project(h):
        z = h @ w_qkv
        bsz, length = z.shape[:2]
        q = z[..., :QK_WIDTH].reshape(bsz, length, DA_HEADS, 2, DA_HEAD_DIM).transpose(0, 2, 3, 1, 4)
        k = z[..., QK_WIDTH:2 * QK_WIDTH].reshape(bsz, length, DA_HEADS, 2, DA_HEAD_DIM).transpose(0, 2, 3, 1, 4)
        v = z[..., 2 * QK_WIDTH:].reshape(bsz, length, DA_HEADS, DA_V_DIM).transpose(0, 2, 1, 3)
        return q, k, v

    lam = (jnp.exp(jnp.sum(lq1.astype(jnp.float32) * lk1.astype(jnp.float32)))
           - jnp.exp(jnp.sum(lq2.astype(jnp.float32) * lk2.astype(jnp.float32))) + lam_init)
    q_c, k_c, v_c = project(h_ctx)
    q_l, k_l, v_l = project(h_lat)
    q_l = _rope_2d(q_l, cos, sin)
    k_l = _rope_2d(k_l, cos, sin)
    k_all = jnp.concatenate([k_c, k_l], axis=3)
    v_all = jnp.concatenate([v_c, v_l], axis=2)

    bsz, _, _, length, _ = q_l.shape
    nb = length // Q_BLOCK
    q_blocks = jnp.moveaxis(q_l.reshape(bsz, DA_HEADS, 2, nb, Q_BLOCK, DA_HEAD_DIM), 3, 0)
    o = lax.map(lambda qb: _diff_attend(qb, k_all, v_all, lam), q_blocks)
    o_lat = jnp.moveaxis(o, 0, 2).reshape(bsz, DA_HEADS, length, DA_V_DIM)

    def finish(o_h):
        of = o_h.astype(jnp.float32)
        of = of * lax.rsqrt(jnp.mean(of * of, axis=-1, keepdims=True) + LN_EPS)
        of = of * subln_g.astype(jnp.float32) * (1.0 - lam_init)
        b_, h_, l_, _ = o_h.shape
        return of.astype(o_h.dtype).transpose(0, 2, 1, 3).reshape(b_, l_, h_ * DA_V_DIM) @ w_o

    y_lat = finish(o_lat)
    y_ctx = finish(_diff_attend(q_c, k_c, v_c, lam)) if with_ctx_out else None
    return y_ctx, y_lat


def _moe(h, w_router, b_router, w_gate, b_gate, w_up, b_up, w_down, b_down):
    lead = h.shape[:-1]
    hf = h.reshape(-1, D_MODEL)
    n = hf.shape[0]
    logits = (hf @ w_router + b_router).astype(jnp.float32)
    top_val, top_idx = lax.top_k(logits, TOP_K)
    top_w = jax.nn.softmax(top_val, axis=-1)
    e_flat = top_idx.reshape(-1)
    tok_flat = jnp.repeat(jnp.arange(n, dtype=jnp.int32), TOP_K)
    w_flat = top_w.reshape(-1)
    order = jnp.argsort(e_flat)
    e_s, tok_s, w_s = e_flat[order], tok_flat[order], w_flat[order]
    counts = jnp.bincount(e_flat, length=N_EXPERTS)
    start = jnp.cumsum(counts) - counts
    padded = (counts + EXPERT_BLOCK - 1) // EXPERT_BLOCK * EXPERT_BLOCK
    pad_end = jnp.cumsum(padded)
    pad_start = pad_end - padded
    nk = n * TOP_K
    dest = pad_start[e_s] + jnp.arange(nk, dtype=pad_start.dtype) - start[e_s]
    n_blocks = -(-nk // EXPERT_BLOCK) + N_EXPERTS
    cap = n_blocks * EXPERT_BLOCK
    x_disp = jnp.zeros((cap, D_MODEL), h.dtype).at[dest].set(hf[tok_s])
    blk_e = jnp.minimum(jnp.searchsorted(pad_end, jnp.arange(n_blocks, dtype=pad_end.dtype) * EXPERT_BLOCK,
                                         side='right'), N_EXPERTS - 1)

    def expert_block(args):
        xb, e = args
        g = jnp.minimum(xb @ w_gate[e] + b_gate[e], SWIGLU_LIMIT)
        u = jnp.clip(xb @ w_up[e] + b_up[e], -SWIGLU_LIMIT, SWIGLU_LIMIT)
        return (g * jax.nn.sigmoid(SWIGLU_ALPHA * g) * (u + 1.0)) @ w_down[e] + b_down[e]

    y_disp = lax.map(expert_block, (x_disp.reshape(n_blocks, EXPERT_BLOCK, D_MODEL), blk_e))
    y_disp = y_disp.reshape(cap, D_MODEL)
    y = jnp.zeros_like(hf).at[tok_s].add(y_disp[dest] * w_s[:, None].astype(h.dtype))
    return y.reshape(lead + (D_MODEL,))


def setup_inputs(seed: int = 0) -> dict:
    key = jax.random.key(seed)
    ks = jax.random.split(key, 37)
    f32 = jnp.float32

    def nrm(i, shape, scale):
        return jax.random.normal(ks[i], shape, f32) * scale

    beta = DEEPNORM_BETA
    lam_im = math.pi * jnp.arange(S5_STATE, dtype=f32) + nrm(10, (N_EVEN, 2, S5_GROUPS, S5_STATE), 0.01)
    return {
        'x': nrm(0, (BATCH, SEQ, D_MODEL), 1.0),
        'c': nrm(1, (BATCH, D_MODEL), 1.0),
        'ctx': nrm(2, (BATCH, CTX_LEN, D_MODEL), 1.0),
        'c_ctx': nrm(3, (D_MODEL,), 1.0),
        'ada_w': nrm(4, (DEPTH, D_MODEL, 6 * D_MODEL), 0.5 * D_MODEL ** -0.5),
        'ada_b': nrm(5, (DEPTH, 6 * D_MODEL), 0.02),
        'ln_g': 1.0 + nrm(6, (DEPTH, 2, D_MODEL), 0.02),
        'ln_b': nrm(7, (DEPTH, 2, D_MODEL), 0.02),
        'even_w_in': nrm(8, (N_EVEN, D_MODEL, MIX_WIDTH), D_MODEL ** -0.5),
        's5_lam_re': -0.5 + nrm(9, (N_EVEN, 2, S5_GROUPS, S5_STATE), 0.01),
        's5_lam_im': lam_im,
        's5_log_dt': jax.random.uniform(ks[11], (N_EVEN, 2, S5_GROUPS), f32,
                                        math.log(S5_DT_MIN), math.log(S5_DT_MAX)),
        's5_b_re': nrm(12, (N_EVEN, 2, S5_GROUPS, S5_STATE, S5_GROUP), (2.0 * S5_GROUP) ** -0.5),
        's5_b_im': nrm(13, (N_EVEN, 2, S5_GROUPS, S5_STATE, S5_GROUP), (2.0 * S5_GROUP) ** -0.5),
        's5_c_re': nrm(14, (N_EVEN, 2, S5_GROUPS, S5_GROUP, S5_STATE), S5_STATE ** -0.5),
        's5_c_im': nrm(15, (N_EVEN, 2, S5_GROUPS, S5_GROUP, S5_STATE), S5_STATE ** -0.5),
        's5_d': nrm(16, (N_EVEN, S5_GROUPS, S5_GROUP), 0.5),
        's5_w_glu': nrm(17, (N_EVEN, S5_WIDTH, S5_WIDTH), S5_WIDTH ** -0.5),
        's5_b_glu': nrm(18, (N_EVEN, S5_WIDTH), 0.02),
        'fnet_w': nrm(19, (N_EVEN, FNET_GROUPS, FNET_GROUP, FNET_GROUP), FNET_GROUP ** -0.5),
        'fnet_b': nrm(20, (N_EVEN, FNET_GROUPS, FNET_GROUP), 0.02),
        'even_w_out': nrm(21, (N_EVEN, MIX_WIDTH, D_MODEL), beta * MIX_WIDTH ** -0.5),
        'odd_w_qkv': nrm(22, (N_ODD, D_MODEL, 2 * QK_WIDTH + V_WIDTH), D_MODEL ** -0.5),
        'odd_w_o': nrm(23, (N_ODD, V_WIDTH, D_MODEL), beta * V_WIDTH ** -0.5),
        'da_lq1': nrm(24, (N_ODD, DA_HEAD_DIM), 0.1),
        'da_lk1': nrm(25, (N_ODD, DA_HEAD_DIM), 0.1),
        'da_lq2': nrm(26, (N_ODD, DA_HEAD_DIM), 0.1),
        'da_lk2': nrm(27, (N_ODD, DA_HEAD_DIM), 0.1),
        'da_subln_g': 1.0 + nrm(28, (N_ODD, DA_V_DIM), 0.02),
        'router_w': nrm(29, (DEPTH, D_MODEL, N_EXPERTS), D_MODEL ** -0.5),
        'router_b': nrm(30, (DEPTH, N_EXPERTS), 0.01),
        'moe_w_gate': nrm(31, (DEPTH, N_EXPERTS, D_MODEL, D_FF), D_MODEL ** -0.5),
        'moe_b_gate': nrm(32, (DEPTH, N_EXPERTS, D_FF), 0.02),
        'moe_w_up': nrm(33, (DEPTH, N_EXPERTS, D_MODEL, D_FF), D_MODEL ** -0.5),
        'moe_b_up': nrm(34, (DEPTH, N_EXPERTS, D_FF), 0.02),
        'moe_w_down': nrm(35, (DEPTH, N_EXPERTS, D_FF, D_MODEL), beta * D_FF ** -0.5),
        'moe_b_down': nrm(36, (DEPTH, N_EXPERTS, D_MODEL), 0.02),
    }


def reference(x, c, ctx, c_ctx, ada_w, ada_b, ln_g, ln_b, even_w_in, s5_lam_re, s5_lam_im, s5_log_dt,
              s5_b_re, s5_b_im, s5_c_re, s5_c_im, s5_d, s5_w_glu, s5_b_glu, fnet_w, fnet_b, even_w_out,
              odd_w_qkv, odd_w_o, da_lq1, da_lk1, da_lq2, da_lk2, da_subln_g, router_w, router_b,
              moe_w_gate, moe_b_gate, moe_w_up, moe_b_up, moe_w_down, moe_b_down):
    rows = x.shape[1] // GRID_W
    cos, sin = _axial_rope_tables(rows, x.dtype)
    cond_lat = jax.nn.silu(c)[:, None, :]
    cond_ctx = jax.nn.silu(c_ctx)
    n_ctx = ctx.shape[1]
    for l in range(DEPTH):
        last = l == DEPTH - 1
        i = l // 2
        mod_l = jnp.split(cond_lat @ ada_w[l] + ada_b[l], 6, axis=-1)
        mod_c = jnp.split(cond_ctx @ ada_w[l] + ada_b[l], 6, axis=-1)
        h_l = _modulate(x, mod_l[0], mod_l[1])
        h_c = _modulate(ctx, mod_c[0], mod_c[1])
        if l % 2 == 0:
            y_c, y_l = _s5_fnet_mixer(h_c, h_l, even_w_in[i], s5_lam_re[i], s5_lam_im[i], s5_log_dt[i],
                                      s5_b_re[i], s5_b_im[i], s5_c_re[i], s5_c_im[i], s5_d[i],
                                      s5_w_glu[i], s5_b_glu[i], fnet_w[i], fnet_b[i], even_w_out[i],
                                      not last)
        else:
            lam_init = 0.8 - 0.6 * math.exp(-0.3 * l)
            y_c, y_l = _diff_attention_mixer(h_c, h_l, odd_w_qkv[i], odd_w_o[i], da_lq1[i], da_lk1[i],
                                             da_lq2[i], da_lk2[i], da_subln_g[i], lam_init, cos, sin,
                                             not last)
        x = _layer_norm(DEEPNORM_ALPHA * x + mod_l[2] * y_l, ln_g[l, 0], ln_b[l, 0])
        h_l = _modulate(x, mod_l[3], mod_l[4])
        moe_args = (router_w[l], router_b[l], moe_w_gate[l], moe_b_gate[l], moe_w_up[l], moe_b_up[l],
                    moe_w_down[l], moe_b_down[l])
        if last:
            y_l = _moe(h_l, *moe_args)
        else:
            ctx = _layer_norm(DEEPNORM_ALPHA * ctx + mod_c[2] * y_c, ln_g[l, 0], ln_b[l, 0])
            h_c = _modulate(ctx, mod_c[3], mod_c[4])
            y_all = _moe(jnp.concatenate([h_c, h_l], axis=1), *moe_args)
            y_c, y_l = y_all[:, :n_ctx], y_all[:, n_ctx:]
            ctx = _layer_norm(DEEPNORM_ALPHA * ctx + mod_c[5] * y_c, ln_g[l, 1], ln_b[l, 1])
        x = _layer_norm(DEEPNORM_ALPHA * x + mod_l[5] * y_l, ln_g[l, 1], ln_b[l, 1])
    return x
```

```python
import functools
import math

import jax
import jax.numpy as jnp
from jax import lax
from jax.experimental import pallas as pl
from jax.experimental.pallas import tpu as pltpu

F32 = jnp.float32
BF16 = jnp.bfloat16
HIGHEST = lax.Precision.HIGHEST

D = 1024
DEPTH = 4
GRID_W = 64
S5_W = 512
S5_G = 32
S5_C = 16
S5_N = 64
FN_W = 512
FN_G = 4
FN_C = 128
HEADS = 8
HD = 64
VD = 128
ROPE_BASE = 10000.0
ROPE_F = 16
N_EXP = 32
TOP_K = 4
SWIGLU_LIMIT = 7.0
SWIGLU_ALPHA = 1.702
ALPHA = (2.0 * DEPTH) ** 0.25
LN_EPS = 1e-5

TM = 256
TME = 256
CH = 16
MIB = 1024 * 1024


def _cp(sem, vmem_mib=48):
    return pltpu.CompilerParams(dimension_semantics=sem, vmem_limit_bytes=vmem_mib * MIB)


def _rowtype(nctxb):
    return lambda i: (jnp.where(i >= nctxb, 1, 0), 0, 0)


def _layer_norm_rows(z, g, b):
    mu = jnp.mean(z, axis=-1, keepdims=True)
    zc = z - mu
    var = jnp.mean(zc * zc, axis=-1, keepdims=True)
    return zc * lax.rsqrt(var + LN_EPS) * g + b


def _ada_kernel(c_ref, w_ref, b_ref, o_ref):
    c = c_ref[...]
    a = c * jax.nn.sigmoid(c)
    o_ref[0] = jnp.dot(a, w_ref[0], preferred_element_type=F32, precision=HIGHEST) + b_ref[0]


def _ada_mods(cond8, ada_w, ada_b):
    tn = 1536
    return pl.pallas_call(
        _ada_kernel,
        grid=(DEPTH, 6 * D // tn),
        in_specs=[pl.BlockSpec((8, D), lambda l, j: (0, 0)),
                  pl.BlockSpec((1, D, tn), lambda l, j: (l, 0, j)),
                  pl.BlockSpec((1, 1, tn), lambda l, j: (l, 0, j))],
        out_specs=pl.BlockSpec((1, 8, tn), lambda l, j: (l, 0, j)),
        out_shape=jax.ShapeDtypeStruct((DEPTH, 8, 6 * D), F32),
        compiler_params=_cp(("arbitrary", "arbitrary")),
        name="ada_mods",
    )(cond8, ada_w, ada_b.reshape(DEPTH, 1, 6 * D))


def _even_in_kernel(x_ref, sc_ref, sh_ref, w_ref, wf_ref, u_ref, p_ref, q_ref):
    h = x_ref[...] * sc_ref[0] + sh_ref[0]
    z = jnp.dot(h.astype(BF16), w_ref[...], preferred_element_type=F32)
    u_ref[...] = z[:, :S5_W]
    for g in range(FN_G):
        f = z[:, S5_W + FN_C * g:S5_W + FN_C * (g + 1)]
        fc = f - jnp.mean(f, axis=-1, keepdims=True)
        fn = fc * lax.rsqrt(jnp.mean(fc * fc, axis=-1, keepdims=True) + LN_EPS)
        pq = jnp.dot(fn.astype(BF16), wf_ref[g], preferred_element_type=F32)
        p_ref[:, FN_C * g:FN_C * (g + 1)] = pq[:, :FN_C].astype(BF16)
        q_ref[:, FN_C * g:FN_C * (g + 1)] = pq[:, FN_C:].astype(BF16)


def _even_in(tok, sc1p, sh, w_in, wf, nctxb):
    nt = tok.shape[0]
    row = lambda i: (i, 0)
    return pl.pallas_call(
        _even_in_kernel,
        grid=(nt // TM,),
        in_specs=[pl.BlockSpec((TM, D), row),
                  pl.BlockSpec((1, 1, D), _rowtype(nctxb)),
                  pl.BlockSpec((1, 1, D), _rowtype(nctxb)),
                  pl.BlockSpec((D, D), lambda i: (0, 0)),
                  pl.BlockSpec((FN_G, FN_C, 2 * FN_C), lambda i: (0, 0, 0))],
        out_specs=[pl.BlockSpec((TM, S5_W), row),
                   pl.BlockSpec((TM, FN_W), row),
                   pl.BlockSpec((TM, FN_W), row)],
        out_shape=[jax.ShapeDtypeStruct((nt, S5_W), F32),
                   jax.ShapeDtypeStruct((nt, FN_W), BF16),
                   jax.ShapeDtypeStruct((nt, FN_W), BF16)],
        compiler_params=_cp(("arbitrary",)),
        name="even_in",
    )(tok, sc1p, sh, w_in, wf)


def _lane_iota(shape):
    return lax.broadcasted_iota(jnp.int32, shape, 1)


def _s5_in_kernel(x_ref, m1_ref, yi_ref, sfr_ref, sfi_ref, sbr_ref, sbi_ref):
    nc = x_ref.shape[1]
    lane = _lane_iota((nc, 128)) // 32
    for pp in range(4):
        @pl.when(pl.program_id(1) == pp)
        def _(pp=pp):
            tiles = []
            for q in range(4):
                acc = None
                for r in range(4):
                    xs = x_ref[4 * q + r]
                    shift = ((r - pp) * 32) % 128
                    if shift:
                        xs = pltpu.roll(xs, shift, 1)
                    acc = xs if acc is None else jnp.where(lane == r, xs, acc)
                tiles.append(acc.astype(BF16))
            u = jnp.concatenate(tiles, axis=1)
            r_all = jnp.dot(u, m1_ref[0], preferred_element_type=F32)
            yi_ref[0] = r_all[:, :512]
            sfr_ref[...] = r_all[:, 512:640]
            sfi_ref[...] = r_all[:, 640:768]
            sbr_ref[...] = r_all[:, 768:896]
            sbi_ref[...] = r_all[:, 896:1024]


def _s5_in(xs, m1):
    nc = xs.shape[1]
    sspec = pl.BlockSpec((nc, 128), lambda i, j: (0, 4 * i + j))
    sshape = jax.ShapeDtypeStruct((nc, 2048), F32)
    return pl.pallas_call(
        _s5_in_kernel,
        grid=(4, 4),
        in_specs=[pl.BlockSpec((CH, nc, 128), lambda i, j: (0, 0, i)),
                  pl.BlockSpec((1, 512, 1024), lambda i, j: (4 * i + j, 0, 0))],
        out_specs=[pl.BlockSpec((1, nc, 512), lambda i, j: (4 * i + j, 0, 0)), sspec, sspec, sspec, sspec],
        out_shape=[jax.ShapeDtypeStruct((16, nc, 512), F32), sshape, sshape, sshape, sshape],
        compiler_params=_cp(("arbitrary", "arbitrary")),
        name="s5_in",
    )(xs, m1)


def _s5_scan_kernel(ncc, sfr_ref, sfi_ref, sbr_ref, sbi_ref, a_ref,
                    hfr_ref, hfi_ref, hbr_ref, hbi_ref):
    nc = sfr_ref.shape[0]
    afr, afi, abr, abi = a_ref[0:1, :], a_ref[1:2, :], a_ref[2:3, :], a_ref[3:4, :]

    def body(i, carry):
        fr, fi, br, bi = carry
        jb = jnp.where(i < ncc, ncc - 1 - i, nc - 1 - i + ncc)
        hfr_ref[pl.ds(i, 1), :] = fr
        hfi_ref[pl.ds(i, 1), :] = fi
        hbr_ref[pl.ds(jb, 1), :] = br
        hbi_ref[pl.ds(jb, 1), :] = bi
        sr, si = sfr_ref[pl.ds(i, 1), :], sfi_ref[pl.ds(i, 1), :]
        tr, ti = sbr_ref[pl.ds(jb, 1), :], sbi_ref[pl.ds(jb, 1), :]
        return (afr * fr - afi * fi + sr, afr * fi + afi * fr + si,
                abr * br - abi * bi + tr, abr * bi + abi * br + ti)

    z = jnp.zeros((1, sfr_ref.shape[1]), F32)
    lax.fori_loop(0, nc, body, (z, z, z, z))


def _s5_scan(sfr, sfi, sbr, sbi, avec, ncc):
    nc = sfr.shape[0]
    spec = pl.BlockSpec((nc, 512), lambda i: (0, i))
    shape = jax.ShapeDtypeStruct((nc, 2048), F32)
    return pl.pallas_call(
        functools.partial(_s5_scan_kernel, ncc),
        grid=(4,),
        in_specs=[spec, spec, spec, spec, pl.BlockSpec((8, 512), lambda i: (0, i))],
        out_specs=[spec, spec, spec, spec],
        out_shape=[shape, shape, shape, shape],
        compiler_params=_cp(("arbitrary",)),
        name="s5_scan",
    )(sfr, sfi, sbr, sbi, avec)


def _s5_out_kernel(yi_ref, hfr_ref, hfi_ref, hbr_ref, hbi_ref, w2_ref, o_ref, yp_ref):
    nc = yi_ref.shape[1]
    pp_dyn = pl.program_id(1)
    hcat = jnp.concatenate([hfr_ref[...], hfi_ref[...], hbr_ref[...], hbi_ref[...]], axis=1).astype(BF16)
    yp_ref[pp_dyn] = yi_ref[0] + jnp.dot(hcat, w2_ref[0], preferred_element_type=F32)

    @pl.when(pp_dyn == 3)
    def _():
        lane = _lane_iota((nc, 128)) // 32
        for s in range(CH):
            acc = None
            for pp in range(4):
                ys = yp_ref[pp, :, (s // 4) * 128:(s // 4 + 1) * 128]
                shift = ((pp - s % 4) * 32) % 128
                if shift:
                    ys = pltpu.roll(ys, shift, 1)
                acc = ys if acc is None else jnp.where(lane == pp, ys, acc)
            o_ref[s] = acc


def _s5_out(yi, hfr, hfi, hbr, hbi, w2):
    nc = yi.shape[1]
    hspec = pl.BlockSpec((nc, 128), lambda i, j: (0, 4 * i + j))
    return pl.pallas_call(
        _s5_out_kernel,
        grid=(4, 4),
        in_specs=[pl.BlockSpec((1, nc, 512), lambda i, j: (4 * i + j, 0, 0)), hspec, hspec, hspec, hspec,
                  pl.BlockSpec((1, 512, 512), lambda i, j: (4 * i + j, 0, 0))],
        out_specs=pl.BlockSpec((CH, nc, 128), lambda i, j: (0, 0, i)),
        out_shape=jax.ShapeDtypeStruct((CH, nc, 512), F32),
        scratch_shapes=[pltpu.VMEM((4, nc, 512), F32)],
        compiler_params=_cp(("arbitrary", "arbitrary")),
        name="s5_out",
    )(yi, hfr, hfi, hbr, hbi, w2)


def _s5_matrices(lam_re, lam_im, log_dt, b_re, b_im, c_re, c_im, d_skip):
    lam = lax.complex(lam_re.astype(F32), lam_im.astype(F32))
    dt = jnp.exp(log_dt.astype(F32))[..., None]
    ldt = lam * dt
    lam_bar = jnp.exp(ldt)
    bb = ((lam_bar - 1.0) / lam)[..., None] * lax.complex(b_re.astype(F32), b_im.astype(F32))
    cc = lax.complex(c_re.astype(F32), c_im.astype(F32))
    ks = jnp.arange(CH + 1, dtype=F32)
    pw = jnp.exp(ldt[..., None] * ks)
    kk = jnp.real(jnp.einsum('dgcn,dgnk,dgne->dkgce', cc, pw[..., :CH], bb, precision=HIGHEST))
    k0 = kk[0, 0] + kk[1, 0] + jnp.eye(S5_C, dtype=F32)[None] * d_skip.astype(F32)[:, :, None]
    kall = jnp.concatenate([kk[1, 1:][::-1], k0[None], kk[0, 1:]], axis=0)
    idx = (jnp.arange(CH)[None, :] - jnp.arange(CH)[:, None]) + CH - 1
    toep = kall[idx]
    toep = toep.transpose(2, 0, 4, 1, 3)
    eye2 = jnp.eye(2, dtype=F32)
    tp = jnp.einsum('pgsctd,gh->psgcthd', toep.reshape(16, 2, CH, S5_C, CH, S5_C), eye2)
    tp = tp.reshape(16, 512, 512)
    w1f = jnp.einsum('gns,gnc->gscn', pw[0][..., :CH][..., ::-1], bb[0])
    w1b = jnp.einsum('gns,gnc->gscn', pw[1][..., :CH], bb[1])

    def pair_cols(w):
        w = w.reshape(16, 2, CH, S5_C, S5_N)
        return jnp.einsum('pgscn,gh->psgchn', w, eye2).reshape(16, 512, 128)

    m1 = jnp.concatenate([tp, pair_cols(jnp.real(w1f)), pair_cols(jnp.imag(w1f)),
                          pair_cols(jnp.real(w1b)), pair_cols(jnp.imag(w1b))], axis=-1)
    cpf = jnp.einsum('gcn,gnt->gntc', cc[0], pw[0][..., 1:])
    cpb = jnp.einsum('gcn,gnt->gntc', cc[1], pw[1][..., 1:][..., ::-1])

    def pair_rows(w):
        w = w.reshape(16, 2, S5_N, CH, S5_C)
        return jnp.einsum('pgntc,gh->pgnthc', w, eye2).reshape(16, 128, 512)

    w2 = jnp.concatenate([pair_rows(jnp.real(cpf)), pair_rows(-jnp.imag(cpf)),
                          pair_rows(jnp.real(cpb)), pair_rows(-jnp.imag(cpb))], axis=1)
    a_f, a_b = pw[0][..., CH].reshape(-1), pw[1][..., CH].reshape(-1)
    z = jnp.zeros_like(jnp.real(a_f))
    avec = jnp.stack([jnp.real(a_f), jnp.imag(a_f), jnp.real(a_b), jnp.imag(a_b), z, z, z, z])
    return m1.astype(BF16), w2.astype(BF16), avec


def _dft1_kernel(m_ref, p_ref, q_ref, z_ref):
    n1 = p_ref.shape[0]
    z_ref[...] = (jnp.dot(m_ref[:, :n1], p_ref[...], preferred_element_type=F32)
                  + jnp.dot(m_ref[:, n1:], q_ref[...], preferred_element_type=F32))


def _dft1(m, p3, q3):
    n1, cols = p3.shape
    cb = 4096
    return pl.pallas_call(
        _dft1_kernel,
        grid=(cols // cb,),
        in_specs=[pl.BlockSpec((2 * n1, 2 * n1), lambda j: (0, 0)),
                  pl.BlockSpec((n1, cb), lambda j: (0, j)),
                  pl.BlockSpec((n1, cb), lambda j: (0, j))],
        out_specs=pl.BlockSpec((2 * n1, cb), lambda j: (0, j)),
        out_shape=jax.ShapeDtypeStruct((2 * n1, cols), F32),
        compiler_params=_cp(("arbitrary",)),
        name="fnet_dft1",
    )(m, p3, q3)


def _dft2_kernel(scale, zr_ref, zi_ref, tr_ref, ti_ref, c_ref, s_ref, b_ref, o_ref):
    for kk in range(zr_ref.shape[0]):
        tr = jnp.concatenate([tr_ref[kk]] * 4, axis=1)
        ti = jnp.concatenate([ti_ref[kk]] * 4, axis=1)
        zr, zi = zr_ref[kk], zi_ref[kk]
        z2r = (zr * tr - zi * ti).astype(BF16)
        z2i = (zr * ti + zi * tr).astype(BF16)
        g = (jnp.dot(c_ref[...], z2r, preferred_element_type=F32)
             + jnp.dot(s_ref[...], z2i, preferred_element_type=F32))
        o_ref[kk] = g * scale + b_ref[...]


def _dft2(z3, twr, twi, c2, s2, bias, scale):
    n1x2, n2, w = z3.shape
    n1 = n1x2 // 2
    kb = 8
    nb = n1 // kb
    return pl.pallas_call(
        functools.partial(_dft2_kernel, scale),
        grid=(nb,),
        in_specs=[pl.BlockSpec((kb, n2, w), lambda i: (i, 0, 0)),
                  pl.BlockSpec((kb, n2, w), lambda i: (i + nb, 0, 0)),
                  pl.BlockSpec((kb, n2, 128), lambda i: (i, 0, 0)),
                  pl.BlockSpec((kb, n2, 128), lambda i: (i, 0, 0)),
                  pl.BlockSpec((n2, n2), lambda i: (0, 0)),
                  pl.BlockSpec((n2, n2), lambda i: (0, 0)),
                  pl.BlockSpec((1, w), lambda i: (0, 0))],
        out_specs=pl.BlockSpec((kb, n2, w), lambda i: (i, 0, 0)),
        out_shape=jax.ShapeDtypeStruct((n1, n2, w), F32),
        compiler_params=_cp(("arbitrary",)),
        name="fnet_dft2",
    )(z3, z3, twr, twi, c2, s2, bias)


def _dft_small_kernel(scale, c_ref, s_ref, p_ref, q_ref, b_ref, o_ref):
    g = (jnp.dot(c_ref[...], p_ref[...], preferred_element_type=F32)
         - jnp.dot(s_ref[...], q_ref[...], preferred_element_type=F32))
    o_ref[...] = g * scale + b_ref[...]


def _dft_small(cm, sm, p, q, bias, scale):
    n, w = p.shape
    full = lambda shape: pl.BlockSpec(shape, lambda i: (0,) * len(shape))
    return pl.pallas_call(
        functools.partial(_dft_small_kernel, scale),
        grid=(1,),
        in_specs=[full((n, n)), full((n, n)), full((n, w)), full((n, w)), full((1, w))],
        out_specs=full((n, w)),
        out_shape=jax.ShapeDtypeStruct((n, w), F32),
        compiler_params=_cp(("arbitrary",)),
        name="fnet_dft_ctx",
    )(cm, sm, p, q, bias)


def _cos_sin(n_rows, n_cols, period):
    ang = (2.0 * math.pi / period) * ((jnp.arange(n_rows)[:, None] * jnp.arange(n_cols)[None, :]) % period).astype(F32)
    return jnp.cos(ang), jnp.sin(ang)


def _fnet_positions(p, q, bias, n_ctx):
    pc, qc, pl_, ql_ = p[:n_ctx], q[:n_ctx], p[n_ctx:], q[n_ctx:]
    length = pl_.shape[0]
    n1 = 1 << ((length.bit_length() - 1) // 2)
    n2 = length // n1
    assert n1 * n2 == length and n1 % 8 == 0
    bias = bias.reshape(1, FN_W).astype(F32)
    cc, sc = _cos_sin(n_ctx, n_ctx, n_ctx)
    g_ctx = _dft_small(cc.astype(BF16), sc.astype(BF16), pc, qc, bias, 1.0 / math.sqrt(FN_C * n_ctx))
    c1, s1 = _cos_sin(n1, n1, n1)
    m = jnp.concatenate([jnp.concatenate([c1, -s1], axis=1),
                         jnp.concatenate([-s1, -c1], axis=1)], axis=0).astype(BF16)
    z = _dft1(m, pl_.reshape(n1, n2 * FN_W), ql_.reshape(n1, n2 * FN_W))
    twc, tws = _cos_sin(n1, n2, length)
    twr = jnp.broadcast_to(twc[:, :, None], (n1, n2, 128))
    twi = jnp.broadcast_to(-tws[:, :, None], (n1, n2, 128))
    c2, s2 = _cos_sin(n2, n2, n2)
    g = _dft2(z.reshape(2 * n1, n2, FN_W), twr, twi, c2.astype(BF16), s2.astype(BF16), bias,
              1.0 / math.sqrt(FN_C * length))
    g_lat = g.transpose(1, 0, 2).reshape(length, FN_W)
    return jnp.concatenate([g_ctx, g_lat], axis=0)


def _even_out_kernel(ys_ref, g_ref, x_ref, wglu_ref, bglu_ref, wout_ref, gate_ref, lg_ref, lb_ref, o_ref):
    ys = ys_ref[...]
    c0 = math.sqrt(2.0 / math.pi)
    y = 0.5 * ys * (1.0 + jnp.tanh(c0 * (ys + 0.044715 * (ys * ys * ys))))
    gl = jnp.dot(y.astype(BF16), wglu_ref[...], preferred_element_type=F32) + bglu_ref[...]
    y2 = y * jax.nn.sigmoid(gl)
    m = (jnp.dot(y2.astype(BF16), wout_ref[:S5_W, :], preferred_element_type=F32)
         + jnp.dot(g_ref[...].astype(BF16), wout_ref[S5_W:, :], preferred_element_type=F32))
    z = ALPHA * x_ref[...] + gate_ref[0] * m
    o_ref[...] = _layer_norm_rows(z, lg_ref[...], lb_ref[...])


def _even_out(ys, g, tok, w_glu, b_glu, w_out, gate, ln_g, ln_b, nctxb):
    nt = tok.shape[0]
    row = lambda i: (i, 0)
    const = lambda i: (0, 0)
    return pl.pallas_call(
        _even_out_kernel,
        grid=(nt // TM,),
        in_specs=[pl.BlockSpec((TM, S5_W), row), pl.BlockSpec((TM, FN_W), row), pl.BlockSpec((TM, D), row),
                  pl.BlockSpec((S5_W, S5_W), const), pl.BlockSpec((1, S5_W), const),
                  pl.BlockSpec((D, D), const), pl.BlockSpec((1, 1, D), _rowtype(nctxb)),
                  pl.BlockSpec((1, D), const), pl.BlockSpec((1, D), const)],
        out_specs=pl.BlockSpec((TM, D), row),
        out_shape=jax.ShapeDtypeStruct((nt, D), F32),
        compiler_params=_cp(("arbitrary",)),
        name="even_out",
    )(ys, g, tok, w_glu, b_glu, w_out, gate, ln_g, ln_b)


def _qkv_kernel(x_ref, sc_ref, sh_ref, w_ref, cos_ref, sin_ref, q_ref, k_ref, v_ref):
    h = (x_ref[...] * sc_ref[0] + sh_ref[0]).astype(BF16)
    cos, sin = cos_ref[...], sin_ref[...]
    z = jnp.dot(h, w_ref[...], preferred_element_type=F32)
    for hh in range(HEADS):
        sl = slice(hh * 128, (hh + 1) * 128)
        q = z[:, sl] * cos + z[:, D + hh * 128:D + (hh + 1) * 128] * sin
        k = z[:, 2 * D + hh * 128:2 * D + (hh + 1) * 128] * cos + z[:, 3 * D + hh * 128:3 * D + (hh + 1) * 128] * sin
        q_ref[:, sl] = (q * (HD ** -0.5)).astype(BF16)
        k_ref[:, sl] = k.astype(BF16)
    v_ref[...] = z[:, 4 * D:].astype(BF16)


def _qkv(tok, sc1p, sh, w5, cos, sin, nctxb):
    nt = tok.shape[0]
    row = lambda i: (i, 0)
    o = jax.ShapeDtypeStruct((nt, D), BF16)
    return pl.pallas_call(
        _qkv_kernel,
        grid=(nt // TM,),
        in_specs=[pl.BlockSpec((TM, D), row),
                  pl.BlockSpec((1, 1, D), _rowtype(nctxb)),
                  pl.BlockSpec((1, 1, D), _rowtype(nctxb)),
                  pl.BlockSpec((D, 5 * D), lambda i: (0, 0)),
                  pl.BlockSpec((TM, 128), row), pl.BlockSpec((TM, 128), row)],
        out_specs=[pl.BlockSpec((TM, D), row)] * 3,
        out_shape=[o, o, o],
        compiler_params=_cp(("arbitrary",)),
        name="qkv_rope",
    )(tok, sc1p, sh, w5, cos, sin)


def _attn_kernel(n_ctx, q_ref, k_ref, v_ref, lam_ref, gs_ref, o_ref, qq_ref, m_ref, l_ref, acc_ref):
    qi, ki = pl.program_id(1), pl.program_id(2)
    nk = pl.num_programs(2)
    tq = q_ref.shape[0]

    @pl.when(ki == 0)
    def _():
        q = q_ref[...]
        lane = lax.broadcasted_iota(jnp.int32, q.shape, 1)
        zero = jnp.zeros_like(q)
        qq_ref[:tq, :] = jnp.where(lane < HD, q, zero)
        qq_ref[tq:, :] = jnp.where(lane >= HD, q, zero)
        m_ref[...] = jnp.full(m_ref.shape, -jnp.inf, F32)
        l_ref[...] = jnp.zeros(l_ref.shape, F32)
        acc_ref[...] = jnp.zeros(acc_ref.shape, F32)

    def step(masked):
        s = lax.dot_general(qq_ref[...], k_ref[...], (((1,), (1,)), ((), ())), preferred_element_type=F32)
        if masked:
            col = lax.broadcasted_iota(jnp.int32, s.shape, 1)
            s = jnp.where(col < n_ctx, s, -jnp.inf)
        m_old = m_ref[...]
        m_new = jnp.maximum(m_old, jnp.max(s, axis=-1, keepdims=True))
        a = jnp.exp(m_old - m_new)
        p = jnp.exp(s - m_new)
        l_ref[...] = a * l_ref[...] + jnp.sum(p, axis=-1, keepdims=True)
        acc_ref[...] = a * acc_ref[...] + jnp.dot(p.astype(BF16), v_ref[...], preferred_element_type=F32)
        m_ref[...] = m_new

    @pl.when(qi > 0)
    def _():
        step(False)

    @pl.when(jnp.logical_and(qi == 0, ki == 0))
    def _():
        step(True)

    @pl.when(ki == nk - 1)
    def _():
        o12 = acc_ref[...] / l_ref[...]
        o = o12[:tq] - lam_ref[...] * o12[tq:]
        o = o * lax.rsqrt(jnp.mean(o * o, axis=-1, keepdims=True) + LN_EPS)
        o_ref[...] = (o * gs_ref[...]).astype(BF16)


def _attention(q, k, v, lamv, gsv, n_ctx, tk):
    nt = q.shape[0]
    tq = n_ctx
    const = lambda h, i, j: (0, 0)
    return pl.pallas_call(
        functools.partial(_attn_kernel, n_ctx),
        grid=(HEADS, nt // tq, nt // tk),
        in_specs=[pl.BlockSpec((tq, 128), lambda h, i, j: (i, h)),
                  pl.BlockSpec((tk, 128), lambda h, i, j: (j, h)),
                  pl.BlockSpec((tk, 128), lambda h, i, j: (j, h)),
                  pl.BlockSpec((1, 128), const), pl.BlockSpec((1, 128), const)],
        out_specs=pl.BlockSpec((tq, 128), lambda h, i, j: (i, h)),
        out_shape=jax.ShapeDtypeStruct((nt, D), BF16),
        scratch_shapes=[pltpu.VMEM((2 * tq, 128), BF16), pltpu.VMEM((2 * tq, 1), F32),
                        pltpu.VMEM((2 * tq, 1), F32), pltpu.VMEM((2 * tq, VD), F32)],
        compiler_params=_cp(("arbitrary", "arbitrary", "arbitrary")),
        name="diff_attention",
    )(q, k, v, lamv, gsv)


def _proj_ln_kernel(a_ref, w_ref, x_ref, gate_ref, lg_ref, lb_ref, o_ref):
    m = jnp.dot(a_ref[...], w_ref[...], preferred_element_type=F32)
    z = ALPHA * x_ref[...] + gate_ref[0] * m
    o_ref[...] = _layer_norm_rows(z, lg_ref[...], lb_ref[...])


def _proj_ln(a, w, tok, gate, ln_g, ln_b, nctxb):
    nt = tok.shape[0]
    row = lambda i: (i, 0)
    const = lambda i: (0, 0)
    return pl.pallas_call(
        _proj_ln_kernel,
        grid=(nt // TM,),
        in_specs=[pl.BlockSpec((TM, D), row), pl.BlockSpec((D, D), const), pl.BlockSpec((TM, D), row),
                  pl.BlockSpec((1, 1, D), _rowtype(nctxb)), pl.BlockSpec((1, D), const), pl.BlockSpec((1, D), const)],
        out_specs=pl.BlockSpec((TM, D), row),
        out_shape=jax.ShapeDtypeStruct((nt, D), F32),
        compiler_params=_cp(("arbitrary",)),
        name="attn_out_ln",
    )(a, w, tok, gate, ln_g, ln_b)


def _router_kernel(x_ref, sc_ref, sh_ref, wh_ref, wl_ref, b_ref, h_ref, lg_ref):
    h = x_ref[...] * sc_ref[0] + sh_ref[0]
    hh = h.astype(BF16)
    hl = (h - hh.astype(F32)).astype(BF16)
    h_ref[...] = hh
    lg_ref[...] = (jnp.dot(hh, wh_ref[...], preferred_element_type=F32)
                   + jnp.dot(hh, wl_ref[...], preferred_element_type=F32)
                   + jnp.dot(hl, wh_ref[...], preferred_element_type=F32)) + b_ref[...]


def _router(tok, sc1p, sh, w_hi, w_lo, bias, nctxb):
    nt = tok.shape[0]
    row = lambda i: (i, 0)
    const = lambda i: (0, 0)
    return pl.pallas_call(
        _router_kernel,
        grid=(nt // TM,),
        in_specs=[pl.BlockSpec((TM, D), row),
                  pl.BlockSpec((1, 1, D), _rowtype(nctxb)), pl.BlockSpec((1, 1, D), _rowtype(nctxb)),
                  pl.BlockSpec((D, 128), const), pl.BlockSpec((D, 128), const), pl.BlockSpec((1, 128), const)],
        out_specs=[pl.BlockSpec((TM, D), row), pl.BlockSpec((TM, 128), row)],
        out_shape=[jax.ShapeDtypeStruct((nt, D), BF16), jax.ShapeDtypeStruct((nt, 128), F32)],
        compiler_params=_cp(("arbitrary",)),
        name="moe_router",
    )(tok, sc1p, sh, w_hi, w_lo, bias)


def _expert_kernel(be_ref, nu_ref, x_ref, wg_ref, bg_ref, wu_ref, bu_ref, wd_ref, bd_ref, o_ref,
                   wgb_ref, wub_ref, wdb_ref):
    i = pl.program_id(0)
    prev = be_ref[jnp.maximum(i - 1, 0)]
    new_expert = jnp.logical_or(i == 0, be_ref[i] != prev)

    @pl.when(new_expert)
    def _():
        wgb_ref[...] = wg_ref[0].astype(BF16)
        wub_ref[...] = wu_ref[0].astype(BF16)
        wdb_ref[...] = wd_ref[0].astype(BF16)

    @pl.when(i < nu_ref[0])
    def _():
        x = x_ref[...]
        g = jnp.minimum(jnp.dot(x, wgb_ref[...], preferred_element_type=F32) + bg_ref[0], SWIGLU_LIMIT)
        u = jnp.clip(jnp.dot(x, wub_ref[...], preferred_element_type=F32) + bu_ref[0], -SWIGLU_LIMIT, SWIGLU_LIMIT)
        act = g * jax.nn.sigmoid(SWIGLU_ALPHA * g) * (u + 1.0)
        o_ref[...] = jnp.dot(act.astype(BF16), wdb_ref[...], preferred_element_type=F32) + bd_ref[0]

    @pl.when(i >= nu_ref[0])
    def _():
        o_ref[...] = jnp.zeros(o_ref.shape, F32)


def _experts(blk_e, n_used, xd, w_gate, b_gate, w_up, b_up, w_down, b_down):
    cap = xd.shape[0]
    wspec = pl.BlockSpec((1, D, D), lambda i, be, nu: (be[i], 0, 0))
    bspec = pl.BlockSpec((1, 1, D), lambda i, be, nu: (be[i], 0, 0))
    row = pl.BlockSpec((TME, D), lambda i, be, nu: (i, 0))
    return pl.pallas_call(
        _expert_kernel,
        grid_spec=pltpu.PrefetchScalarGridSpec(
            num_scalar_prefetch=2,
            grid=(cap // TME,),
            in_specs=[row, wspec, bspec, wspec, bspec, wspec, bspec],
            out_specs=row,
            scratch_shapes=[pltpu.VMEM((D, D), BF16)] * 3),
        out_shape=jax.ShapeDtypeStruct((cap, D), F32),
        compiler_params=_cp(("arbitrary",), 56),
        name="moe_experts",
    )(blk_e, n_used, xd, w_gate, b_gate.reshape(N_EXP, 1, D), w_up, b_up.reshape(N_EXP, 1, D),
      w_down, b_down.reshape(N_EXP, 1, D))


def _moe_finish_kernel(yg_ref, w_ref, x_ref, gate_ref, lg_ref, lb_ref, o_ref):
    w = w_ref[...]
    y = yg_ref[:, :D] * w[:, 0:1]
    for kk in range(1, TOP_K):
        y = y + yg_ref[:, kk * D:(kk + 1) * D] * w[:, kk:kk + 1]
    z = ALPHA * x_ref[...] + gate_ref[0] * y
    o_ref[...] = _layer_norm_rows(z, lg_ref[...], lb_ref[...])


def _moe_finish(yg, top_w, tok, gate, ln_g, ln_b, nctxb):
    nt = tok.shape[0]
    row = lambda i: (i, 0)
    const = lambda i: (0, 0)
    return pl.pallas_call(
        _moe_finish_kernel,
        grid=(nt // TM,),
        in_specs=[pl.BlockSpec((TM, TOP_K * D), row), pl.BlockSpec((TM, TOP_K), row), pl.BlockSpec((TM, D), row),
                  pl.BlockSpec((1, 1, D), _rowtype(nctxb)), pl.BlockSpec((1, D), const), pl.BlockSpec((1, D), const)],
        out_specs=pl.BlockSpec((TM, D), row),
        out_shape=jax.ShapeDtypeStruct((nt, D), F32),
        compiler_params=_cp(("arbitrary",)),
        name="moe_finish",
    )(yg, top_w, tok, gate, ln_g, ln_b)


def _moe_layer(tok, sc1p, sh, gate, ln_g, ln_b, w_router, b_router, w_gate, b_gate, w_up, b_up,
               w_down, b_down, nctxb):
    nt = tok.shape[0]
    wr = jnp.zeros((D, 128), F32).at[:, :N_EXP].set(w_router.astype(F32))
    wr_hi = wr.astype(BF16)
    wr_lo = (wr - wr_hi.astype(F32)).astype(BF16)
    br = jnp.zeros((1, 128), F32).at[0, :N_EXP].set(b_router.astype(F32))
    hb, logits = _router(tok, sc1p, sh, wr_hi, wr_lo, br, nctxb)
    top_val, top_idx = lax.top_k(logits[:, :N_EXP], TOP_K)
    top_w = jax.nn.softmax(top_val, axis=-1)
    onehot = jnp.sum((top_idx[:, :, None] == jnp.arange(N_EXP, dtype=jnp.int32)[None, None, :]).astype(jnp.int32), axis=1)
    csum = jnp.cumsum(onehot, axis=0)
    counts = csum[-1]
    padded = (counts + TME - 1) // TME * TME
    pad_end = jnp.cumsum(padded)
    pad_start = pad_end - padded
    rank = jnp.take_along_axis(csum - onehot, top_idx, axis=1)
    dest = (pad_start[top_idx] + rank).astype(jnp.int32)
    cap = nt * TOP_K + N_EXP * TME
    nb = cap // TME
    blk_e = jnp.minimum(jnp.searchsorted(pad_end, jnp.arange(nb, dtype=pad_end.dtype) * TME, side='right'),
                        N_EXP - 1).astype(jnp.int32)
    n_used = (pad_end[-1] // TME).astype(jnp.int32).reshape(1)
    tok_ids = jnp.broadcast_to(jnp.arange(nt, dtype=jnp.int32)[:, None], (nt, TOP_K))
    src = jnp.zeros((cap,), jnp.int32).at[dest.reshape(-1)].set(tok_ids.reshape(-1))
    xd = jnp.take(hb, src, axis=0)
    yd = _experts(blk_e, n_used, xd, w_gate, b_gate, w_up, b_up, w_down, b_down)
    yg = jnp.take(yd, dest.reshape(-1), axis=0).reshape(nt, TOP_K * D)
    return _moe_finish(yg, top_w, tok, gate, ln_g, ln_b, nctxb)


def _rope_tables(rows, n_ctx):
    row = jnp.broadcast_to(jnp.arange(rows, dtype=F32)[:, None], (rows, GRID_W)).reshape(-1)
    col = jnp.broadcast_to(jnp.arange(GRID_W, dtype=F32)[None, :], (rows, GRID_W)).reshape(-1)
    theta = ROPE_BASE ** (-jnp.arange(ROPE_F, dtype=F32) / ROPE_F)
    ang = jnp.stack([row[:, None] * theta, col[:, None] * theta], axis=1)
    ang = jnp.stack([ang, ang], axis=2).reshape(rows * GRID_W, HD)
    cos = jnp.concatenate([jnp.ones((n_ctx, HD), F32), jnp.cos(ang)], axis=0)
    sin = jnp.concatenate([jnp.zeros((n_ctx, HD), F32), jnp.sin(ang)], axis=0)
    return jnp.tile(cos, (1, 2)), jnp.tile(sin, (1, 2))


def _rot_cols(w):
    k = w.shape[0]
    wr = w.reshape(k, -1, 2, 2, ROPE_F)
    rot = jnp.stack([-wr[..., 1, :], wr[..., 0, :]], axis=-2)
    return rot.reshape(w.shape)


def kernel(x, c, ctx, c_ctx, ada_w, ada_b, ln_g, ln_b, even_w_in, s5_lam_re, s5_lam_im, s5_log_dt, s5_b_re, s5_b_im, s5_c_re, s5_c_im, s5_d, s5_w_glu, s5_b_glu, fnet_w, fnet_b, even_w_out, odd_w_qkv, odd_w_o, da_lq1, da_lk1, da_lq2, da_lk2, da_subln_g, router_w, router_b, moe_w_gate, moe_b_gate, moe_w_up, moe_b_up, moe_w_down, moe_b_down):
    seq = x.shape[1]
    n_ctx = ctx.shape[1]
    assert x.shape[0] == 1 and n_ctx % TM == 0 and seq % TM == 0 and n_ctx == TM
    nctxb = n_ctx // TM
    nt = n_ctx + seq
    nc = nt // CH
    ncc = n_ctx // CH
    tk = 1280 if nt % 1280 == 0 else 256

    tok = jnp.concatenate([ctx[0], x[0]], axis=0).astype(F32)
    cond8 = jnp.zeros((8, D), F32).at[0].set(c_ctx.astype(F32)).at[1].set(c[0].astype(F32))
    mods = _ada_mods(cond8, ada_w, ada_b)[:, :2].reshape(DEPTH, 2, 6, 1, D)
    cos, sin = _rope_tables(seq // GRID_W, n_ctx)
    c128, s128 = _cos_sin(FN_C, FN_C, FN_C)

    for l in range(DEPTH):
        i = l // 2
        m = mods[l]
        shift_a, scale_a, gate_a, shift_b, scale_b, gate_b = (m[:, j] for j in range(6))
        lg0, lb0 = ln_g[l, 0].reshape(1, D), ln_b[l, 0].reshape(1, D)
        lg1, lb1 = ln_g[l, 1].reshape(1, D), ln_b[l, 1].reshape(1, D)
        if l % 2 == 0:
            wf = jnp.concatenate([jnp.einsum('ab,gbd->gad', c128, fnet_w[i].astype(F32), precision=HIGHEST),
                                  jnp.einsum('ab,gbd->gad', s128, fnet_w[i].astype(F32), precision=HIGHEST)],
                                 axis=-1).astype(BF16)
            u, p, q = _even_in(tok, 1.0 + scale_a, shift_a, even_w_in[i].astype(BF16), wf, nctxb)
            m1, w2, avec = _s5_matrices(s5_lam_re[i], s5_lam_im[i], s5_log_dt[i], s5_b_re[i], s5_b_im[i],
                                        s5_c_re[i], s5_c_im[i], s5_d[i])
            xs = u.reshape(nc, CH, S5_W).transpose(1, 0, 2)
            yi, sfr, sfi, sbr, sbi = _s5_in(xs, m1)
            hfr, hfi, hbr, hbi = _s5_scan(sfr, sfi, sbr, sbi, avec, ncc)
            ys = _s5_out(yi, hfr, hfi, hbr, hbi, w2).transpose(1, 0, 2).reshape(nt, S5_W)
            g = _fnet_positions(p, q, fnet_b[i], n_ctx)
            tok = _even_out(ys, g, tok, s5_w_glu[i].astype(BF16), s5_b_glu[i].reshape(1, S5_W).astype(F32),
                            even_w_out[i].astype(BF16), gate_a, lg0, lb0, nctxb)
        else:
            lam_init = 0.8 - 0.6 * math.exp(-0.3 * l)
            lam = (jnp.exp(jnp.sum(da_lq1[i].astype(F32) * da_lk1[i].astype(F32)))
                   - jnp.exp(jnp.sum(da_lq2[i].astype(F32) * da_lk2[i].astype(F32))) + lam_init)
            wq, wk, wv = odd_w_qkv[i][:, :D], odd_w_qkv[i][:, D:2 * D], odd_w_qkv[i][:, 2 * D:]
            w5 = jnp.concatenate([wq, _rot_cols(wq), wk, _rot_cols(wk), wv], axis=1).astype(BF16)
            qb, kb, vb = _qkv(tok, 1.0 + scale_a, shift_a, w5, cos, sin, nctxb)
            lamv = jnp.full((1, VD), lam, F32)
            gsv = (da_subln_g[i].astype(F32) * (1.0 - lam_init)).reshape(1, VD)
            on = _attention(qb, kb, vb, lamv, gsv, n_ctx, tk)
            tok = _proj_ln(on, odd_w_o[i].astype(BF16), tok, gate_a, lg0, lb0, nctxb)
        tok = _moe_layer(tok, 1.0 + scale_b, shift_b, gate_b, lg1, lb1, router_w[l], router_b[l],
                         moe_w_gate[l], moe_b_gate[l], moe_w_up[l], moe_b_up[l], moe_w_down[l], moe_b_down[l],
                         nctxb)
    return tok[n_ctx:].reshape(1, seq, D).astype(x.dtype)
```

```python
import functools
import math

import jax
import jax.numpy as jnp
from jax import lax
from jax.experimental import pallas as pl
from jax.experimental.pallas import tpu as pltpu

F32 = jnp.float32
BF16 = jnp.bfloat16
HIGHEST = lax.Precision.HIGHEST

D = 1024
DEPTH = 4
GRID_W = 64
S5_W = 512
S5_G = 32
S5_C = 16
S5_N = 64
FN_W = 512
FN_G = 4
FN_C = 128
HEADS = 8
HD = 64
VD = 128
ROPE_BASE = 10000.0
ROPE_F = 16
N_EXP = 32
TOP_K = 4
SWIGLU_LIMIT = 7.0
SWIGLU_ALPHA = 1.702
ALPHA = (2.0 * DEPTH) ** 0.25
LN_EPS = 1e-5
LOG2E = 1.4426950408889634

TM = 256
TME = 256
CH = 16
MIB = 1024 * 1024


def _cp(sem, vmem_mib=48):
    return pltpu.CompilerParams(dimension_semantics=sem, vmem_limit_bytes=vmem_mib * MIB)


def _rowtype(nctxb):
    return lambda i: (jnp.where(i >= nctxb, 1, 0), 0, 0)


def _layer_norm_rows(z, g, b):
    mu = jnp.mean(z, axis=-1, keepdims=True)
    zc = z - mu
    var = jnp.mean(zc * zc, axis=-1, keepdims=True)
    return zc * lax.rsqrt(var + LN_EPS) * g + b


def _ada_kernel(c_ref, w_ref, b_ref, o_ref):
    c = c_ref[...]
    a = c * jax.nn.sigmoid(c)
    o_ref[0] = jnp.dot(a, w_ref[0], preferred_element_type=F32, precision=HIGHEST) + b_ref[0]


def _ada_mods(cond8, ada_w, ada_b):
    tn = 1536
    return pl.pallas_call(
        _ada_kernel,
        grid=(DEPTH, 6 * D // tn),
        in_specs=[pl.BlockSpec((8, D), lambda l, j: (0, 0)),
                  pl.BlockSpec((1, D, tn), lambda l, j: (l, 0, j)),
                  pl.BlockSpec((1, 1, tn), lambda l, j: (l, 0, j))],
        out_specs=pl.BlockSpec((1, 8, tn), lambda l, j: (l, 0, j)),
        out_shape=jax.ShapeDtypeStruct((DEPTH, 8, 6 * D), F32),
        compiler_params=_cp(("arbitrary", "arbitrary")),
        name="ada_mods",
    )(cond8, ada_w, ada_b.reshape(DEPTH, 1, 6 * D))


def _even_in_kernel(x_ref, sc_ref, sh_ref, w_ref, wf_ref, u_ref, p_ref, q_ref):
    h = x_ref[...] * sc_ref[0] + sh_ref[0]
    z = jnp.dot(h.astype(BF16), w_ref[...], preferred_element_type=F32)
    u_ref[...] = z[:, :S5_W]
    for g in range(FN_G):
        f = z[:, S5_W + FN_C * g:S5_W + FN_C * (g + 1)]
        fc = f - jnp.mean(f, axis=-1, keepdims=True)
        fn = fc * lax.rsqrt(jnp.mean(fc * fc, axis=-1, keepdims=True) + LN_EPS)
        pq = jnp.dot(fn.astype(BF16), wf_ref[g], preferred_element_type=F32)
        p_ref[:, FN_C * g:FN_C * (g + 1)] = pq[:, :FN_C].astype(BF16)
        q_ref[:, FN_C * g:FN_C * (g + 1)] = pq[:, FN_C:].astype(BF16)


def _even_in(tok, sc1p, sh, w_in, wf, nctxb):
    nt = tok.shape[0]
    row = lambda i: (i, 0)
    return pl.pallas_call(
        _even_in_kernel,
        grid=(nt // TM,),
        in_specs=[pl.BlockSpec((TM, D), row),
                  pl.BlockSpec((1, 1, D), _rowtype(nctxb)),
                  pl.BlockSpec((1, 1, D), _rowtype(nctxb)),
                  pl.BlockSpec((D, D), lambda i: (0, 0)),
                  pl.BlockSpec((FN_G, FN_C, 2 * FN_C), lambda i: (0, 0, 0))],
        out_specs=[pl.BlockSpec((TM, S5_W), row),
                   pl.BlockSpec((TM, FN_W), row),
                   pl.BlockSpec((TM, FN_W), row)],
        out_shape=[jax.ShapeDtypeStruct((nt, S5_W), F32),
                   jax.ShapeDtypeStruct((nt, FN_W), BF16),
                   jax.ShapeDtypeStruct((nt, FN_W), BF16)],
        compiler_params=_cp(("arbitrary",)),
        name="even_in",
    )(tok, sc1p, sh, w_in, wf)


def _lane_iota(shape):
    return lax.broadcasted_iota(jnp.int32, shape, 1)


def _s5_in_kernel(*refs):
    x_ref = refs[:CH]
    m1_ref, yi_ref, sfr_ref, sfi_ref, sbr_ref, sbi_ref = refs[CH:]
    nc = x_ref[0].shape[0]
    lane = _lane_iota((nc, 128)) // 32
    for pp in range(4):
        @pl.when(pl.program_id(1) == pp)
        def _(pp=pp):
            tiles = []
            for q in range(4):
                acc = None
                for r in range(4):
                    xs = x_ref[4 * q + r][...]
                    shift = ((r - pp) * 32) % 128
                    if shift:
                        xs = pltpu.roll(xs, shift, 1)
                    acc = xs if acc is None else jnp.where(lane == r, xs, acc)
                tiles.append(acc.astype(BF16))
            u = jnp.concatenate(tiles, axis=1)
            r_all = jnp.dot(u, m1_ref[0], preferred_element_type=F32)
            yi_ref[0] = r_all[:, :512]
            sfr_ref[...] = r_all[:, 512:640]
            sfi_ref[...] = r_all[:, 640:768]
            sbr_ref[...] = r_all[:, 768:896]
            sbi_ref[...] = r_all[:, 896:1024]


def _s5_in(u2, m1):
    nc = u2.shape[0]
    sspec = pl.BlockSpec((nc, 128), lambda i, j: (0, 4 * i + j))
    sshape = jax.ShapeDtypeStruct((nc, 2048), F32)
    xspecs = [pl.BlockSpec((nc, 128), functools.partial(lambda s, i, j: (0, 4 * s + i), s)) for s in range(CH)]
    return pl.pallas_call(
        _s5_in_kernel,
        grid=(4, 4),
        in_specs=xspecs + [pl.BlockSpec((1, 512, 1024), lambda i, j: (4 * i + j, 0, 0))],
        out_specs=[pl.BlockSpec((1, nc, 512), lambda i, j: (4 * i + j, 0, 0)), sspec, sspec, sspec, sspec],
        out_shape=[jax.ShapeDtypeStruct((16, nc, 512), F32), sshape, sshape, sshape, sshape],
        compiler_params=_cp(("arbitrary", "arbitrary")),
        name="s5_in",
    )(*([u2] * CH), m1)


def _s5_scan_kernel(ncc, sfr_ref, sfi_ref, sbr_ref, sbi_ref, a_ref,
                    hfr_ref, hfi_ref, hbr_ref, hbi_ref):
    nc = sfr_ref.shape[0]
    afr, afi, abr, abi = a_ref[0:1, :], a_ref[1:2, :], a_ref[2:3, :], a_ref[3:4, :]

    def body(i, carry):
        fr, fi, br, bi = carry
        jb = jnp.where(i < ncc, ncc - 1 - i, nc - 1 - i + ncc)
        hfr_ref[pl.ds(i, 1), :] = fr
        hfi_ref[pl.ds(i, 1), :] = fi
        hbr_ref[pl.ds(jb, 1), :] = br
        hbi_ref[pl.ds(jb, 1), :] = bi
        sr, si = sfr_ref[pl.ds(i, 1), :], sfi_ref[pl.ds(i, 1), :]
        tr, ti = sbr_ref[pl.ds(jb, 1), :], sbi_ref[pl.ds(jb, 1), :]
        return (afr * fr - afi * fi + sr, afr * fi + afi * fr + si,
                abr * br - abi * bi + tr, abr * bi + abi * br + ti)

    z = jnp.zeros((1, sfr_ref.shape[1]), F32)
    lax.fori_loop(0, nc, body, (z, z, z, z))


def _s5_scan(sfr, sfi, sbr, sbi, avec, ncc):
    nc = sfr.shape[0]
    spec = pl.BlockSpec((nc, 512), lambda i: (0, i))
    shape = jax.ShapeDtypeStruct((nc, 2048), F32)
    return pl.pallas_call(
        functools.partial(_s5_scan_kernel, ncc),
        grid=(4,),
        in_specs=[spec, spec, spec, spec, pl.BlockSpec((8, 512), lambda i: (0, i))],
        out_specs=[spec, spec, spec, spec],
        out_shape=[shape, shape, shape, shape],
        compiler_params=_cp(("arbitrary",)),
        name="s5_scan",
    )(sfr, sfi, sbr, sbi, avec)


def _s5_out_kernel(yi_ref, hfr_ref, hfi_ref, hbr_ref, hbi_ref, w2_ref, *rest):
    o_ref, yp_ref = rest[:CH], rest[CH]
    nc = yi_ref.shape[1]
    pp_dyn = pl.program_id(1)
    hcat = jnp.concatenate([hfr_ref[...], hfi_ref[...], hbr_ref[...], hbi_ref[...]], axis=1).astype(BF16)
    yp_ref[pp_dyn] = yi_ref[0] + jnp.dot(hcat, w2_ref[0], preferred_element_type=F32)

    @pl.when(pp_dyn == 3)
    def _():
        lane = _lane_iota((nc, 128)) // 32
        for s in range(CH):
            acc = None
            for pp in range(4):
                ys = yp_ref[pp, :, (s // 4) * 128:(s // 4 + 1) * 128]
                shift = ((pp - s % 4) * 32) % 128
                if shift:
                    ys = pltpu.roll(ys, shift, 1)
                acc = ys if acc is None else jnp.where(lane == pp, ys, acc)
            o_ref[s][...] = acc


def _s5_out(yi, hfr, hfi, hbr, hbi, w2):
    nc = yi.shape[1]
    hspec = pl.BlockSpec((nc, 128), lambda i, j: (0, 4 * i + j))
    ospec = pl.BlockSpec((nc, 128), lambda i, j: (0, i))
    return pl.pallas_call(
        _s5_out_kernel,
        grid=(4, 4),
        in_specs=[pl.BlockSpec((1, nc, 512), lambda i, j: (4 * i + j, 0, 0)), hspec, hspec, hspec, hspec,
                  pl.BlockSpec((1, 512, 512), lambda i, j: (4 * i + j, 0, 0))],
        out_specs=[ospec] * CH,
        out_shape=[jax.ShapeDtypeStruct((nc, S5_W), F32)] * CH,
        scratch_shapes=[pltpu.VMEM((4, nc, 512), F32)],
        compiler_params=_cp(("arbitrary", "arbitrary")),
        name="s5_out",
    )(yi, hfr, hfi, hbr, hbi, w2)


def _s5_matrices(lam_re, lam_im, log_dt, b_re, b_im, c_re, c_im, d_skip):
    lam = lax.complex(lam_re.astype(F32), lam_im.astype(F32))
    dt = jnp.exp(log_dt.astype(F32))[..., None]
    ldt = lam * dt
    lam_bar = jnp.exp(ldt)
    bb = ((lam_bar - 1.0) / lam)[..., None] * lax.complex(b_re.astype(F32), b_im.astype(F32))
    cc = lax.complex(c_re.astype(F32), c_im.astype(F32))
    ks = jnp.arange(CH + 1, dtype=F32)
    pw = jnp.exp(ldt[..., None] * ks)
    kk = jnp.real(jnp.einsum('dgcn,dgnk,dgne->dkgce', cc, pw[..., :CH], bb, precision=HIGHEST))
    k0 = kk[0, 0] + kk[1, 0] + jnp.eye(S5_C, dtype=F32)[None] * d_skip.astype(F32)[:, :, None]
    kall = jnp.concatenate([kk[1, 1:][::-1], k0[None], kk[0, 1:]], axis=0)
    idx = (jnp.arange(CH)[None, :] - jnp.arange(CH)[:, None]) + CH - 1
    toep = kall[idx]
    toep = toep.transpose(2, 0, 4, 1, 3)
    eye2 = jnp.eye(2, dtype=F32)
    tp = jnp.einsum('pgsctd,gh->psgcthd', toep.reshape(16, 2, CH, S5_C, CH, S5_C), eye2)
    tp = tp.reshape(16, 512, 512)
    w1f = jnp.einsum('gns,gnc->gscn', pw[0][..., :CH][..., ::-1], bb[0])
    w1b = jnp.einsum('gns,gnc->gscn', pw[1][..., :CH], bb[1])

    def pair_cols(w):
        w = w.reshape(16, 2, CH, S5_C, S5_N)
        return jnp.einsum('pgscn,gh->psgchn', w, eye2).reshape(16, 512, 128)

    m1 = jnp.concatenate([tp, pair_cols(jnp.real(w1f)), pair_cols(jnp.imag(w1f)),
                          pair_cols(jnp.real(w1b)), pair_cols(jnp.imag(w1b))], axis=-1)
    cpf = jnp.einsum('gcn,gnt->gntc', cc[0], pw[0][..., 1:])
    cpb = jnp.einsum('gcn,gnt->gntc', cc[1], pw[1][..., 1:][..., ::-1])

    def pair_rows(w):
        w = w.reshape(16, 2, S5_N, CH, S5_C)
        return jnp.einsum('pgntc,gh->pgnthc', w, eye2).reshape(16, 128, 512)

    w2 = jnp.concatenate([pair_rows(jnp.real(cpf)), pair_rows(-jnp.imag(cpf)),
                          pair_rows(jnp.real(cpb)), pair_rows(-jnp.imag(cpb))], axis=1)
    a_f, a_b = pw[0][..., CH].reshape(-1), pw[1][..., CH].reshape(-1)
    z = jnp.zeros_like(jnp.real(a_f))
    avec = jnp.stack([jnp.real(a_f), jnp.imag(a_f), jnp.real(a_b), jnp.imag(a_b), z, z, z, z])
    return m1.astype(BF16), w2.astype(BF16), avec


def _dft1_kernel(m_ref, p_ref, q_ref, z_ref):
    n1 = p_ref.shape[0]
    z_ref[...] = (jnp.dot(m_ref[:, :n1], p_ref[...], preferred_element_type=F32)
                  + jnp.dot(m_ref[:, n1:], q_ref[...], preferred_element_type=F32))


def _dft1(m, p3, q3):
    n1, cols = p3.shape
    cb = 4096
    return pl.pallas_call(
        _dft1_kernel,
        grid=(cols // cb,),
        in_specs=[pl.BlockSpec((2 * n1, 2 * n1), lambda j: (0, 0)),
                  pl.BlockSpec((n1, cb), lambda j: (0, j)),
                  pl.BlockSpec((n1, cb), lambda j: (0, j))],
        out_specs=pl.BlockSpec((2 * n1, cb), lambda j: (0, j)),
        out_shape=jax.ShapeDtypeStruct((2 * n1, cols), F32),
        compiler_params=_cp(("arbitrary",)),
        name="fnet_dft1",
    )(m, p3, q3)


def _dft2_kernel(scale, zr_ref, zi_ref, tr_ref, ti_ref, c_ref, s_ref, b_ref, o_ref):
    for kk in range(zr_ref.shape[0]):
        tr = jnp.concatenate([tr_ref[kk]] * 4, axis=1)
        ti = jnp.concatenate([ti_ref[kk]] * 4, axis=1)
        zr, zi = zr_ref[kk], zi_ref[kk]
        z2r = (zr * tr - zi * ti).astype(BF16)
        z2i = (zr * ti + zi * tr).astype(BF16)
        g = (jnp.dot(c_ref[...], z2r, preferred_element_type=F32)
             + jnp.dot(s_ref[...], z2i, preferred_element_type=F32))
        o_ref[:, kk * FN_W:(kk + 1) * FN_W] = g * scale + b_ref[...]


def _dft2(z3, twr, twi, c2, s2, bias, scale):
    n1x2, n2, w = z3.shape
    n1 = n1x2 // 2
    kb = 8
    nb = n1 // kb
    return pl.pallas_call(
        functools.partial(_dft2_kernel, scale),
        grid=(nb,),
        in_specs=[pl.BlockSpec((kb, n2, w), lambda i: (i, 0, 0)),
                  pl.BlockSpec((kb, n2, w), lambda i: (i + nb, 0, 0)),
                  pl.BlockSpec((kb, n2, 128), lambda i: (i, 0, 0)),
                  pl.BlockSpec((kb, n2, 128), lambda i: (i, 0, 0)),
                  pl.BlockSpec((n2, n2), lambda i: (0, 0)),
                  pl.BlockSpec((n2, n2), lambda i: (0, 0)),
                  pl.BlockSpec((1, w), lambda i: (0, 0))],
        out_specs=pl.BlockSpec((n2, kb * w), lambda i: (0, i)),
        out_shape=jax.ShapeDtypeStruct((n2, n1 * w), F32),
        compiler_params=_cp(("arbitrary",)),
        name="fnet_dft2",
    )(z3, z3, twr, twi, c2, s2, bias)


def _dft_small_kernel(scale, c_ref, s_ref, p_ref, q_ref, b_ref, o_ref):
    g = (jnp.dot(c_ref[...], p_ref[...], preferred_element_type=F32)
         - jnp.dot(s_ref[...], q_ref[...], preferred_element_type=F32))
    o_ref[...] = g * scale + b_ref[...]


def _dft_small(cm, sm, p, q, bias, scale):
    n, w = p.shape
    full = lambda shape: pl.BlockSpec(shape, lambda i: (0,) * len(shape))
    return pl.pallas_call(
        functools.partial(_dft_small_kernel, scale),
        grid=(1,),
        in_specs=[full((n, n)), full((n, n)), full((n, w)), full((n, w)), full((1, w))],
        out_specs=full((n, w)),
        out_shape=jax.ShapeDtypeStruct((n, w), F32),
        compiler_params=_cp(("arbitrary",)),
        name="fnet_dft_ctx",
    )(cm, sm, p, q, bias)


def _cos_sin(n_rows, n_cols, period):
    ang = (2.0 * math.pi / period) * ((jnp.arange(n_rows)[:, None] * jnp.arange(n_cols)[None, :]) % period).astype(F32)
    return jnp.cos(ang), jnp.sin(ang)


def _fnet_positions(p, q, bias, n_ctx):
    pc, qc, pl_, ql_ = p[:n_ctx], q[:n_ctx], p[n_ctx:], q[n_ctx:]
    length = pl_.shape[0]
    n1 = 1 << ((length.bit_length() - 1) // 2)
    n2 = length // n1
    assert n1 * n2 == length and n1 % 8 == 0
    bias = bias.reshape(1, FN_W).astype(F32)
    cc, sc = _cos_sin(n_ctx, n_ctx, n_ctx)
    g_ctx = _dft_small(cc.astype(BF16), sc.astype(BF16), pc, qc, bias, 1.0 / math.sqrt(FN_C * n_ctx))
    c1, s1 = _cos_sin(n1, n1, n1)
    m = jnp.concatenate([jnp.concatenate([c1, -s1], axis=1),
                         jnp.concatenate([-s1, -c1], axis=1)], axis=0).astype(BF16)
    z = _dft1(m, pl_.reshape(n1, n2 * FN_W), ql_.reshape(n1, n2 * FN_W))
    twc, tws = _cos_sin(n1, n2, length)
    twr = jnp.broadcast_to(twc[:, :, None], (n1, n2, 128))
    twi = jnp.broadcast_to(-tws[:, :, None], (n1, n2, 128))
    c2, s2 = _cos_sin(n2, n2, n2)
    g = _dft2(z.reshape(2 * n1, n2, FN_W), twr, twi, c2.astype(BF16), s2.astype(BF16), bias,
              1.0 / math.sqrt(FN_C * length))
    g_lat = g.reshape(length, FN_W)
    return jnp.concatenate([g_ctx, g_lat], axis=0)


def _even_out_kernel(ys_ref, g_ref, x_ref, wglu_ref, bglu_ref, wout_ref, gate_ref, lg_ref, lb_ref, o_ref):
    ys = ys_ref[...]
    c0 = math.sqrt(2.0 / math.pi)
    y = 0.5 * ys * (1.0 + jnp.tanh(c0 * (ys + 0.044715 * (ys * ys * ys))))
    gl = jnp.dot(y.astype(BF16), wglu_ref[...], preferred_element_type=F32) + bglu_ref[...]
    y2 = y * jax.nn.sigmoid(gl)
    m = (jnp.dot(y2.astype(BF16), wout_ref[:S5_W, :], preferred_element_type=F32)
         + jnp.dot(g_ref[...].astype(BF16), wout_ref[S5_W:, :], preferred_element_type=F32))
    z = ALPHA * x_ref[...] + gate_ref[0] * m
    o_ref[...] = _layer_norm_rows(z, lg_ref[...], lb_ref[...])


def _even_out(ys, g, tok, w_glu, b_glu, w_out, gate, ln_g, ln_b, nctxb):
    nt = tok.shape[0]
    row = lambda i: (i, 0)
    const = lambda i: (0, 0)
    return pl.pallas_call(
        _even_out_kernel,
        grid=(nt // TM,),
        in_specs=[pl.BlockSpec((TM, S5_W), row), pl.BlockSpec((TM, FN_W), row), pl.BlockSpec((TM, D), row),
                  pl.BlockSpec((S5_W, S5_W), const), pl.BlockSpec((1, S5_W), const),
                  pl.BlockSpec((D, D), const), pl.BlockSpec((1, 1, D), _rowtype(nctxb)),
                  pl.BlockSpec((1, D), const), pl.BlockSpec((1, D), const)],
        out_specs=pl.BlockSpec((TM, D), row),
        out_shape=jax.ShapeDtypeStruct((nt, D), F32),
        compiler_params=_cp(("arbitrary",)),
        name="even_out",
    )(ys, g, tok, w_glu, b_glu, w_out, gate, ln_g, ln_b)


def _qkv_kernel(x_ref, sc_ref, sh_ref, w_ref, cos_ref, sin_ref, q_ref, k_ref, v_ref):
    h = (x_ref[...] * sc_ref[0] + sh_ref[0]).astype(BF16)
    cos, sin = cos_ref[...], sin_ref[...]
    z = jnp.dot(h, w_ref[...], preferred_element_type=F32)
    for hh in range(HEADS):
        sl = slice(hh * 128, (hh + 1) * 128)
        q = z[:, sl] * cos + z[:, D + hh * 128:D + (hh + 1) * 128] * sin
        k = z[:, 2 * D + hh * 128:2 * D + (hh + 1) * 128] * cos + z[:, 3 * D + hh * 128:3 * D + (hh + 1) * 128] * sin
        q_ref[:, sl] = (q * (HD ** -0.5 * LOG2E)).astype(BF16)
        k_ref[:, sl] = k.astype(BF16)
        v_ref[:, 2 * hh * VD:(2 * hh + 1) * VD] = z[:, 4 * D + hh * VD:4 * D + (hh + 1) * VD].astype(BF16)
        v_ref[:, (2 * hh + 1) * VD:(2 * hh + 2) * VD] = jnp.ones((z.shape[0], VD), BF16)


def _qkv(tok, sc1p, sh, w5, cos, sin, nctxb):
    nt = tok.shape[0]
    row = lambda i: (i, 0)
    o = jax.ShapeDtypeStruct((nt, D), BF16)
    ov = jax.ShapeDtypeStruct((nt, 2 * D), BF16)
    return pl.pallas_call(
        _qkv_kernel,
        grid=(nt // TM,),
        in_specs=[pl.BlockSpec((TM, D), row),
                  pl.BlockSpec((1, 1, D), _rowtype(nctxb)),
                  pl.BlockSpec((1, 1, D), _rowtype(nctxb)),
                  pl.BlockSpec((D, 5 * D), lambda i: (0, 0)),
                  pl.BlockSpec((TM, 128), row), pl.BlockSpec((TM, 128), row)],
        out_specs=[pl.BlockSpec((TM, D), row), pl.BlockSpec((TM, D), row), pl.BlockSpec((TM, 2 * D), row)],
        out_shape=[o, o, ov],
        compiler_params=_cp(("arbitrary",)),
        name="qkv_rope",
    )(tok, sc1p, sh, w5, cos, sin)


def _attn_kernel(n_head, ts, n_pairs, q_ref, k_ref, v_ref, lam_ref, gs_ref, o_ref,
                 qq_ref, sa_ref, sb_ref, m_ref, acc_ref):
    tq = q_ref.shape[0]
    q = q_ref[...]
    lane = lax.broadcasted_iota(jnp.int32, q.shape, 1)
    zero = jnp.zeros_like(q)
    qq_ref[:tq, :] = jnp.where(lane < HD, q, zero)
    qq_ref[tq:, :] = jnp.where(lane >= HD, q, zero)
    m_ref[...] = jnp.full(m_ref.shape, -1e30, F32)
    acc_ref[...] = jnp.zeros(acc_ref.shape, F32)

    def scores(off, size, dst_ref):
        dst_ref[:, :size] = lax.dot_general(qq_ref[...], k_ref[pl.ds(off, size), :],
                                            (((1,), (1,)), ((), ())), preferred_element_type=F32)

    def consume(off, size, src_ref):
        tiles = [src_ref[:, t * 128:(t + 1) * 128] for t in range(size // 128)]
        mx = functools.reduce(jnp.maximum, tiles)
        m_old = m_ref[...]
        m_new = jnp.maximum(m_old, jnp.max(mx, axis=-1, keepdims=True))
        alpha = jnp.exp2(m_old - m_new)
        p = jnp.concatenate([jnp.exp2(t - m_new).astype(BF16) for t in tiles], axis=1)
        pv = jnp.dot(p, v_ref[pl.ds(off, size), :], preferred_element_type=F32)
        acc_ref[...] = jnp.concatenate([alpha, alpha], axis=1) * acc_ref[...] + pv
        m_ref[...] = m_new

    scores(0, n_head, sa_ref)
    if n_pairs:
        scores(n_head, ts, sb_ref)
    consume(0, n_head, sa_ref)

    def pair(i, last):
        off0 = pl.multiple_of(n_head + (2 * i) * ts, 128)
        off1 = pl.multiple_of(n_head + (2 * i + 1) * ts, 128)
        scores(off1, ts, sa_ref)
        consume(off0, ts, sb_ref)
        if not last:
            scores(pl.multiple_of(n_head + (2 * i + 2) * ts, 128), ts, sb_ref)
        consume(off1, ts, sa_ref)

    if n_pairs:
        def body(i, carry):
            pair(i, False)
            return carry
        lax.fori_loop(0, n_pairs - 1, body, 0)
        pair(n_pairs - 1, True)

    acc = acc_ref[...]
    o12 = acc[:, :VD] / acc[:, VD:]
    o = o12[:tq] - lam_ref[...] * o12[tq:]
    o = o * lax.rsqrt(jnp.mean(o * o, axis=-1, keepdims=True) + LN_EPS)
    o_ref[...] = (o * gs_ref[...]).astype(BF16)


def _attention(q, k, vext, lamv, gsv, tq, n_head, ts):
    nq, nk = q.shape[0], k.shape[0]
    n_sub = (nk - n_head) // ts if ts else 0
    assert n_head + n_sub * ts == nk and n_sub % 2 == 0 and nq % tq == 0
    const = lambda h, i: (0, 0)
    return pl.pallas_call(
        functools.partial(_attn_kernel, n_head, ts, n_sub // 2),
        grid=(HEADS, nq // tq),
        in_specs=[pl.BlockSpec((tq, 128), lambda h, i: (i, h)),
                  pl.BlockSpec((nk, 128), lambda h, i: (0, h)),
                  pl.BlockSpec((nk, 2 * VD), lambda h, i: (0, h)),
                  pl.BlockSpec((1, 128), const), pl.BlockSpec((1, 128), const)],
        out_specs=pl.BlockSpec((tq, 128), lambda h, i: (i, h)),
        out_shape=jax.ShapeDtypeStruct((nq, D), BF16),
        scratch_shapes=[pltpu.VMEM((2 * tq, 128), BF16),
                        pltpu.VMEM((2 * tq, max(ts, n_head)), F32), pltpu.VMEM((2 * tq, max(ts, n_head)), F32),
                        pltpu.VMEM((2 * tq, 128), F32), pltpu.VMEM((2 * tq, 2 * VD), F32)],
        compiler_params=_cp(("arbitrary", "arbitrary")),
        name="diff_attention",
    )(q, k, vext, lamv, gsv)


def _proj_ln_kernel(a_ref, w_ref, x_ref, gate_ref, lg_ref, lb_ref, o_ref):
    m = jnp.dot(a_ref[...], w_ref[...], preferred_element_type=F32)
    z = ALPHA * x_ref[...] + gate_ref[0] * m
    o_ref[...] = _layer_norm_rows(z, lg_ref[...], lb_ref[...])


def _proj_ln(a, w, tok, gate, ln_g, ln_b, nctxb):
    nt = tok.shape[0]
    row = lambda i: (i, 0)
    const = lambda i: (0, 0)
    return pl.pallas_call(
        _proj_ln_kernel,
        grid=(nt // TM,),
        in_specs=[pl.BlockSpec((TM, D), row), pl.BlockSpec((D, D), const), pl.BlockSpec((TM, D), row),
                  pl.BlockSpec((1, 1, D), _rowtype(nctxb)), pl.BlockSpec((1, D), const), pl.BlockSpec((1, D), const)],
        out_specs=pl.BlockSpec((TM, D), row),
        out_shape=jax.ShapeDtypeStruct((nt, D), F32),
        compiler_params=_cp(("arbitrary",)),
        name="attn_out_ln",
    )(a, w, tok, gate, ln_g, ln_b)


def _router_kernel(x_ref, sc_ref, sh_ref, wh_ref, wl_ref, b_ref, tri_ref, h_ref, ri_ref, rw_ref, cnt_ref,
                   run_ref):
    @pl.when(pl.program_id(0) == 0)
    def _():
        run_ref[...] = jnp.zeros(run_ref.shape, F32)

    h = x_ref[...] * sc_ref[0] + sh_ref[0]
    hh = h.astype(BF16)
    hl = (h - hh.astype(F32)).astype(BF16)
    h_ref[...] = hh
    v = (jnp.dot(hh, wh_ref[...], preferred_element_type=F32)
         + jnp.dot(hh, wl_ref[...], preferred_element_type=F32)
         + jnp.dot(hl, wh_ref[...], preferred_element_type=F32)) + b_ref[...]
    lane = lax.broadcasted_iota(jnp.int32, v.shape, 1)
    lane_f = lane.astype(F32)
    vals, sels, idxs = [], [], []
    for _ in range(TOP_K):
        mk = jnp.max(v, axis=-1, keepdims=True)
        ik = jnp.min(jnp.where(v == mk, lane_f, 128.0), axis=-1, keepdims=True)
        sel = lane_f == ik
        v = jnp.where(sel, -jnp.inf, v)
        vals.append(mk)
        sels.append(sel)
        idxs.append(ik)
    es = [jnp.exp(mk - vals[0]) for mk in vals]
    den = es[0] + es[1] + es[2] + es[3]
    onehot = jnp.zeros(v.shape, F32)
    for sel in sels:
        onehot = jnp.where(sel, 1.0, onehot)
    prefix = jnp.dot(tri_ref[...], onehot.astype(BF16), preferred_element_type=F32)
    rank_all = prefix + run_ref[0:1, :]
    run_ref[0:1, :] = run_ref[0:1, :] + jnp.sum(onehot, axis=0, keepdims=True)
    ri = jnp.zeros(v.shape, F32)
    rw = jnp.zeros(v.shape, F32)
    for k in range(TOP_K):
        rk = jnp.sum(jnp.where(sels[k], rank_all, 0.0), axis=-1, keepdims=True)
        ri = jnp.where(lane == k, idxs[k], ri)
        ri = jnp.where(lane == TOP_K + k, rk, ri)
        rw = jnp.where(lane == k, es[k] / den, rw)
    ri_ref[...] = ri.astype(jnp.int32)
    rw_ref[...] = rw
    cnt_ref[...] = run_ref[...]


def _router(tok, sc1p, sh, w_hi, w_lo, bias, nctxb):
    nt = tok.shape[0]
    row = lambda i: (i, 0)
    const = lambda i: (0, 0)
    tri = (jnp.arange(TM)[:, None] > jnp.arange(TM)[None, :]).astype(BF16)
    return pl.pallas_call(
        _router_kernel,
        grid=(nt // TM,),
        in_specs=[pl.BlockSpec((TM, D), row),
                  pl.BlockSpec((1, 1, D), _rowtype(nctxb)), pl.BlockSpec((1, 1, D), _rowtype(nctxb)),
                  pl.BlockSpec((D, 128), const), pl.BlockSpec((D, 128), const), pl.BlockSpec((1, 128), const),
                  pl.BlockSpec((TM, TM), const)],
        out_specs=[pl.BlockSpec((TM, D), row), pl.BlockSpec((TM, 128), row), pl.BlockSpec((TM, 128), row),
                   pl.BlockSpec((8, 128), const)],
        out_shape=[jax.ShapeDtypeStruct((nt, D), BF16), jax.ShapeDtypeStruct((nt, 128), jnp.int32),
                   jax.ShapeDtypeStruct((nt, 128), F32), jax.ShapeDtypeStruct((8, 128), F32)],
        scratch_shapes=[pltpu.VMEM((8, 128), F32)],
        compiler_params=_cp(("arbitrary",)),
        name="moe_router",
    )(tok, sc1p, sh, w_hi, w_lo, bias, tri)


def _expert_kernel(be_ref, nu_ref, x_ref, wg_ref, bg_ref, wu_ref, bu_ref, wd_ref, bd_ref, o_ref,
                   wgb_ref, wub_ref, wdb_ref):
    i = pl.program_id(0)
    prev = be_ref[jnp.maximum(i - 1, 0)]
    new_expert = jnp.logical_or(i == 0, be_ref[i] != prev)

    @pl.when(new_expert)
    def _():
        wgb_ref[...] = wg_ref[...].astype(BF16)
        wub_ref[...] = wu_ref[...].astype(BF16)
        wdb_ref[...] = wd_ref[...].astype(BF16)

    @pl.when(i < nu_ref[0])
    def _():
        x = x_ref[...]
        g = jnp.minimum(jnp.dot(x, wgb_ref[...], preferred_element_type=F32) + bg_ref[...], SWIGLU_LIMIT)
        u = jnp.clip(jnp.dot(x, wub_ref[...], preferred_element_type=F32) + bu_ref[...], -SWIGLU_LIMIT, SWIGLU_LIMIT)
        act = g * jax.nn.sigmoid(SWIGLU_ALPHA * g) * (u + 1.0)
        y = jnp.dot(act.astype(BF16), wdb_ref[...], preferred_element_type=F32) + bd_ref[...]
        o_ref[...] = y.astype(BF16)

    @pl.when(i >= nu_ref[0])
    def _():
        o_ref[...] = jnp.zeros(o_ref.shape, BF16)


def _experts(layer, blk_e, n_used, xd, w_gate, b_gate, w_up, b_up, w_down, b_down):
    cap = xd.shape[0]
    wspec = pl.BlockSpec((None, None, D, D), lambda i, be, nu: (layer, be[i], 0, 0))
    bspec = pl.BlockSpec((None, None, 1, D), lambda i, be, nu: (layer, be[i], 0, 0))
    row = pl.BlockSpec((TME, D), lambda i, be, nu: (i, 0))
    b4 = lambda b: b.reshape(DEPTH, N_EXP, 1, D)
    return pl.pallas_call(
        _expert_kernel,
        grid_spec=pltpu.PrefetchScalarGridSpec(
            num_scalar_prefetch=2,
            grid=(cap // TME,),
            in_specs=[row, wspec, bspec, wspec, bspec, wspec, bspec],
            out_specs=row,
            scratch_shapes=[pltpu.VMEM((D, D), BF16)] * 3),
        out_shape=jax.ShapeDtypeStruct((cap, D), BF16),
        compiler_params=_cp(("arbitrary",), 56),
        name="moe_experts",
    )(blk_e, n_used, xd, w_gate, b4(b_gate), w_up, b4(b_up), w_down, b4(b_down))


def _moe_finish_kernel(yg_ref, w_ref, x_ref, gate_ref, lg_ref, lb_ref, o_ref):
    w = w_ref[...]
    y = yg_ref[:, :D].astype(F32) * w[:, 0:1]
    for kk in range(1, TOP_K):
        y = y + yg_ref[:, kk * D:(kk + 1) * D].astype(F32) * w[:, kk:kk + 1]
    z = ALPHA * x_ref[...] + gate_ref[0] * y
    o_ref[...] = _layer_norm_rows(z, lg_ref[...], lb_ref[...])


def _moe_finish(yg, top_w, tok, gate, ln_g, ln_b, nctxb):
    nt = tok.shape[0]
    row = lambda i: (i, 0)
    const = lambda i: (0, 0)
    return pl.pallas_call(
        _moe_finish_kernel,
        grid=(nt // TM,),
        in_specs=[pl.BlockSpec((TM, TOP_K * D), row), pl.BlockSpec((TM, TOP_K), row), pl.BlockSpec((TM, D), row),
                  pl.BlockSpec((1, 1, D), _rowtype(nctxb)), pl.BlockSpec((1, D), const), pl.BlockSpec((1, D), const)],
        out_specs=pl.BlockSpec((TM, D), row),
        out_shape=jax.ShapeDtypeStruct((nt, D), F32),
        compiler_params=_cp(("arbitrary",)),
        name="moe_finish",
    )(yg, top_w, tok, gate, ln_g, ln_b)


def _moe_layer(layer, tok, sc1p, sh, gate, ln_g, ln_b, w_router, b_router, w_gate, b_gate, w_up, b_up,
               w_down, b_down, nctxb):
    nt = tok.shape[0]
    wr = jnp.zeros((D, 128), F32).at[:, :N_EXP].set(w_router.astype(F32))
    wr_hi = wr.astype(BF16)
    wr_lo = (wr - wr_hi.astype(F32)).astype(BF16)
    br = jnp.full((1, 128), -1e30, F32).at[0, :N_EXP].set(b_router.astype(F32))
    hb, ri, rw, cnt = _router(tok, sc1p, sh, wr_hi, wr_lo, br, nctxb)
    top_idx, rank, top_w = ri[:, :TOP_K], ri[:, TOP_K:2 * TOP_K], rw[:, :TOP_K]
    counts = cnt[0, :N_EXP].astype(jnp.int32)
    padded = (counts + TME - 1) // TME * TME
    pad_end = jnp.cumsum(padded)
    pad_start = pad_end - padded
    dest = (pad_start[top_idx] + rank).astype(jnp.int32)
    cap = nt * TOP_K + N_EXP * TME
    nb = cap // TME
    blk_start = jnp.arange(nb, dtype=jnp.int32) * TME
    blk_e = jnp.minimum(jnp.sum((pad_end[None, :] <= blk_start[:, None]).astype(jnp.int32), axis=1), N_EXP - 1)
    n_used = (pad_end[-1] // TME).astype(jnp.int32).reshape(1)
    tok_ids = jnp.broadcast_to(jnp.arange(nt, dtype=jnp.int32)[:, None], (nt, TOP_K))
    src = jnp.zeros((cap,), jnp.int32).at[dest.reshape(-1)].set(tok_ids.reshape(-1))
    xd = jnp.take(hb, src, axis=0)
    yd = _experts(layer, blk_e, n_used, xd, w_gate, b_gate, w_up, b_up, w_down, b_down)
    yg = jnp.take(yd, dest.reshape(-1), axis=0).reshape(nt, TOP_K * D)
    return _moe_finish(yg, top_w, tok, gate, ln_g, ln_b, nctxb)


def _rope_tables(rows, n_ctx):
    row = jnp.broadcast_to(jnp.arange(rows, dtype=F32)[:, None], (rows, GRID_W)).reshape(-1)
    col = jnp.broadcast_to(jnp.arange(GRID_W, dtype=F32)[None, :], (rows, GRID_W)).reshape(-1)
    theta = ROPE_BASE ** (-jnp.arange(ROPE_F, dtype=F32) / ROPE_F)
    ang = jnp.stack([row[:, None] * theta, col[:, None] * theta], axis=1)
    ang = jnp.stack([ang, ang], axis=2).reshape(rows * GRID_W, HD)
    cos = jnp.concatenate([jnp.ones((n_ctx, HD), F32), jnp.cos(ang)], axis=0)
    sin = jnp.concatenate([jnp.zeros((n_ctx, HD), F32), jnp.sin(ang)], axis=0)
    return jnp.tile(cos, (1, 2)), jnp.tile(sin, (1, 2))


def _rot_cols(w):
    k = w.shape[0]
    wr = w.reshape(k, -1, 2, 2, ROPE_F)
    rot = jnp.stack([-wr[..., 1, :], wr[..., 0, :]], axis=-2)
    return rot.reshape(w.shape)


def kernel(x, c, ctx, c_ctx, ada_w, ada_b, ln_g, ln_b, even_w_in, s5_lam_re, s5_lam_im, s5_log_dt, s5_b_re, s5_b_im, s5_c_re, s5_c_im, s5_d, s5_w_glu, s5_b_glu, fnet_w, fnet_b, even_w_out, odd_w_qkv, odd_w_o, da_lq1, da_lk1, da_lq2, da_lk2, da_subln_g, router_w, router_b, moe_w_gate, moe_b_gate, moe_w_up, moe_b_up, moe_w_down, moe_b_down):
    seq = x.shape[1]
    n_ctx = ctx.shape[1]
    assert x.shape[0] == 1 and n_ctx % TM == 0 and seq % TM == 0 and n_ctx == TM
    nctxb = n_ctx // TM
    nt = n_ctx + seq
    nc = nt // CH
    ncc = n_ctx // CH
    tq_lat = 512 if seq % 512 == 0 else TM
    ts_lat = 512 if seq % 1024 == 0 else TM
    assert seq % (2 * ts_lat) == 0

    tok = jnp.concatenate([ctx[0], x[0]], axis=0).astype(F32)
    cond8 = jnp.zeros((8, D), F32).at[0].set(c_ctx.astype(F32)).at[1].set(c[0].astype(F32))
    mods = _ada_mods(cond8, ada_w, ada_b)[:, :2].reshape(DEPTH, 2, 6, 1, D)
    cos, sin = _rope_tables(seq // GRID_W, n_ctx)
    c128, s128 = _cos_sin(FN_C, FN_C, FN_C)

    for l in range(DEPTH):
        i = l // 2
        m = mods[l]
        shift_a, scale_a, gate_a, shift_b, scale_b, gate_b = (m[:, j] for j in range(6))
        lg0, lb0 = ln_g[l, 0].reshape(1, D), ln_b[l, 0].reshape(1, D)
        lg1, lb1 = ln_g[l, 1].reshape(1, D), ln_b[l, 1].reshape(1, D)
        if l % 2 == 0:
            wf = jnp.concatenate([jnp.einsum('ab,gbd->gad', c128, fnet_w[i].astype(F32), precision=HIGHEST),
                                  jnp.einsum('ab,gbd->gad', s128, fnet_w[i].astype(F32), precision=HIGHEST)],
                                 axis=-1).astype(BF16)
            u, p, q = _even_in(tok, 1.0 + scale_a, shift_a, even_w_in[i].astype(BF16), wf, nctxb)
            m1, w2, avec = _s5_matrices(s5_lam_re[i], s5_lam_im[i], s5_log_dt[i], s5_b_re[i], s5_b_im[i],
                                        s5_c_re[i], s5_c_im[i], s5_d[i])
            yi, sfr, sfi, sbr, sbi = _s5_in(u.reshape(nc, CH * S5_W), m1)
            hfr, hfi, hbr, hbi = _s5_scan(sfr, sfi, sbr, sbi, avec, ncc)
            ys = jnp.stack(_s5_out(yi, hfr, hfi, hbr, hbi, w2), axis=1).reshape(nt, S5_W)
            g = _fnet_positions(p, q, fnet_b[i], n_ctx)
            tok = _even_out(ys, g, tok, s5_w_glu[i].astype(BF16), s5_b_glu[i].reshape(1, S5_W).astype(F32),
                            even_w_out[i].astype(BF16), gate_a, lg0, lb0, nctxb)
        else:
            lam_init = 0.8 - 0.6 * math.exp(-0.3 * l)
            lam = (jnp.exp(jnp.sum(da_lq1[i].astype(F32) * da_lk1[i].astype(F32)))
                   - jnp.exp(jnp.sum(da_lq2[i].astype(F32) * da_lk2[i].astype(F32))) + lam_init)
            wq, wk, wv = odd_w_qkv[i][:, :D], odd_w_qkv[i][:, D:2 * D], odd_w_qkv[i][:, 2 * D:]
            w5 = jnp.concatenate([wq, _rot_cols(wq), wk, _rot_cols(wk), wv], axis=1).astype(BF16)
            qb, kb, vb = _qkv(tok, 1.0 + scale_a, shift_a, w5, cos, sin, nctxb)
            lamv = jnp.full((1, VD), lam, F32)
            gsv = (da_subln_g[i].astype(F32) * (1.0 - lam_init)).reshape(1, VD)
            on_lat = _attention(qb[n_ctx:], kb, vb, lamv, gsv, tq_lat, n_ctx, ts_lat)
            on_ctx = _attention(qb[:n_ctx], kb[:n_ctx], vb[:n_ctx], lamv, gsv, n_ctx, n_ctx, 0)
            on = jnp.concatenate([on_ctx, on_lat], axis=0)
            tok = _proj_ln(on, odd_w_o[i].astype(BF16), tok, gate_a, lg0, lb0, nctxb)
        tok = _moe_layer(l, tok, 1.0 + scale_b, shift_b, gate_b, lg1, lb1, router_w[l], router_b[l],
                         moe_w_gate, moe_b_gate, moe_w_up, moe_b_up, moe_w_down, moe_b_down, nctxb)
    return tok[n_ctx:].reshape(1, seq, D).astype(x.dtype)
```

```python
import functools
import math

import jax
import jax.numpy as jnp
from jax import lax
from jax.experimental import pallas as pl
from jax.experimental.pallas import tpu as pltpu

F32 = jnp.float32
BF16 = jnp.bfloat16
HIGHEST = lax.Precision.HIGHEST

D = 1024
DEPTH = 4
GRID_W = 64
S5_W = 512
S5_G = 32
S5_C = 16
S5_N = 64
FN_W = 512
FN_G = 4
FN_C = 128
HEADS = 8
HD = 64
VD = 128
ROPE_BASE = 10000.0
ROPE_F = 16
N_EXP = 32
TOP_K = 4
SWIGLU_LIMIT = 7.0
SWIGLU_ALPHA = 1.702
ALPHA = (2.0 * DEPTH) ** 0.25
LN_EPS = 1e-5
LOG2E = 1.4426950408889634

TM = 256
TME = 256
CH = 16
MIB = 1024 * 1024


def _cp(sem, vmem_mib=48):
    return pltpu.CompilerParams(dimension_semantics=sem, vmem_limit_bytes=vmem_mib * MIB)


def _rowtype(nctxb):
    return lambda i: (jnp.where(i >= nctxb, 1, 0), 0, 0)


def _layer_norm_rows(z, g, b):
    mu = jnp.mean(z, axis=-1, keepdims=True)
    zc = z - mu
    var = jnp.mean(zc * zc, axis=-1, keepdims=True)
    return zc * lax.rsqrt(var + LN_EPS) * g + b


def _ada_kernel(c_ref, w_ref, b_ref, o_ref):
    c = c_ref[...]
    a = c * jax.nn.sigmoid(c)
    o_ref[0] = jnp.dot(a, w_ref[0], preferred_element_type=F32, precision=HIGHEST) + b_ref[0]


def _ada_mods(cond8, ada_w, ada_b):
    tn = 1536
    return pl.pallas_call(
        _ada_kernel,
        grid=(DEPTH, 6 * D // tn),
        in_specs=[pl.BlockSpec((8, D), lambda l, j: (0, 0)),
                  pl.BlockSpec((1, D, tn), lambda l, j: (l, 0, j)),
                  pl.BlockSpec((1, 1, tn), lambda l, j: (l, 0, j))],
        out_specs=pl.BlockSpec((1, 8, tn), lambda l, j: (l, 0, j)),
        out_shape=jax.ShapeDtypeStruct((DEPTH, 8, 6 * D), F32),
        compiler_params=_cp(("arbitrary", "arbitrary")),
        name="ada_mods",
    )(cond8, ada_w, ada_b.reshape(DEPTH, 1, 6 * D))


def _even_in_kernel(x_ref, sc_ref, sh_ref, w_ref, wf_ref, u_ref, p_ref, q_ref):
    h = x_ref[...] * sc_ref[0] + sh_ref[0]
    z = jnp.dot(h.astype(BF16), w_ref[...], preferred_element_type=F32)
    u_ref[...] = z[:, :S5_W]
    for g in range(FN_G):
        f = z[:, S5_W + FN_C * g:S5_W + FN_C * (g + 1)]
        fc = f - jnp.mean(f, axis=-1, keepdims=True)
        fn = fc * lax.rsqrt(jnp.mean(fc * fc, axis=-1, keepdims=True) + LN_EPS)
        pq = jnp.dot(fn.astype(BF16), wf_ref[g], preferred_element_type=F32)
        p_ref[:, FN_C * g:FN_C * (g + 1)] = pq[:, :FN_C]
        q_ref[:, FN_C * g:FN_C * (g + 1)] = pq[:, FN_C:]


def _even_in(tok, sc1p, sh, w_in, wf, nctxb):
    nt = tok.shape[0]
    row = lambda i: (i, 0)
    return pl.pallas_call(
        _even_in_kernel,
        grid=(nt // TM,),
        in_specs=[pl.BlockSpec((TM, D), row),
                  pl.BlockSpec((1, 1, D), _rowtype(nctxb)),
                  pl.BlockSpec((1, 1, D), _rowtype(nctxb)),
                  pl.BlockSpec((D, D), lambda i: (0, 0)),
                  pl.BlockSpec((FN_G, FN_C, 2 * FN_C), lambda i: (0, 0, 0))],
        out_specs=[pl.BlockSpec((TM, S5_W), row),
                   pl.BlockSpec((TM, FN_W), row),
                   pl.BlockSpec((TM, FN_W), row)],
        out_shape=[jax.ShapeDtypeStruct((nt, S5_W), F32),
                   jax.ShapeDtypeStruct((nt, FN_W), F32),
                   jax.ShapeDtypeStruct((nt, FN_W), F32)],
        compiler_params=_cp(("arbitrary",)),
        name="even_in",
    )(tok, sc1p, sh, w_in, wf)


def _lane_iota(shape):
    return lax.broadcasted_iota(jnp.int32, shape, 1)


def _s5_in_kernel(x_ref, m1_ref, yi_ref, sfr_ref, sfi_ref, sbr_ref, sbi_ref):
    nc = x_ref.shape[0] // CH
    lane = _lane_iota((nc, 128)) // 32
    for pp in range(4):
        @pl.when(pl.program_id(1) == pp)
        def _(pp=pp):
            tiles = []
            for q in range(4):
                acc = None
                for r in range(4):
                    xs = x_ref[pl.ds(4 * q + r, nc, stride=CH), :]
                    shift = ((r - pp) * 32) % 128
                    if shift:
                        xs = pltpu.roll(xs, shift, 1)
                    acc = xs if acc is None else jnp.where(lane == r, xs, acc)
                tiles.append(acc.astype(BF16))
            u = jnp.concatenate(tiles, axis=1)
            r_all = jnp.dot(u, m1_ref[0], preferred_element_type=F32)
            yi_ref[0] = r_all[:, :512]
            sfr_ref[...] = r_all[:, 512:640]
            sfi_ref[...] = r_all[:, 640:768]
            sbr_ref[...] = r_all[:, 768:896]
            sbi_ref[...] = r_all[:, 896:1024]


def _s5_in(u, m1):
    nt = u.shape[0]
    nc = nt // CH
    sspec = pl.BlockSpec((nc, 128), lambda i, j: (0, 4 * i + j))
    sshape = jax.ShapeDtypeStruct((nc, 2048), F32)
    return pl.pallas_call(
        _s5_in_kernel,
        grid=(4, 4),
        in_specs=[pl.BlockSpec((nt, 128), lambda i, j: (0, i)),
                  pl.BlockSpec((1, 512, 1024), lambda i, j: (4 * i + j, 0, 0))],
        out_specs=[pl.BlockSpec((1, nc, 512), lambda i, j: (4 * i + j, 0, 0)), sspec, sspec, sspec, sspec],
        out_shape=[jax.ShapeDtypeStruct((16, nc, 512), F32), sshape, sshape, sshape, sshape],
        compiler_params=_cp(("arbitrary", "arbitrary")),
        name="s5_in",
    )(u, m1)


def _s5_scan_kernel(ncc, sfr_ref, sfi_ref, sbr_ref, sbi_ref, a_ref,
                    hfr_ref, hfi_ref, hbr_ref, hbi_ref):
    nc = sfr_ref.shape[0]
    afr, afi, abr, abi = a_ref[0:1, :], a_ref[1:2, :], a_ref[2:3, :], a_ref[3:4, :]

    def body(i, carry):
        fr, fi, br, bi = carry
        jb = jnp.where(i < ncc, ncc - 1 - i, nc - 1 - i + ncc)
        hfr_ref[pl.ds(i, 1), :] = fr
        hfi_ref[pl.ds(i, 1), :] = fi
        hbr_ref[pl.ds(jb, 1), :] = br
        hbi_ref[pl.ds(jb, 1), :] = bi
        sr, si = sfr_ref[pl.ds(i, 1), :], sfi_ref[pl.ds(i, 1), :]
        tr, ti = sbr_ref[pl.ds(jb, 1), :], sbi_ref[pl.ds(jb, 1), :]
        return (afr * fr - afi * fi + sr, afr * fi + afi * fr + si,
                abr * br - abi * bi + tr, abr * bi + abi * br + ti)

    z = jnp.zeros((1, sfr_ref.shape[1]), F32)
    lax.fori_loop(0, nc, body, (z, z, z, z))


def _s5_scan(sfr, sfi, sbr, sbi, avec, ncc):
    nc = sfr.shape[0]
    spec = pl.BlockSpec((nc, 512), lambda i: (0, i))
    shape = jax.ShapeDtypeStruct((nc, 2048), F32)
    return pl.pallas_call(
        functools.partial(_s5_scan_kernel, ncc),
        grid=(4,),
        in_specs=[spec, spec, spec, spec, pl.BlockSpec((8, 512), lambda i: (0, i))],
        out_specs=[spec, spec, spec, spec],
        out_shape=[shape, shape, shape, shape],
        compiler_params=_cp(("arbitrary",)),
        name="s5_scan",
    )(sfr, sfi, sbr, sbi, avec)


def _s5_out_kernel(yi_ref, hfr_ref, hfi_ref, hbr_ref, hbi_ref, w2_ref, o_ref, yp_ref):
    nc = yi_ref.shape[1]
    pp_dyn = pl.program_id(1)
    hcat = jnp.concatenate([hfr_ref[...], hfi_ref[...], hbr_ref[...], hbi_ref[...]], axis=1).astype(BF16)
    yp_ref[pp_dyn] = yi_ref[0] + jnp.dot(hcat, w2_ref[0], preferred_element_type=F32)

    @pl.when(pp_dyn == 3)
    def _():
        lane = _lane_iota((nc, 128)) // 32
        for s in range(CH):
            acc = None
            for pp in range(4):
                ys = yp_ref[pp, :, (s // 4) * 128:(s // 4 + 1) * 128]
                shift = ((pp - s % 4) * 32) % 128
                if shift:
                    ys = pltpu.roll(ys, shift, 1)
                acc = ys if acc is None else jnp.where(lane == pp, ys, acc)
            o_ref[pl.ds(s, nc, stride=CH), :] = acc


def _s5_out(yi, hfr, hfi, hbr, hbi, w2):
    nc = yi.shape[1]
    hspec = pl.BlockSpec((nc, 128), lambda i, j: (0, 4 * i + j))
    return pl.pallas_call(
        _s5_out_kernel,
        grid=(4, 4),
        in_specs=[pl.BlockSpec((1, nc, 512), lambda i, j: (4 * i + j, 0, 0)), hspec, hspec, hspec, hspec,
                  pl.BlockSpec((1, 512, 512), lambda i, j: (4 * i + j, 0, 0))],
        out_specs=pl.BlockSpec((nc * CH, 128), lambda i, j: (0, i)),
        out_shape=jax.ShapeDtypeStruct((nc * CH, S5_W), F32),
        scratch_shapes=[pltpu.VMEM((4, nc, 512), F32)],
        compiler_params=_cp(("arbitrary", "arbitrary")),
        name="s5_out",
    )(yi, hfr, hfi, hbr, hbi, w2)


def _s5_matrices(lam_re, lam_im, log_dt, b_re, b_im, c_re, c_im, d_skip):
    lam = lax.complex(lam_re.astype(F32), lam_im.astype(F32))
    dt = jnp.exp(log_dt.astype(F32))[..., None]
    ldt = lam * dt
    lam_bar = jnp.exp(ldt)
    bb = ((lam_bar - 1.0) / lam)[..., None] * lax.complex(b_re.astype(F32), b_im.astype(F32))
    cc = lax.complex(c_re.astype(F32), c_im.astype(F32))
    ks = jnp.arange(CH + 1, dtype=F32)
    pw = jnp.exp(ldt[..., None] * ks)
    kk = jnp.real(jnp.einsum('dgcn,dgnk,dgne->dkgce', cc, pw[..., :CH], bb, precision=HIGHEST))
    k0 = kk[0, 0] + kk[1, 0] + jnp.eye(S5_C, dtype=F32)[None] * d_skip.astype(F32)[:, :, None]
    kall = jnp.concatenate([kk[1, 1:][::-1], k0[None], kk[0, 1:]], axis=0)
    idx = (jnp.arange(CH)[None, :] - jnp.arange(CH)[:, None]) + CH - 1
    toep = kall[idx]
    toep = toep.transpose(2, 0, 4, 1, 3)
    eye2 = jnp.eye(2, dtype=F32)
    tp = jnp.einsum('pgsctd,gh->psgcthd', toep.reshape(16, 2, CH, S5_C, CH, S5_C), eye2)
    tp = tp.reshape(16, 512, 512)
    w1f = jnp.einsum('gns,gnc->gscn', pw[0][..., :CH][..., ::-1], bb[0])
    w1b = jnp.einsum('gns,gnc->gscn', pw[1][..., :CH], bb[1])

    def pair_cols(w):
        w = w.reshape(16, 2, CH, S5_C, S5_N)
        return jnp.einsum('pgscn,gh->psgchn', w, eye2).reshape(16, 512, 128)

    m1 = jnp.concatenate([tp, pair_cols(jnp.real(w1f)), pair_cols(jnp.imag(w1f)),
                          pair_cols(jnp.real(w1b)), pair_cols(jnp.imag(w1b))], axis=-1)
    cpf = jnp.einsum('gcn,gnt->gntc', cc[0], pw[0][..., 1:])
    cpb = jnp.einsum('gcn,gnt->gntc', cc[1], pw[1][..., 1:][..., ::-1])

    def pair_rows(w):
        w = w.reshape(16, 2, S5_N, CH, S5_C)
        return jnp.einsum('pgntc,gh->pgnthc', w, eye2).reshape(16, 128, 512)

    w2 = jnp.concatenate([pair_rows(jnp.real(cpf)), pair_rows(-jnp.imag(cpf)),
                          pair_rows(jnp.real(cpb)), pair_rows(-jnp.imag(cpb))], axis=1)
    a_f, a_b = pw[0][..., CH].reshape(-1), pw[1][..., CH].reshape(-1)
    z = jnp.zeros_like(jnp.real(a_f))
    avec = jnp.stack([jnp.real(a_f), jnp.imag(a_f), jnp.real(a_b), jnp.imag(a_b), z, z, z, z])
    return m1.astype(BF16), w2.astype(BF16), avec


def _dft_lat_kernel(n1, n2, scale, p_ref, q_ref, m_ref, c2_ref, s2_ref, rot_ref, b_ref, o_ref, zr_ref, zi_ref):
    length = n1 * n2
    rr, ri = rot_ref[0], rot_ref[1]

    def phase1(blk, tw):
        b0 = pl.multiple_of(blk * 8, 8)
        pv = p_ref.at[pl.ds(b0, length - n2 + 8), :]
        qv = q_ref.at[pl.ds(b0, length - n2 + 8), :]
        zrv = zr_ref.at[pl.ds(b0, length - n2 + 8), :]
        ziv = zi_ref.at[pl.ds(b0, length - n2 + 8), :]
        twr, twi = tw
        for j in range(8):
            rows = pl.ds(j, n1, stride=n2)
            x = jnp.concatenate([pv[rows, :], qv[rows, :]], axis=0).astype(BF16)
            z = jnp.dot(m_ref[...], x, preferred_element_type=F32)
            zr, zi = z[:n1], z[n1:]
            zrv[rows, :] = zr * twr - zi * twi
            ziv[rows, :] = zr * twi + zi * twr
            twr, twi = twr * rr - twi * ri, twr * ri + twi * rr
        return twr, twi

    lax.fori_loop(0, n2 // 8, phase1, (jnp.ones((n1, 128), F32), jnp.zeros((n1, 128), F32)))

    def phase2(blk, carry):
        k0 = pl.multiple_of(blk * 8, 8)
        ov = o_ref.at[pl.ds(k0, length - n1 + 8), :]
        for j in range(8):
            r0 = pl.multiple_of((k0 + j) * n2, 8)
            zr = zr_ref[pl.ds(r0, n2), :].astype(BF16)
            zi = zi_ref[pl.ds(r0, n2), :].astype(BF16)
            g = (jnp.dot(c2_ref[...], zr, preferred_element_type=F32)
                 + jnp.dot(s2_ref[...], zi, preferred_element_type=F32))
            ov[pl.ds(j, n2, stride=n1), :] = g * scale + b_ref[...]
        return carry

    lax.fori_loop(0, n1 // 8, phase2, 0)


def _dft_lat(p, q, m, c2, s2, rot, bias, n1, n2, scale):
    length, w = p.shape
    col = pl.BlockSpec((length, 128), lambda j: (0, j), pipeline_mode=pl.Buffered(1))
    const = lambda shape: pl.BlockSpec(shape, lambda j: (0,) * len(shape))
    return pl.pallas_call(
        functools.partial(_dft_lat_kernel, n1, n2, scale),
        grid=(w // 128,),
        in_specs=[col, col, const((2 * n1, 2 * n1)), const((n2, n2)), const((n2, n2)), const((2, n1, 128)),
                  pl.BlockSpec((1, 128), lambda j: (0, j))],
        out_specs=pl.BlockSpec((length, 128), lambda j: (0, j)),
        out_shape=jax.ShapeDtypeStruct((length, w), F32),
        scratch_shapes=[pltpu.VMEM((length, 128), F32), pltpu.VMEM((length, 128), F32)],
        compiler_params=_cp(("arbitrary",), 56),
        name="fnet_dft_lat",
    )(p, q, m, c2, s2, rot, bias)


def _dft_small_kernel(scale, c_ref, s_ref, p_ref, q_ref, b_ref, o_ref):
    g = (jnp.dot(c_ref[...], p_ref[...].astype(BF16), preferred_element_type=F32)
         - jnp.dot(s_ref[...], q_ref[...].astype(BF16), preferred_element_type=F32))
    o_ref[...] = g * scale + b_ref[...]


def _dft_small(cm, sm, p, q, bias, scale):
    n, w = p.shape
    full = lambda shape: pl.BlockSpec(shape, lambda i: (0,) * len(shape))
    return pl.pallas_call(
        functools.partial(_dft_small_kernel, scale),
        grid=(1,),
        in_specs=[full((n, n)), full((n, n)), full((n, w)), full((n, w)), full((1, w))],
        out_specs=full((n, w)),
        out_shape=jax.ShapeDtypeStruct((n, w), F32),
        compiler_params=_cp(("arbitrary",)),
        name="fnet_dft_ctx",
    )(cm, sm, p, q, bias)


def _cos_sin(n_rows, n_cols, period):
    ang = (2.0 * math.pi / period) * ((jnp.arange(n_rows)[:, None] * jnp.arange(n_cols)[None, :]) % period).astype(F32)
    return jnp.cos(ang), jnp.sin(ang)


def _fnet_positions(p, q, bias, n_ctx):
    pc, qc, pl_, ql_ = p[:n_ctx], q[:n_ctx], p[n_ctx:], q[n_ctx:]
    length = pl_.shape[0]
    n1 = 1 << ((length.bit_length() - 1) // 2)
    n2 = length // n1
    assert n1 * n2 == length and n1 % 8 == 0
    bias = bias.reshape(1, FN_W).astype(F32)
    cc, sc = _cos_sin(n_ctx, n_ctx, n_ctx)
    g_ctx = _dft_small(cc.astype(BF16), sc.astype(BF16), pc, qc, bias, 1.0 / math.sqrt(FN_C * n_ctx))
    c1, s1 = _cos_sin(n1, n1, n1)
    m = jnp.concatenate([jnp.concatenate([c1, -s1], axis=1),
                         jnp.concatenate([-s1, -c1], axis=1)], axis=0).astype(BF16)
    ang = (2.0 * math.pi / length) * jnp.arange(n1, dtype=F32)
    rot = jnp.stack([jnp.broadcast_to(jnp.cos(ang)[:, None], (n1, 128)),
                     jnp.broadcast_to(-jnp.sin(ang)[:, None], (n1, 128))])
    c2, s2 = _cos_sin(n2, n2, n2)
    g_lat = _dft_lat(pl_, ql_, m, c2.astype(BF16), s2.astype(BF16), rot, bias, n1, n2,
                     1.0 / math.sqrt(FN_C * length))
    return jnp.concatenate([g_ctx, g_lat], axis=0)


def _even_out_kernel(ys_ref, g_ref, x_ref, wglu_ref, bglu_ref, wout_ref, gate_ref, lg_ref, lb_ref, o_ref):
    ys = ys_ref[...]
    c0 = math.sqrt(2.0 / math.pi)
    y = 0.5 * ys * (1.0 + jnp.tanh(c0 * (ys + 0.044715 * (ys * ys * ys))))
    gl = jnp.dot(y.astype(BF16), wglu_ref[...], preferred_element_type=F32) + bglu_ref[...]
    y2 = y * jax.nn.sigmoid(gl)
    m = (jnp.dot(y2.astype(BF16), wout_ref[:S5_W, :], preferred_element_type=F32)
         + jnp.dot(g_ref[...].astype(BF16), wout_ref[S5_W:, :], preferred_element_type=F32))
    z = ALPHA * x_ref[...] + gate_ref[0] * m
    o_ref[...] = _layer_norm_rows(z, lg_ref[...], lb_ref[...])


def _even_out(ys, g, tok, w_glu, b_glu, w_out, gate, ln_g, ln_b, nctxb):
    nt = tok.shape[0]
    row = lambda i: (i, 0)
    const = lambda i: (0, 0)
    return pl.pallas_call(
        _even_out_kernel,
        grid=(nt // TM,),
        in_specs=[pl.BlockSpec((TM, S5_W), row), pl.BlockSpec((TM, FN_W), row), pl.BlockSpec((TM, D), row),
                  pl.BlockSpec((S5_W, S5_W), const), pl.BlockSpec((1, S5_W), const),
                  pl.BlockSpec((D, D), const), pl.BlockSpec((1, 1, D), _rowtype(nctxb)),
                  pl.BlockSpec((1, D), const), pl.BlockSpec((1, D), const)],
        out_specs=pl.BlockSpec((TM, D), row),
        out_shape=jax.ShapeDtypeStruct((nt, D), F32),
        compiler_params=_cp(("arbitrary",)),
        name="even_out",
    )(ys, g, tok, w_glu, b_glu, w_out, gate, ln_g, ln_b)


def _qkv_kernel(x_ref, sc_ref, sh_ref, w_ref, cos_ref, sin_ref, q_ref, k_ref, v_ref):
    h = (x_ref[...] * sc_ref[0] + sh_ref[0]).astype(BF16)
    cos, sin = cos_ref[...], sin_ref[...]
    z = jnp.dot(h, w_ref[...], preferred_element_type=F32)
    for hh in range(HEADS):
        sl = slice(hh * 128, (hh + 1) * 128)
        q = z[:, sl] * cos + z[:, D + hh * 128:D + (hh + 1) * 128] * sin
        k = z[:, 2 * D + hh * 128:2 * D + (hh + 1) * 128] * cos + z[:, 3 * D + hh * 128:3 * D + (hh + 1) * 128] * sin
        q_ref[:, sl] = (q * (HD ** -0.5 * LOG2E)).astype(BF16)
        k_ref[:, sl] = k.astype(BF16)
        v_ref[:, 2 * hh * VD:(2 * hh + 1) * VD] = z[:, 4 * D + hh * VD:4 * D + (hh + 1) * VD].astype(BF16)
        v_ref[:, (2 * hh + 1) * VD:(2 * hh + 2) * VD] = jnp.ones((z.shape[0], VD), BF16)


def _qkv(tok, sc1p, sh, w5, cos, sin, nctxb):
    nt = tok.shape[0]
    row = lambda i: (i, 0)
    o = jax.ShapeDtypeStruct((nt, D), BF16)
    ov = jax.ShapeDtypeStruct((nt, 2 * D), BF16)
    return pl.pallas_call(
        _qkv_kernel,
        grid=(nt // TM,),
        in_specs=[pl.BlockSpec((TM, D), row),
                  pl.BlockSpec((1, 1, D), _rowtype(nctxb)),
                  pl.BlockSpec((1, 1, D), _rowtype(nctxb)),
                  pl.BlockSpec((D, 5 * D), lambda i: (0, 0)),
                  pl.BlockSpec((TM, 128), row), pl.BlockSpec((TM, 128), row)],
        out_specs=[pl.BlockSpec((TM, D), row), pl.BlockSpec((TM, D), row), pl.BlockSpec((TM, 2 * D), row)],
        out_shape=[o, o, ov],
        compiler_params=_cp(("arbitrary",)),
        name="qkv_rope",
    )(tok, sc1p, sh, w5, cos, sin)


def _attn_kernel(n_head, ts, n_pairs, q_ref, k_ref, v_ref, lam_ref, gs_ref, o_ref,
                 qq_ref, sa_ref, sb_ref, m_ref, acc_ref):
    tq = q_ref.shape[0]
    q = q_ref[...]
    lane = lax.broadcasted_iota(jnp.int32, q.shape, 1)
    zero = jnp.zeros_like(q)
    qq_ref[:tq, :] = jnp.where(lane < HD, q, zero)
    qq_ref[tq:, :] = jnp.where(lane >= HD, q, zero)
    m_ref[...] = jnp.full(m_ref.shape, -1e30, F32)
    acc_ref[...] = jnp.zeros(acc_ref.shape, F32)

    def scores(off, size, dst_ref):
        dst_ref[:, :size] = lax.dot_general(qq_ref[...], k_ref[pl.ds(off, size), :],
                                            (((1,), (1,)), ((), ())), preferred_element_type=F32)

    def consume(off, size, src_ref):
        tiles = [src_ref[:, t * 128:(t + 1) * 128] for t in range(size // 128)]
        mx = functools.reduce(jnp.maximum, tiles)
        m_old = m_ref[...]
        m_new = jnp.maximum(m_old, jnp.max(mx, axis=-1, keepdims=True))
        alpha = jnp.exp2(m_old - m_new)
        p = jnp.concatenate([jnp.exp2(t - m_new).astype(BF16) for t in tiles], axis=1)
        pv = jnp.dot(p, v_ref[pl.ds(off, size), :], preferred_element_type=F32)
        acc_ref[...] = jnp.concatenate([alpha, alpha], axis=1) * acc_ref[...] + pv
        m_ref[...] = m_new

    scores(0, n_head, sa_ref)
    if n_pairs:
        scores(n_head, ts, sb_ref)
    consume(0, n_head, sa_ref)

    def pair(i, last):
        off0 = pl.multiple_of(n_head + (2 * i) * ts, 128)
        off1 = pl.multiple_of(n_head + (2 * i + 1) * ts, 128)
        scores(off1, ts, sa_ref)
        consume(off0, ts, sb_ref)
        if not last:
            scores(pl.multiple_of(n_head + (2 * i + 2) * ts, 128), ts, sb_ref)
        consume(off1, ts, sa_ref)

    if n_pairs:
        def body(i, carry):
            pair(i, False)
            return carry
        lax.fori_loop(0, n_pairs - 1, body, 0)
        pair(n_pairs - 1, True)

    acc = acc_ref[...]
    o12 = acc[:, :VD] / acc[:, VD:]
    o = o12[:tq] - lam_ref[...] * o12[tq:]
    o = o * lax.rsqrt(jnp.mean(o * o, axis=-1, keepdims=True) + LN_EPS)
    o_ref[...] = (o * gs_ref[...]).astype(BF16)


def _attention(q, k, vext, lamv, gsv, tq, n_head, ts):
    nq, nk = q.shape[0], k.shape[0]
    n_sub = (nk - n_head) // ts if ts else 0
    assert n_head + n_sub * ts == nk and n_sub % 2 == 0 and nq % tq == 0
    const = lambda h, i: (0, 0)
    return pl.pallas_call(
        functools.partial(_attn_kernel, n_head, ts, n_sub // 2),
        grid=(HEADS, nq // tq),
        in_specs=[pl.BlockSpec((tq, 128), lambda h, i: (i, h)),
                  pl.BlockSpec((nk, 128), lambda h, i: (0, h)),
                  pl.BlockSpec((nk, 2 * VD), lambda h, i: (0, h)),
                  pl.BlockSpec((1, 128), const), pl.BlockSpec((1, 128), const)],
        out_specs=pl.BlockSpec((tq, 128), lambda h, i: (i, h)),
        out_shape=jax.ShapeDtypeStruct((nq, D), BF16),
        scratch_shapes=[pltpu.VMEM((2 * tq, 128), BF16),
                        pltpu.VMEM((2 * tq, max(ts, n_head)), F32), pltpu.VMEM((2 * tq, max(ts, n_head)), F32),
                        pltpu.VMEM((2 * tq, 128), F32), pltpu.VMEM((2 * tq, 2 * VD), F32)],
        compiler_params=_cp(("arbitrary", "arbitrary")),
        name="diff_attention",
    )(q, k, vext, lamv, gsv)


def _proj_ln_kernel(a_ref, w_ref, x_ref, gate_ref, lg_ref, lb_ref, o_ref):
    m = jnp.dot(a_ref[...], w_ref[...], preferred_element_type=F32)
    z = ALPHA * x_ref[...] + gate_ref[0] * m
    o_ref[...] = _layer_norm_rows(z, lg_ref[...], lb_ref[...])


def _proj_ln(a, w, tok, gate, ln_g, ln_b, nctxb):
    nt = tok.shape[0]
    row = lambda i: (i, 0)
    const = lambda i: (0, 0)
    return pl.pallas_call(
        _proj_ln_kernel,
        grid=(nt // TM,),
        in_specs=[pl.BlockSpec((TM, D), row), pl.BlockSpec((D, D), const), pl.BlockSpec((TM, D), row),
                  pl.BlockSpec((1, 1, D), _rowtype(nctxb)), pl.BlockSpec((1, D), const), pl.BlockSpec((1, D), const)],
        out_specs=pl.BlockSpec((TM, D), row),
        out_shape=jax.ShapeDtypeStruct((nt, D), F32),
        compiler_params=_cp(("arbitrary",)),
        name="attn_out_ln",
    )(a, w, tok, gate, ln_g, ln_b)


def _router_kernel(x_ref, sc_ref, sh_ref, wh_ref, wl_ref, b_ref, tri_ref, h_ref, ri_ref, rw_ref, cnt_ref,
                   run_ref):
    @pl.when(pl.program_id(0) == 0)
    def _():
        run_ref[...] = jnp.zeros(run_ref.shape, F32)

    h = x_ref[...] * sc_ref[0] + sh_ref[0]
    hh = h.astype(BF16)
    hl = (h - hh.astype(F32)).astype(BF16)
    h_ref[...] = hh
    v = (jnp.dot(hh, wh_ref[...], preferred_element_type=F32)
         + jnp.dot(hh, wl_ref[...], preferred_element_type=F32)
         + jnp.dot(hl, wh_ref[...], preferred_element_type=F32)) + b_ref[...]
    lane = lax.broadcasted_iota(jnp.int32, v.shape, 1)
    lane_f = lane.astype(F32)
    vals, sels, idxs = [], [], []
    for _ in range(TOP_K):
        mk = jnp.max(v, axis=-1, keepdims=True)
        ik = jnp.min(jnp.where(v == mk, lane_f, 128.0), axis=-1, keepdims=True)
        sel = lane_f == ik
        v = jnp.where(sel, -jnp.inf, v)
        vals.append(mk)
        sels.append(sel)
        idxs.append(ik)
    es = [jnp.exp(mk - vals[0]) for mk in vals]
    den = es[0] + es[1] + es[2] + es[3]
    onehot = jnp.zeros(v.shape, F32)
    for sel in sels:
        onehot = jnp.where(sel, 1.0, onehot)
    prefix = jnp.dot(tri_ref[...], onehot.astype(BF16), preferred_element_type=F32)
    rank_all = prefix + run_ref[0:1, :]
    run_ref[0:1, :] = run_ref[0:1, :] + jnp.sum(onehot, axis=0, keepdims=True)
    ri = jnp.zeros(v.shape, F32)
    rw = jnp.zeros(v.shape, F32)
    for k in range(TOP_K):
        rk = jnp.sum(jnp.where(sels[k], rank_all, 0.0), axis=-1, keepdims=True)
        ri = jnp.where(lane == k, idxs[k], ri)
        ri = jnp.where(lane == TOP_K + k, rk, ri)
        rw = jnp.where(lane == k, es[k] / den, rw)
    ri_ref[...] = ri.astype(jnp.int32)
    rw_ref[...] = rw
    cnt_ref[...] = run_ref[...]


def _router(tok, sc1p, sh, w_hi, w_lo, bias, nctxb):
    nt = tok.shape[0]
    row = lambda i: (i, 0)
    const = lambda i: (0, 0)
    tri = (jnp.arange(TM)[:, None] > jnp.arange(TM)[None, :]).astype(BF16)
    return pl.pallas_call(
        _router_kernel,
        grid=(nt // TM,),
        in_specs=[pl.BlockSpec((TM, D), row),
                  pl.BlockSpec((1, 1, D), _rowtype(nctxb)), pl.BlockSpec((1, 1, D), _rowtype(nctxb)),
                  pl.BlockSpec((D, 128), const), pl.BlockSpec((D, 128), const), pl.BlockSpec((1, 128), const),
                  pl.BlockSpec((TM, TM), const)],
        out_specs=[pl.BlockSpec((TM, D), row), pl.BlockSpec((TM, 128), row), pl.BlockSpec((TM, 128), row),
                   pl.BlockSpec((8, 128), const)],
        out_shape=[jax.ShapeDtypeStruct((nt, D), BF16), jax.ShapeDtypeStruct((nt, 128), jnp.int32),
                   jax.ShapeDtypeStruct((nt, 128), F32), jax.ShapeDtypeStruct((8, 128), F32)],
        scratch_shapes=[pltpu.VMEM((8, 128), F32)],
        compiler_params=_cp(("arbitrary",)),
        name="moe_router",
    )(tok, sc1p, sh, w_hi, w_lo, bias, tri)


def _expert_kernel(be_ref, nu_ref, x_ref, wg_ref, bg_ref, wu_ref, bu_ref, wd_ref, bd_ref, o_ref,
                   wgb_ref, wub_ref, wdb_ref):
    i = pl.program_id(0)
    prev = be_ref[jnp.maximum(i - 1, 0)]
    new_expert = jnp.logical_or(i == 0, be_ref[i] != prev)

    @pl.when(new_expert)
    def _():
        wgb_ref[...] = wg_ref[...].astype(BF16)
        wub_ref[...] = wu_ref[...].astype(BF16)
        wdb_ref[...] = wd_ref[...].astype(BF16)

    @pl.when(i < nu_ref[0])
    def _():
        x = x_ref[...]
        g = jnp.minimum(jnp.dot(x, wgb_ref[...], preferred_element_type=F32) + bg_ref[...], SWIGLU_LIMIT)
        u = jnp.clip(jnp.dot(x, wub_ref[...], preferred_element_type=F32) + bu_ref[...], -SWIGLU_LIMIT, SWIGLU_LIMIT)
        act = g * jax.nn.sigmoid(SWIGLU_ALPHA * g) * (u + 1.0)
        y = jnp.dot(act.astype(BF16), wdb_ref[...], preferred_element_type=F32) + bd_ref[...]
        o_ref[...] = y.astype(BF16)

    @pl.when(i >= nu_ref[0])
    def _():
        o_ref[...] = jnp.zeros(o_ref.shape, BF16)


def _experts(layer, blk_e, n_used, xd, w_gate, b_gate, w_up, b_up, w_down, b_down):
    cap = xd.shape[0]
    wspec = pl.BlockSpec((None, None, D, D), lambda i, be, nu: (layer, be[i], 0, 0))
    bspec = pl.BlockSpec((None, None, 1, D), lambda i, be, nu: (layer, be[i], 0, 0))
    row = pl.BlockSpec((TME, D), lambda i, be, nu: (i, 0))
    b4 = lambda b: b.reshape(DEPTH, N_EXP, 1, D)
    return pl.pallas_call(
        _expert_kernel,
        grid_spec=pltpu.PrefetchScalarGridSpec(
            num_scalar_prefetch=2,
            grid=(cap // TME,),
            in_specs=[row, wspec, bspec, wspec, bspec, wspec, bspec],
            out_specs=row,
            scratch_shapes=[pltpu.VMEM((D, D), BF16)] * 3),
        out_shape=jax.ShapeDtypeStruct((cap, D), BF16),
        compiler_params=_cp(("arbitrary",), 56),
        name="moe_experts",
    )(blk_e, n_used, xd, w_gate, b4(b_gate), w_up, b4(b_up), w_down, b4(b_down))


def _moe_finish_kernel(yg_ref, w_ref, x_ref, gate_ref, lg_ref, lb_ref, o_ref):
    w = w_ref[...]
    y = yg_ref[0].astype(F32) * w[:, 0:1]
    for kk in range(1, TOP_K):
        y = y + yg_ref[kk].astype(F32) * w[:, kk:kk + 1]
    z = ALPHA * x_ref[...] + gate_ref[0] * y
    o_ref[...] = _layer_norm_rows(z, lg_ref[...], lb_ref[...])


def _moe_finish(yg, top_w, tok, gate, ln_g, ln_b, nctxb):
    nt = tok.shape[0]
    row = lambda i: (i, 0)
    const = lambda i: (0, 0)
    return pl.pallas_call(
        _moe_finish_kernel,
        grid=(nt // TM,),
        in_specs=[pl.BlockSpec((TOP_K, TM, D), lambda i: (0, i, 0)), pl.BlockSpec((TM, TOP_K), row),
                  pl.BlockSpec((TM, D), row),
                  pl.BlockSpec((1, 1, D), _rowtype(nctxb)), pl.BlockSpec((1, D), const), pl.BlockSpec((1, D), const)],
        out_specs=pl.BlockSpec((TM, D), row),
        out_shape=jax.ShapeDtypeStruct((nt, D), F32),
        compiler_params=_cp(("arbitrary",)),
        name="moe_finish",
    )(yg, top_w, tok, gate, ln_g, ln_b)


def _moe_layer(layer, tok, sc1p, sh, gate, ln_g, ln_b, w_router, b_router, w_gate, b_gate, w_up, b_up,
               w_down, b_down, nctxb):
    nt = tok.shape[0]
    wr = jnp.zeros((D, 128), F32).at[:, :N_EXP].set(w_router.astype(F32))
    wr_hi = wr.astype(BF16)
    wr_lo = (wr - wr_hi.astype(F32)).astype(BF16)
    br = jnp.full((1, 128), -1e30, F32).at[0, :N_EXP].set(b_router.astype(F32))
    hb, ri, rw, cnt = _router(tok, sc1p, sh, wr_hi, wr_lo, br, nctxb)
    top_idx, rank, top_w = ri[:, :TOP_K], ri[:, TOP_K:2 * TOP_K], rw[:, :TOP_K]
    counts = cnt[0, :N_EXP].astype(jnp.int32)
    padded = (counts + TME - 1) // TME * TME
    pad_end = jnp.cumsum(padded)
    pad_start = pad_end - padded
    dest = (pad_start[top_idx] + rank).astype(jnp.int32)
    cap = nt * TOP_K + N_EXP * TME
    nb = cap // TME
    blk_start = jnp.arange(nb, dtype=jnp.int32) * TME
    blk_e = jnp.minimum(jnp.sum((pad_end[None, :] <= blk_start[:, None]).astype(jnp.int32), axis=1), N_EXP - 1)
    n_used = (pad_end[-1] // TME).astype(jnp.int32).reshape(1)
    tok_ids = jnp.broadcast_to(jnp.arange(nt, dtype=jnp.int32)[:, None], (nt, TOP_K))
    src = jnp.zeros((cap,), jnp.int32).at[dest.reshape(-1)].set(
        tok_ids.reshape(-1), unique_indices=True, mode="promise_in_bounds")
    xd = hb.at[src].get(mode="promise_in_bounds")
    yd = _experts(layer, blk_e, n_used, xd, w_gate, b_gate, w_up, b_up, w_down, b_down)
    yg = yd.at[dest.T.reshape(-1)].get(mode="promise_in_bounds").reshape(TOP_K, nt, D)
    return _moe_finish(yg, top_w, tok, gate, ln_g, ln_b, nctxb)


def _rope_tables(rows, n_ctx):
    row = jnp.broadcast_to(jnp.arange(rows, dtype=F32)[:, None], (rows, GRID_W)).reshape(-1)
    col = jnp.broadcast_to(jnp.arange(GRID_W, dtype=F32)[None, :], (rows, GRID_W)).reshape(-1)
    theta = ROPE_BASE ** (-jnp.arange(ROPE_F, dtype=F32) / ROPE_F)
    ang = jnp.stack([row[:, None] * theta, col[:, None] * theta], axis=1)
    ang = jnp.stack([ang, ang], axis=2).reshape(rows * GRID_W, HD)
    cos = jnp.concatenate([jnp.ones((n_ctx, HD), F32), jnp.cos(ang)], axis=0)
    sin = jnp.concatenate([jnp.zeros((n_ctx, HD), F32), jnp.sin(ang)], axis=0)
    return jnp.tile(cos, (1, 2)), jnp.tile(sin, (1, 2))


def _rot_cols(w):
    k = w.shape[0]
    wr = w.reshape(k, -1, 2, 2, ROPE_F)
    rot = jnp.stack([-wr[..., 1, :], wr[..., 0, :]], axis=-2)
    return rot.reshape(w.shape)


def kernel(x, c, ctx, c_ctx, ada_w, ada_b, ln_g, ln_b, even_w_in, s5_lam_re, s5_lam_im, s5_log_dt, s5_b_re, s5_b_im, s5_c_re, s5_c_im, s5_d, s5_w_glu, s5_b_glu, fnet_w, fnet_b, even_w_out, odd_w_qkv, odd_w_o, da_lq1, da_lk1, da_lq2, da_lk2, da_subln_g, router_w, router_b, moe_w_gate, moe_b_gate, moe_w_up, moe_b_up, moe_w_down, moe_b_down):
    seq = x.shape[1]
    n_ctx = ctx.shape[1]
    assert x.shape[0] == 1 and n_ctx % TM == 0 and seq % TM == 0 and n_ctx == TM
    nctxb = n_ctx // TM
    nt = n_ctx + seq
    nc = nt // CH
    ncc = n_ctx // CH
    tq_lat = 512 if seq % 512 == 0 else TM
    ts_lat = 512 if seq % 1024 == 0 else TM
    assert seq % (2 * ts_lat) == 0

    tok = jnp.concatenate([ctx[0], x[0]], axis=0).astype(F32)
    cond8 = jnp.zeros((8, D), F32).at[0].set(c_ctx.astype(F32)).at[1].set(c[0].astype(F32))
    mods = _ada_mods(cond8, ada_w, ada_b)[:, :2].reshape(DEPTH, 2, 6, 1, D)
    cos, sin = _rope_tables(seq // GRID_W, n_ctx)
    c128, s128 = _cos_sin(FN_C, FN_C, FN_C)

    for l in range(DEPTH):
        i = l // 2
        m = mods[l]
        shift_a, scale_a, gate_a, shift_b, scale_b, gate_b = (m[:, j] for j in range(6))
        lg0, lb0 = ln_g[l, 0].reshape(1, D), ln_b[l, 0].reshape(1, D)
        lg1, lb1 = ln_g[l, 1].reshape(1, D), ln_b[l, 1].reshape(1, D)
        if l % 2 == 0:
            wf = jnp.concatenate([jnp.einsum('ab,gbd->gad', c128, fnet_w[i].astype(F32), precision=HIGHEST),
                                  jnp.einsum('ab,gbd->gad', s128, fnet_w[i].astype(F32), precision=HIGHEST)],
                                 axis=-1).astype(BF16)
            u, p, q = _even_in(tok, 1.0 + scale_a, shift_a, even_w_in[i].astype(BF16), wf, nctxb)
            m1, w2, avec = _s5_matrices(s5_lam_re[i], s5_lam_im[i], s5_log_dt[i], s5_b_re[i], s5_b_im[i],
                                        s5_c_re[i], s5_c_im[i], s5_d[i])
            yi, sfr, sfi, sbr, sbi = _s5_in(u, m1)
            hfr, hfi, hbr, hbi = _s5_scan(sfr, sfi, sbr, sbi, avec, ncc)
            ys = _s5_out(yi, hfr, hfi, hbr, hbi, w2)
            g = _fnet_positions(p, q, fnet_b[i], n_ctx)
            tok = _even_out(ys, g, tok, s5_w_glu[i].astype(BF16), s5_b_glu[i].reshape(1, S5_W).astype(F32),
                            even_w_out[i].astype(BF16), gate_a, lg0, lb0, nctxb)
        else:
            lam_init = 0.8 - 0.6 * math.exp(-0.3 * l)
            lam = (jnp.exp(jnp.sum(da_lq1[i].astype(F32) * da_lk1[i].astype(F32)))
                   - jnp.exp(jnp.sum(da_lq2[i].astype(F32) * da_lk2[i].astype(F32))) + lam_init)
            wq, wk, wv = odd_w_qkv[i][:, :D], odd_w_qkv[i][:, D:2 * D], odd_w_qkv[i][:, 2 * D:]
            w5 = jnp.concatenate([wq, _rot_cols(wq), wk, _rot_cols(wk), wv], axis=1).astype(BF16)
            qb, kb, vb = _qkv(tok, 1.0 + scale_a, shift_a, w5, cos, sin, nctxb)
            lamv = jnp.full((1, VD), lam, F32)
            gsv = (da_subln_g[i].astype(F32) * (1.0 - lam_init)).reshape(1, VD)
            on_lat = _attention(qb[n_ctx:], kb, vb, lamv, gsv, tq_lat, n_ctx, ts_lat)
            on_ctx = _attention(qb[:n_ctx], kb[:n_ctx], vb[:n_ctx], lamv, gsv, n_ctx, n_ctx, 0)
            on = jnp.concatenate([on_ctx, on_lat], axis=0)
            tok = _proj_ln(on, odd_w_o[i].astype(BF16), tok, gate_a, lg0, lb0, nctxb)
        tok = _moe_layer(l, tok, 1.0 + scale_b, shift_b, gate_b, lg1, lb1, router_w[l], router_b[l],
                         moe_w_gate, moe_b_gate, moe_w_up, moe_b_up, moe_w_down, moe_b_down, nctxb)
    return tok[n_ctx:].reshape(1, seq, D).astype(x.dtype)
```

```python
import functools
import math

import jax
import jax.numpy as jnp
from jax import lax
from jax.experimental import pallas as pl
from jax.experimental.pallas import tpu as pltpu

F32 = jnp.float32
BF16 = jnp.bfloat16
HIGHEST = lax.Precision.HIGHEST

D = 1024
DEPTH = 4
GRID_W = 64
S5_W = 512
S5_G = 32
S5_C = 16
S5_N = 64
FN_W = 512
FN_G = 4
FN_C = 128
HEADS = 8
HD = 64
VD = 128
ROPE_BASE = 10000.0
ROPE_F = 16
N_EXP = 32
TOP_K = 4
SWIGLU_LIMIT = 7.0
SWIGLU_ALPHA = 1.702
ALPHA = (2.0 * DEPTH) ** 0.25
LN_EPS = 1e-5
LOG2E = 1.4426950408889634
HI16 = -65536

TM = 256
TME = 512
CH = 16
MIB = 1024 * 1024


def _cp(sem, vmem_mib=48):
    return pltpu.CompilerParams(dimension_semantics=sem, vmem_limit_bytes=vmem_mib * MIB)


def _rowtype(nctxb):
    return lambda i: (jnp.where(i >= nctxb, 1, 0), 0, 0)


def _layer_norm_rows(z, g, b):
    mu = jnp.mean(z, axis=-1, keepdims=True)
    zc = z - mu
    var = jnp.mean(zc * zc, axis=-1, keepdims=True)
    return zc * lax.rsqrt(var + LN_EPS) * g + b


def _ada_kernel(c_ref, w_ref, b_ref, o_ref):
    c = c_ref[...]
    a = c * jax.nn.sigmoid(c)
    o_ref[0] = jnp.dot(a, w_ref[0], preferred_element_type=F32, precision=HIGHEST) + b_ref[0]


def _ada_mods(cond8, ada_w, ada_b):
    tn = 1536
    return pl.pallas_call(
        _ada_kernel,
        grid=(DEPTH, 6 * D // tn),
        in_specs=[pl.BlockSpec((8, D), lambda l, j: (0, 0)),
                  pl.BlockSpec((1, D, tn), lambda l, j: (l, 0, j)),
                  pl.BlockSpec((1, 1, tn), lambda l, j: (l, 0, j))],
        out_specs=pl.BlockSpec((1, 8, tn), lambda l, j: (l, 0, j)),
        out_shape=jax.ShapeDtypeStruct((DEPTH, 8, 6 * D), F32),
        compiler_params=_cp(("arbitrary", "arbitrary")),
        name="ada_mods",
    )(cond8, ada_w, ada_b.reshape(DEPTH, 1, 6 * D))


def _even_in_kernel(x_ref, sc_ref, sh_ref, w_ref, wf_ref, u_ref, p_ref, q_ref):
    h = x_ref[...] * sc_ref[0] + sh_ref[0]
    z = jnp.dot(h.astype(BF16), w_ref[...], preferred_element_type=F32)
    u_ref[...] = z[:, :S5_W]
    for g in range(FN_G):
        f = z[:, S5_W + FN_C * g:S5_W + FN_C * (g + 1)]
        fc = f - jnp.mean(f, axis=-1, keepdims=True)
        fn = fc * lax.rsqrt(jnp.mean(fc * fc, axis=-1, keepdims=True) + LN_EPS)
        pq = jnp.dot(fn.astype(BF16), wf_ref[g], preferred_element_type=F32)
        p_ref[:, FN_C * g:FN_C * (g + 1)] = pq[:, :FN_C]
        q_ref[:, FN_C * g:FN_C * (g + 1)] = pq[:, FN_C:]


def _even_in(tok, sc1p, sh, w_in, wf, nctxb):
    nt = tok.shape[0]
    row = lambda i: (i, 0)
    return pl.pallas_call(
        _even_in_kernel,
        grid=(nt // TM,),
        in_specs=[pl.BlockSpec((TM, D), row),
                  pl.BlockSpec((1, 1, D), _rowtype(nctxb)),
                  pl.BlockSpec((1, 1, D), _rowtype(nctxb)),
                  pl.BlockSpec((D, D), lambda i: (0, 0)),
                  pl.BlockSpec((FN_G, FN_C, 2 * FN_C), lambda i: (0, 0, 0))],
        out_specs=[pl.BlockSpec((TM, S5_W), row),
                   pl.BlockSpec((TM, FN_W), row),
                   pl.BlockSpec((TM, FN_W), row)],
        out_shape=[jax.ShapeDtypeStruct((nt, S5_W), F32),
                   jax.ShapeDtypeStruct((nt, FN_W), F32),
                   jax.ShapeDtypeStruct((nt, FN_W), F32)],
        compiler_params=_cp(("arbitrary",)),
        name="even_in",
    )(tok, sc1p, sh, w_in, wf)


def _lane_iota(shape):
    return lax.broadcasted_iota(jnp.int32, shape, 1)


def _s5_in_kernel(x_ref, m1_ref, yi_ref, sfr_ref, sfi_ref, sbr_ref, sbi_ref):
    nc = x_ref.shape[0] // CH
    lane = _lane_iota((nc, 128)) // 32
    for pp in range(4):
        @pl.when(pl.program_id(1) == pp)
        def _(pp=pp):
            tiles = []
            for q in range(4):
                acc = None
                for r in range(4):
                    xs = x_ref[pl.ds(4 * q + r, nc, stride=CH), :]
                    shift = ((r - pp) * 32) % 128
                    if shift:
                        xs = pltpu.roll(xs, shift, 1)
                    acc = xs if acc is None else jnp.where(lane == r, xs, acc)
                tiles.append(acc.astype(BF16))
            u = jnp.concatenate(tiles, axis=1)
            r_all = jnp.dot(u, m1_ref[0], preferred_element_type=F32)
            yi_ref[0] = r_all[:, :512]
            sfr_ref[...] = r_all[:, 512:640]
            sfi_ref[...] = r_all[:, 640:768]
            sbr_ref[...] = r_all[:, 768:896]
            sbi_ref[...] = r_all[:, 896:1024]


def _s5_in(u, m1):
    nt = u.shape[0]
    nc = nt // CH
    sspec = pl.BlockSpec((nc, 128), lambda i, j: (0, 4 * i + j))
    sshape = jax.ShapeDtypeStruct((nc, 2048), F32)
    return pl.pallas_call(
        _s5_in_kernel,
        grid=(4, 4),
        in_specs=[pl.BlockSpec((nt, 128), lambda i, j: (0, i)),
                  pl.BlockSpec((1, 512, 1024), lambda i, j: (4 * i + j, 0, 0))],
        out_specs=[pl.BlockSpec((1, nc, 512), lambda i, j: (4 * i + j, 0, 0)), sspec, sspec, sspec, sspec],
        out_shape=[jax.ShapeDtypeStruct((16, nc, 512), F32), sshape, sshape, sshape, sshape],
        compiler_params=_cp(("arbitrary", "arbitrary")),
        name="s5_in",
    )(u, m1)


def _s5_scan_kernel(ncc, sfr_ref, sfi_ref, sbr_ref, sbi_ref, a_ref,
                    hfr_ref, hfi_ref, hbr_ref, hbi_ref):
    nc = sfr_ref.shape[0]
    afr, afi, abr, abi = a_ref[0:1, :], a_ref[1:2, :], a_ref[2:3, :], a_ref[3:4, :]

    def body(i, carry):
        fr, fi, br, bi = carry
        jb = jnp.where(i < ncc, ncc - 1 - i, nc - 1 - i + ncc)
        hfr_ref[pl.ds(i, 1), :] = fr
        hfi_ref[pl.ds(i, 1), :] = fi
        hbr_ref[pl.ds(jb, 1), :] = br
        hbi_ref[pl.ds(jb, 1), :] = bi
        sr, si = sfr_ref[pl.ds(i, 1), :], sfi_ref[pl.ds(i, 1), :]
        tr, ti = sbr_ref[pl.ds(jb, 1), :], sbi_ref[pl.ds(jb, 1), :]
        return (afr * fr - afi * fi + sr, afr * fi + afi * fr + si,
                abr * br - abi * bi + tr, abr * bi + abi * br + ti)

    z = jnp.zeros((1, sfr_ref.shape[1]), F32)
    lax.fori_loop(0, nc, body, (z, z, z, z))


def _s5_scan(sfr, sfi, sbr, sbi, avec, ncc):
    nc = sfr.shape[0]
    spec = pl.BlockSpec((nc, 512), lambda i: (0, i))
    shape = jax.ShapeDtypeStruct((nc, 2048), F32)
    return pl.pallas_call(
        functools.partial(_s5_scan_kernel, ncc),
        grid=(4,),
        in_specs=[spec, spec, spec, spec, pl.BlockSpec((8, 512), lambda i: (0, i))],
        out_specs=[spec, spec, spec, spec],
        out_shape=[shape, shape, shape, shape],
        compiler_params=_cp(("arbitrary",)),
        name="s5_scan",
    )(sfr, sfi, sbr, sbi, avec)


def _s5_out_kernel(yi_ref, hfr_ref, hfi_ref, hbr_ref, hbi_ref, w2_ref, o_ref, yp_ref):
    nc = yi_ref.shape[1]
    pp_dyn = pl.program_id(1)
    hcat = jnp.concatenate([hfr_ref[...], hfi_ref[...], hbr_ref[...], hbi_ref[...]], axis=1).astype(BF16)
    yp_ref[pp_dyn] = yi_ref[0] + jnp.dot(hcat, w2_ref[0], preferred_element_type=F32)

    @pl.when(pp_dyn == 3)
    def _():
        lane = _lane_iota((nc, 128)) // 32
        for s in range(CH):
            acc = None
            for pp in range(4):
                ys = yp_ref[pp, :, (s // 4) * 128:(s // 4 + 1) * 128]
                shift = ((pp - s % 4) * 32) % 128
                if shift:
                    ys = pltpu.roll(ys, shift, 1)
                acc = ys if acc is None else jnp.where(lane == pp, ys, acc)
            o_ref[pl.ds(s, nc, stride=CH), :] = acc


def _s5_out(yi, hfr, hfi, hbr, hbi, w2):
    nc = yi.shape[1]
    hspec = pl.BlockSpec((nc, 128), lambda i, j: (0, 4 * i + j))
    return pl.pallas_call(
        _s5_out_kernel,
        grid=(4, 4),
        in_specs=[pl.BlockSpec((1, nc, 512), lambda i, j: (4 * i + j, 0, 0)), hspec, hspec, hspec, hspec,
                  pl.BlockSpec((1, 512, 512), lambda i, j: (4 * i + j, 0, 0))],
        out_specs=pl.BlockSpec((nc * CH, 128), lambda i, j: (0, i)),
        out_shape=jax.ShapeDtypeStruct((nc * CH, S5_W), F32),
        scratch_shapes=[pltpu.VMEM((4, nc, 512), F32)],
        compiler_params=_cp(("arbitrary", "arbitrary")),
        name="s5_out",
    )(yi, hfr, hfi, hbr, hbi, w2)


def _toeplitz_kernel(k_ref, o_ref):
    lane = _lane_iota((32, 128)) // 32
    for s in range(CH):
        for q in range(CH // 4):
            acc = None
            for j in range(4):
                piece = k_ref[0, 4 * q + j - s + CH - 1]
                acc = piece if acc is None else jnp.where(lane == j, piece, acc)
            o_ref[0, s * 32:(s + 1) * 32, q * 128:(q + 1) * 128] = acc.astype(BF16)


def _toeplitz(kblk_rep):
    npair, nlag = kblk_rep.shape[:2]
    return pl.pallas_call(
        _toeplitz_kernel,
        grid=(npair,),
        in_specs=[pl.BlockSpec((1, nlag, 32, 128), lambda p: (p, 0, 0, 0))],
        out_specs=pl.BlockSpec((1, CH * 32, CH * 32), lambda p: (p, 0, 0)),
        out_shape=jax.ShapeDtypeStruct((npair, CH * 32, CH * 32), BF16),
        compiler_params=_cp(("arbitrary",)),
        name="s5_toeplitz",
    )(kblk_rep)


def _s5_matrices(lam_re, lam_im, log_dt, b_re, b_im, c_re, c_im, d_skip):
    lam = lax.complex(lam_re.astype(F32), lam_im.astype(F32))
    dt = jnp.exp(log_dt.astype(F32))[..., None]
    ldt = lam * dt
    lam_bar = jnp.exp(ldt)
    bb = ((lam_bar - 1.0) / lam)[..., None] * lax.complex(b_re.astype(F32), b_im.astype(F32))
    cc = lax.complex(c_re.astype(F32), c_im.astype(F32))
    ks = jnp.arange(CH + 1, dtype=F32)
    pw = jnp.exp(ldt[..., None] * ks)
    kk = jnp.real(jnp.einsum('dgcn,dgnk,dgne->dkgce', cc, pw[..., :CH], bb, precision=HIGHEST))
    k0 = kk[0, 0] + kk[1, 0] + jnp.eye(S5_C, dtype=F32)[None] * d_skip.astype(F32)[:, :, None]
    kall = jnp.concatenate([kk[1, 1:][::-1], k0[None], kk[0, 1:]], axis=0)
    eye2 = jnp.eye(2, dtype=F32)
    kt = kall.transpose(1, 0, 3, 2).reshape(16, 2, 2 * CH - 1, S5_C, S5_C)
    kblk = jnp.einsum('pglcd,gh->plgchd', kt, eye2).reshape(16, 2 * CH - 1, 32, 32)
    tp = _toeplitz(jnp.tile(kblk, (1, 1, 1, 4)))
    w1f = jnp.einsum('gns,gnc->gscn', pw[0][..., :CH][..., ::-1], bb[0])
    w1b = jnp.einsum('gns,gnc->gscn', pw[1][..., :CH], bb[1])

    def pair_cols(w):
        w = w.reshape(16, 2, CH, S5_C, S5_N)
        return jnp.einsum('pgscn,gh->psgchn', w, eye2).reshape(16, 512, 128)

    m1 = jnp.concatenate([tp, pair_cols(jnp.real(w1f)).astype(BF16), pair_cols(jnp.imag(w1f)).astype(BF16),
                          pair_cols(jnp.real(w1b)).astype(BF16), pair_cols(jnp.imag(w1b)).astype(BF16)], axis=-1)
    cpf = jnp.einsum('gcn,gnt->gntc', cc[0], pw[0][..., 1:])
    cpb = jnp.einsum('gcn,gnt->gntc', cc[1], pw[1][..., 1:][..., ::-1])

    def pair_rows(w):
        w = w.reshape(16, 2, S5_N, CH, S5_C)
        return jnp.einsum('pgntc,gh->pgnthc', w, eye2).reshape(16, 128, 512)

    w2 = jnp.concatenate([pair_rows(jnp.real(cpf)), pair_rows(-jnp.imag(cpf)),
                          pair_rows(jnp.real(cpb)), pair_rows(-jnp.imag(cpb))], axis=1)
    a_f, a_b = pw[0][..., CH].reshape(-1), pw[1][..., CH].reshape(-1)
    z = jnp.zeros_like(jnp.real(a_f))
    avec = jnp.stack([jnp.real(a_f), jnp.imag(a_f), jnp.real(a_b), jnp.imag(a_b), z, z, z, z])
    return m1.astype(BF16), w2.astype(BF16), avec


def _dft_lat_kernel(n1, n2, scale, p_ref, q_ref, m_ref, c2_ref, s2_ref, rot_ref, b_ref, o_ref, zr_ref, zi_ref):
    length = n1 * n2
    rr, ri = rot_ref[0], rot_ref[1]

    def phase1(blk, tw):
        b0 = pl.multiple_of(blk * 8, 8)
        pv = p_ref.at[pl.ds(b0, length - n2 + 8), :]
        qv = q_ref.at[pl.ds(b0, length - n2 + 8), :]
        zrv = zr_ref.at[pl.ds(b0, length - n2 + 8), :]
        ziv = zi_ref.at[pl.ds(b0, length - n2 + 8), :]
        twr, twi = tw
        for j in range(8):
            rows = pl.ds(j, n1, stride=n2)
            x = jnp.concatenate([pv[rows, :], qv[rows, :]], axis=0).astype(BF16)
            z = jnp.dot(m_ref[...], x, preferred_element_type=F32)
            zr, zi = z[:n1], z[n1:]
            zrv[rows, :] = zr * twr - zi * twi
            ziv[rows, :] = zr * twi + zi * twr
            twr, twi = twr * rr - twi * ri, twr * ri + twi * rr
        return twr, twi

    lax.fori_loop(0, n2 // 8, phase1, (jnp.ones((n1, 128), F32), jnp.zeros((n1, 128), F32)))

    def phase2(blk, carry):
        k0 = pl.multiple_of(blk * 8, 8)
        ov = o_ref.at[pl.ds(k0, length - n1 + 8), :]
        for j in range(8):
            r0 = pl.multiple_of((k0 + j) * n2, 8)
            zr = zr_ref[pl.ds(r0, n2), :].astype(BF16)
            zi = zi_ref[pl.ds(r0, n2), :].astype(BF16)
            g = (jnp.dot(c2_ref[...], zr, preferred_element_type=F32)
                 + jnp.dot(s2_ref[...], zi, preferred_element_type=F32))
            ov[pl.ds(j, n2, stride=n1), :] = g * scale + b_ref[...]
        return carry

    lax.fori_loop(0, n1 // 8, phase2, 0)


def _dft_lat(p, q, m, c2, s2, rot, bias, n1, n2, scale):
    length, w = p.shape
    col = pl.BlockSpec((length, 128), lambda j: (0, j), pipeline_mode=pl.Buffered(1))
    const = lambda shape: pl.BlockSpec(shape, lambda j: (0,) * len(shape))
    return pl.pallas_call(
        functools.partial(_dft_lat_kernel, n1, n2, scale),
        grid=(w // 128,),
        in_specs=[col, col, const((2 * n1, 2 * n1)), const((n2, n2)), const((n2, n2)), const((2, n1, 128)),
                  pl.BlockSpec((1, 128), lambda j: (0, j))],
        out_specs=pl.BlockSpec((length, 128), lambda j: (0, j)),
        out_shape=jax.ShapeDtypeStruct((length, w), F32),
        scratch_shapes=[pltpu.VMEM((length, 128), F32), pltpu.VMEM((length, 128), F32)],
        compiler_params=_cp(("arbitrary",), 56),
        name="fnet_dft_lat",
    )(p, q, m, c2, s2, rot, bias)


def _dft_small_kernel(scale, c_ref, s_ref, p_ref, q_ref, b_ref, o_ref):
    g = (jnp.dot(c_ref[...], p_ref[...].astype(BF16), preferred_element_type=F32)
         - jnp.dot(s_ref[...], q_ref[...].astype(BF16), preferred_element_type=F32))
    o_ref[...] = g * scale + b_ref[...]


def _dft_small(cm, sm, p, q, bias, scale):
    n, w = p.shape
    full = lambda shape: pl.BlockSpec(shape, lambda i: (0,) * len(shape))
    return pl.pallas_call(
        functools.partial(_dft_small_kernel, scale),
        grid=(1,),
        in_specs=[full((n, n)), full((n, n)), full((n, w)), full((n, w)), full((1, w))],
        out_specs=full((n, w)),
        out_shape=jax.ShapeDtypeStruct((n, w), F32),
        compiler_params=_cp(("arbitrary",)),
        name="fnet_dft_ctx",
    )(cm, sm, p, q, bias)


def _cos_sin(n_rows, n_cols, period):
    ang = (2.0 * math.pi / period) * ((jnp.arange(n_rows)[:, None] * jnp.arange(n_cols)[None, :]) % period).astype(F32)
    return jnp.cos(ang), jnp.sin(ang)


def _fnet_positions(p, q, bias, n_ctx):
    pc, qc, pl_, ql_ = p[:n_ctx], q[:n_ctx], p[n_ctx:], q[n_ctx:]
    length = pl_.shape[0]
    n1 = 1 << ((length.bit_length() - 1) // 2)
    n2 = length // n1
    assert n1 * n2 == length and n1 % 8 == 0
    bias = bias.reshape(1, FN_W).astype(F32)
    cc, sc = _cos_sin(n_ctx, n_ctx, n_ctx)
    g_ctx = _dft_small(cc.astype(BF16), sc.astype(BF16), pc, qc, bias, 1.0 / math.sqrt(FN_C * n_ctx))
    c1, s1 = _cos_sin(n1, n1, n1)
    m = jnp.concatenate([jnp.concatenate([c1, -s1], axis=1),
                         jnp.concatenate([-s1, -c1], axis=1)], axis=0).astype(BF16)
    ang = (2.0 * math.pi / length) * jnp.arange(n1, dtype=F32)
    rot = jnp.stack([jnp.broadcast_to(jnp.cos(ang)[:, None], (n1, 128)),
                     jnp.broadcast_to(-jnp.sin(ang)[:, None], (n1, 128))])
    c2, s2 = _cos_sin(n2, n2, n2)
    g_lat = _dft_lat(pl_, ql_, m, c2.astype(BF16), s2.astype(BF16), rot, bias, n1, n2,
                     1.0 / math.sqrt(FN_C * length))
    return jnp.concatenate([g_ctx, g_lat], axis=0)


def _even_out_kernel(ys_ref, g_ref, x_ref, wglu_ref, bglu_ref, wout_ref, gate_ref, lg_ref, lb_ref, o_ref):
    ys = ys_ref[...]
    c0 = math.sqrt(2.0 / math.pi)
    y = 0.5 * ys * (1.0 + jnp.tanh(c0 * (ys + 0.044715 * (ys * ys * ys))))
    gl = jnp.dot(y.astype(BF16), wglu_ref[...], preferred_element_type=F32) + bglu_ref[...]
    y2 = y * jax.nn.sigmoid(gl)
    m = (jnp.dot(y2.astype(BF16), wout_ref[:S5_W, :], preferred_element_type=F32)
         + jnp.dot(g_ref[...].astype(BF16), wout_ref[S5_W:, :], preferred_element_type=F32))
    z = ALPHA * x_ref[...] + gate_ref[0] * m
    o_ref[...] = _layer_norm_rows(z, lg_ref[...], lb_ref[...])


def _even_out(ys, g, tok, w_glu, b_glu, w_out, gate, ln_g, ln_b, nctxb):
    nt = tok.shape[0]
    row = lambda i: (i, 0)
    const = lambda i: (0, 0)
    return pl.pallas_call(
        _even_out_kernel,
        grid=(nt // TM,),
        in_specs=[pl.BlockSpec((TM, S5_W), row), pl.BlockSpec((TM, FN_W), row), pl.BlockSpec((TM, D), row),
                  pl.BlockSpec((S5_W, S5_W), const), pl.BlockSpec((1, S5_W), const),
                  pl.BlockSpec((D, D), const), pl.BlockSpec((1, 1, D), _rowtype(nctxb)),
                  pl.BlockSpec((1, D), const), pl.BlockSpec((1, D), const)],
        out_specs=pl.BlockSpec((TM, D), row),
        out_shape=jax.ShapeDtypeStruct((nt, D), F32),
        compiler_params=_cp(("arbitrary",)),
        name="even_out",
    )(ys, g, tok, w_glu, b_glu, w_out, gate, ln_g, ln_b)


def _qkv_kernel(x_ref, sc_ref, sh_ref, w_ref, cos_ref, sin_ref, q_ref, k_ref, v_ref):
    h = (x_ref[...] * sc_ref[0] + sh_ref[0]).astype(BF16)
    cos, sin = cos_ref[...], sin_ref[...]
    z = jnp.dot(h, w_ref[...], preferred_element_type=F32)
    for hh in range(HEADS):
        sl = slice(hh * 128, (hh + 1) * 128)
        q = z[:, sl] * cos + z[:, D + hh * 128:D + (hh + 1) * 128] * sin
        k = z[:, 2 * D + hh * 128:2 * D + (hh + 1) * 128] * cos + z[:, 3 * D + hh * 128:3 * D + (hh + 1) * 128] * sin
        q_ref[:, sl] = (q * (HD ** -0.5 * LOG2E)).astype(BF16)
        k_ref[:, sl] = k.astype(BF16)
        v_ref[:, 2 * hh * VD:(2 * hh + 1) * VD] = z[:, 4 * D + hh * VD:4 * D + (hh + 1) * VD].astype(BF16)
        v_ref[:, (2 * hh + 1) * VD:(2 * hh + 2) * VD] = jnp.ones((z.shape[0], VD), BF16)


def _qkv(tok, sc1p, sh, w5, cos, sin, nctxb):
    nt = tok.shape[0]
    row = lambda i: (i, 0)
    o = jax.ShapeDtypeStruct((nt, D), BF16)
    ov = jax.ShapeDtypeStruct((nt, 2 * D), BF16)
    return pl.pallas_call(
        _qkv_kernel,
        grid=(nt // TM,),
        in_specs=[pl.BlockSpec((TM, D), row),
                  pl.BlockSpec((1, 1, D), _rowtype(nctxb)),
                  pl.BlockSpec((1, 1, D), _rowtype(nctxb)),
                  pl.BlockSpec((D, 5 * D), lambda i: (0, 0)),
                  pl.BlockSpec((TM, 128), row), pl.BlockSpec((TM, 128), row)],
        out_specs=[pl.BlockSpec((TM, D), row), pl.BlockSpec((TM, D), row), pl.BlockSpec((TM, 2 * D), row)],
        out_shape=[o, o, ov],
        compiler_params=_cp(("arbitrary",)),
        name="qkv_rope",
    )(tok, sc1p, sh, w5, cos, sin)


def _attn_kernel(n_head, ts, n_pairs, q_ref, k_ref, v_ref, lam_ref, gs_ref, o_ref,
                 qq_ref, sa_ref, sb_ref, m_ref, acc_ref):
    tq = q_ref.shape[0]
    q = q_ref[...]
    lane = lax.broadcasted_iota(jnp.int32, q.shape, 1)
    zero = jnp.zeros_like(q)
    qq_ref[:tq, :] = jnp.where(lane < HD, q, zero)
    qq_ref[tq:, :] = jnp.where(lane >= HD, q, zero)
    m_ref[...] = jnp.full(m_ref.shape, -1e30, F32)
    acc_ref[...] = jnp.zeros(acc_ref.shape, F32)

    def scores(off, size, dst_ref):
        dst_ref[:, :size] = lax.dot_general(qq_ref[...], k_ref[pl.ds(off, size), :],
                                            (((1,), (1,)), ((), ())), preferred_element_type=F32)

    def consume(off, size, src_ref):
        tiles = [src_ref[:, t * 128:(t + 1) * 128] for t in range(size // 128)]
        mx = functools.reduce(jnp.maximum, tiles)
        m_old = m_ref[...]
        m_new = jnp.maximum(m_old, jnp.max(mx, axis=-1, keepdims=True))
        alpha = jnp.exp2(m_old - m_new)
        p = jnp.concatenate([jnp.exp2(t - m_new).astype(BF16) for t in tiles], axis=1)
        pv = jnp.dot(p, v_ref[pl.ds(off, size), :], preferred_element_type=F32)
        acc_ref[...] = jnp.concatenate([alpha, alpha], axis=1) * acc_ref[...] + pv
        m_ref[...] = m_new

    scores(0, n_head, sa_ref)
    if n_pairs:
        scores(n_head, ts, sb_ref)
    consume(0, n_head, sa_ref)

    def pair(i, last):
        off0 = pl.multiple_of(n_head + (2 * i) * ts, 128)
        off1 = pl.multiple_of(n_head + (2 * i + 1) * ts, 128)
        scores(off1, ts, sa_ref)
        consume(off0, ts, sb_ref)
        if not last:
            scores(pl.multiple_of(n_head + (2 * i + 2) * ts, 128), ts, sb_ref)
        consume(off1, ts, sa_ref)

    if n_pairs:
        def body(i, carry):
            pair(i, False)
            return carry
        lax.fori_loop(0, n_pairs - 1, body, 0)
        pair(n_pairs - 1, True)

    acc = acc_ref[...]
    o12 = acc[:, :VD] / acc[:, VD:]
    o = o12[:tq] - lam_ref[...] * o12[tq:]
    o = o * lax.rsqrt(jnp.mean(o * o, axis=-1, keepdims=True) + LN_EPS)
    o_ref[...] = (o * gs_ref[...]).astype(BF16)


def _attention(q, k, vext, lamv, gsv, tq, n_head, ts):
    nq, nk = q.shape[0], k.shape[0]
    n_sub = (nk - n_head) // ts if ts else 0
    assert n_head + n_sub * ts == nk and n_sub % 2 == 0 and nq % tq == 0
    const = lambda h, i: (0, 0)
    return pl.pallas_call(
        functools.partial(_attn_kernel, n_head, ts, n_sub // 2),
        grid=(HEADS, nq // tq),
        in_specs=[pl.BlockSpec((tq, 128), lambda h, i: (i, h)),
                  pl.BlockSpec((nk, 128), lambda h, i: (0, h)),
                  pl.BlockSpec((nk, 2 * VD), lambda h, i: (0, h)),
                  pl.BlockSpec((1, 128), const), pl.BlockSpec((1, 128), const)],
        out_specs=pl.BlockSpec((tq, 128), lambda h, i: (i, h)),
        out_shape=jax.ShapeDtypeStruct((nq, D), BF16),
        scratch_shapes=[pltpu.VMEM((2 * tq, 128), BF16),
                        pltpu.VMEM((2 * tq, max(ts, n_head)), F32), pltpu.VMEM((2 * tq, max(ts, n_head)), F32),
                        pltpu.VMEM((2 * tq, 128), F32), pltpu.VMEM((2 * tq, 2 * VD), F32)],
        compiler_params=_cp(("arbitrary", "arbitrary"), 56),
        name="diff_attention",
    )(q, k, vext, lamv, gsv)


def _proj_ln_kernel(a_ref, w_ref, x_ref, gate_ref, lg_ref, lb_ref, o_ref):
    m = jnp.dot(a_ref[...], w_ref[...], preferred_element_type=F32)
    z = ALPHA * x_ref[...] + gate_ref[0] * m
    o_ref[...] = _layer_norm_rows(z, lg_ref[...], lb_ref[...])


def _proj_ln(a, w, tok, gate, ln_g, ln_b, nctxb):
    nt = tok.shape[0]
    row = lambda i: (i, 0)
    const = lambda i: (0, 0)
    return pl.pallas_call(
        _proj_ln_kernel,
        grid=(nt // TM,),
        in_specs=[pl.BlockSpec((TM, D), row), pl.BlockSpec((D, D), const), pl.BlockSpec((TM, D), row),
                  pl.BlockSpec((1, 1, D), _rowtype(nctxb)), pl.BlockSpec((1, D), const), pl.BlockSpec((1, D), const)],
        out_specs=pl.BlockSpec((TM, D), row),
        out_shape=jax.ShapeDtypeStruct((nt, D), F32),
        compiler_params=_cp(("arbitrary",)),
        name="attn_out_ln",
    )(a, w, tok, gate, ln_g, ln_b)


def _router_kernel(x_ref, sc_ref, sh_ref, wh_ref, wl_ref, b_ref, tri_ref, h_ref, ri_ref, rw_ref, cnt_ref,
                   run_ref):
    @pl.when(pl.program_id(0) == 0)
    def _():
        run_ref[...] = jnp.zeros(run_ref.shape, F32)

    h = x_ref[...] * sc_ref[0] + sh_ref[0]
    hh = h.astype(BF16)
    hl = (h - hh.astype(F32)).astype(BF16)
    bits = lax.bitcast_convert_type(hh.astype(F32), jnp.int32)
    h_ref[...] = (bits[:, D // 2:] & HI16) | lax.shift_right_logical(bits[:, :D // 2], 16)
    v = (jnp.dot(hh, wh_ref[...], preferred_element_type=F32)
         + jnp.dot(hh, wl_ref[...], preferred_element_type=F32)
         + jnp.dot(hl, wh_ref[...], preferred_element_type=F32)) + b_ref[...]
    lane = lax.broadcasted_iota(jnp.int32, v.shape, 1)
    lane_f = lane.astype(F32)
    vals, sels, idxs = [], [], []
    for _ in range(TOP_K):
        mk = jnp.max(v, axis=-1, keepdims=True)
        ik = jnp.min(jnp.where(v == mk, lane_f, 128.0), axis=-1, keepdims=True)
        sel = lane_f == ik
        v = jnp.where(sel, -jnp.inf, v)
        vals.append(mk)
        sels.append(sel)
        idxs.append(ik)
    es = [jnp.exp(mk - vals[0]) for mk in vals]
    den = es[0] + es[1] + es[2] + es[3]
    onehot = jnp.zeros(v.shape, F32)
    for sel in sels:
        onehot = jnp.where(sel, 1.0, onehot)
    prefix = jnp.dot(tri_ref[...], onehot.astype(BF16), preferred_element_type=F32)
    rank_all = prefix + run_ref[0:1, :]
    run_ref[0:1, :] = run_ref[0:1, :] + jnp.sum(onehot, axis=0, keepdims=True)
    ri = jnp.zeros(v.shape, F32)
    rw = jnp.zeros(v.shape, F32)
    for k in range(TOP_K):
        rk = jnp.sum(jnp.where(sels[k], rank_all, 0.0), axis=-1, keepdims=True)
        ri = jnp.where(lane == k, idxs[k], ri)
        ri = jnp.where(lane == TOP_K + k, rk, ri)
        rw = jnp.where(lane == k, es[k] / den, rw)
    ri_ref[...] = ri.astype(jnp.int32)
    rw_ref[...] = rw
    cnt_ref[...] = run_ref[...]


def _router(tok, sc1p, sh, w_hi, w_lo, bias, nctxb):
    nt = tok.shape[0]
    row = lambda i: (i, 0)
    const = lambda i: (0, 0)
    tri = (jnp.arange(TM)[:, None] > jnp.arange(TM)[None, :]).astype(BF16)
    return pl.pallas_call(
        _router_kernel,
        grid=(nt // TM,),
        in_specs=[pl.BlockSpec((TM, D), row),
                  pl.BlockSpec((1, 1, D), _rowtype(nctxb)), pl.BlockSpec((1, 1, D), _rowtype(nctxb)),
                  pl.BlockSpec((D, 128), const), pl.BlockSpec((D, 128), const), pl.BlockSpec((1, 128), const),
                  pl.BlockSpec((TM, TM), const)],
        out_specs=[pl.BlockSpec((TM, D // 2), row), pl.BlockSpec((TM, 128), row), pl.BlockSpec((TM, 128), row),
                   pl.BlockSpec((8, 128), const)],
        out_shape=[jax.ShapeDtypeStruct((nt, D // 2), jnp.int32), jax.ShapeDtypeStruct((nt, 128), jnp.int32),
                   jax.ShapeDtypeStruct((nt, 128), F32), jax.ShapeDtypeStruct((8, 128), F32)],
        scratch_shapes=[pltpu.VMEM((8, 128), F32)],
        compiler_params=_cp(("arbitrary",)),
        name="moe_router",
    )(tok, sc1p, sh, w_hi, w_lo, bias, tri)


def _dispatch_kernel(dest_ref, x_ref, init_ref, o_ref, sem):
    del init_ref
    base = pl.program_id(0) * (TM * TOP_K)

    def row_copy(r, k):
        slot = dest_ref[base + r * TOP_K + k]
        return pltpu.make_async_copy(x_ref.at[pl.ds(r, 1), :], o_ref.at[pl.ds(slot, 1), :], sem)

    def issue(r, carry):
        for k in range(TOP_K):
            row_copy(r, k).start()
        return carry

    def drain(r, carry):
        for k in range(TOP_K):
            row_copy(r, k).wait()
        return carry

    lax.fori_loop(0, TM, issue, 0)
    lax.fori_loop(0, TM, drain, 0)


def _dispatch(dest_flat, hw, cap):
    nt, w = hw.shape
    return pl.pallas_call(
        _dispatch_kernel,
        grid_spec=pltpu.PrefetchScalarGridSpec(
            num_scalar_prefetch=1,
            grid=(nt // TM,),
            in_specs=[pl.BlockSpec((TM, w), lambda i, d: (i, 0)), pl.BlockSpec(memory_space=pl.ANY)],
            out_specs=pl.BlockSpec(memory_space=pl.ANY),
            scratch_shapes=[pltpu.SemaphoreType.DMA(())]),
        out_shape=jax.ShapeDtypeStruct((cap, w), hw.dtype),
        input_output_aliases={2: 0},
        compiler_params=_cp(("arbitrary",)),
        name="moe_dispatch",
    )(dest_flat, hw, jnp.zeros((cap, w), hw.dtype))


def _expert_kernel(be_ref, nu_ref, x_ref, wg_ref, bg_ref, wu_ref, bu_ref, wd_ref, bd_ref, o_ref,
                   wgb_ref, wub_ref, wdb_ref):
    i = pl.program_id(0)
    prev = be_ref[jnp.maximum(i - 1, 0)]
    new_expert = jnp.logical_or(i == 0, be_ref[i] != prev)

    @pl.when(new_expert)
    def _():
        wgb_ref[...] = wg_ref[...].astype(BF16)
        wub_ref[...] = wu_ref[...].astype(BF16)
        wdb_ref[...] = wd_ref[...].astype(BF16)

    @pl.when(i < nu_ref[0])
    def _():
        xw = x_ref[...]
        x = jnp.concatenate([lax.bitcast_convert_type(lax.shift_left(xw, 16), F32),
                             lax.bitcast_convert_type(xw & HI16, F32)], axis=1).astype(BF16)
        g = jnp.minimum(jnp.dot(x, wgb_ref[...], preferred_element_type=F32) + bg_ref[...], SWIGLU_LIMIT)
        u = jnp.clip(jnp.dot(x, wub_ref[...], preferred_element_type=F32) + bu_ref[...], -SWIGLU_LIMIT, SWIGLU_LIMIT)
        act = g * jax.nn.sigmoid(SWIGLU_ALPHA * g) * (u + 1.0)
        y = jnp.dot(act.astype(BF16), wdb_ref[...], preferred_element_type=F32) + bd_ref[...]
        o_ref[...] = y.astype(BF16)

    @pl.when(i >= nu_ref[0])
    def _():
        o_ref[...] = jnp.zeros(o_ref.shape, BF16)


def _experts(layer, blk_e, n_used, xd, w_gate, b_gate, w_up, b_up, w_down, b_down):
    cap = xd.shape[0]
    wspec = pl.BlockSpec((None, None, D, D), lambda i, be, nu: (layer, be[i], 0, 0))
    bspec = pl.BlockSpec((None, None, 1, D), lambda i, be, nu: (layer, be[i], 0, 0))
    row = pl.BlockSpec((TME, D), lambda i, be, nu: (i, 0))
    xrow = pl.BlockSpec((TME, D // 2), lambda i, be, nu: (i, 0))
    b4 = lambda b: b.reshape(DEPTH, N_EXP, 1, D)
    return pl.pallas_call(
        _expert_kernel,
        grid_spec=pltpu.PrefetchScalarGridSpec(
            num_scalar_prefetch=2,
            grid=(cap // TME,),
            in_specs=[xrow, wspec, bspec, wspec, bspec, wspec, bspec],
            out_specs=row,
            scratch_shapes=[pltpu.VMEM((D, D), BF16)] * 3),
        out_shape=jax.ShapeDtypeStruct((cap, D), BF16),
        compiler_params=_cp(("arbitrary",), 56),
        name="moe_experts",
    )(blk_e, n_used, xd, w_gate, b4(b_gate), w_up, b4(b_up), w_down, b4(b_down))


def _moe_finish_kernel(yg_ref, w_ref, x_ref, gate_ref, lg_ref, lb_ref, o_ref):
    w = w_ref[...]
    y = yg_ref[0].astype(F32) * w[:, 0:1]
    for kk in range(1, TOP_K):
        y = y + yg_ref[kk].astype(F32) * w[:, kk:kk + 1]
    z = ALPHA * x_ref[...] + gate_ref[0] * y
    o_ref[...] = _layer_norm_rows(z, lg_ref[...], lb_ref[...])


def _moe_finish(yg, top_w, tok, gate, ln_g, ln_b, nctxb):
    nt = tok.shape[0]
    row = lambda i: (i, 0)
    const = lambda i: (0, 0)
    return pl.pallas_call(
        _moe_finish_kernel,
        grid=(nt // TM,),
        in_specs=[pl.BlockSpec((TOP_K, TM, D), lambda i: (0, i, 0)), pl.BlockSpec((TM, TOP_K), row),
                  pl.BlockSpec((TM, D), row),
                  pl.BlockSpec((1, 1, D), _rowtype(nctxb)), pl.BlockSpec((1, D), const), pl.BlockSpec((1, D), const)],
        out_specs=pl.BlockSpec((TM, D), row),
        out_shape=jax.ShapeDtypeStruct((nt, D), F32),
        compiler_params=_cp(("arbitrary",)),
        name="moe_finish",
    )(yg, top_w, tok, gate, ln_g, ln_b)


def _moe_layer(layer, tok, sc1p, sh, gate, ln_g, ln_b, w_router, b_router, w_gate, b_gate, w_up, b_up,
               w_down, b_down, nctxb):
    nt = tok.shape[0]
    wr = jnp.zeros((D, 128), F32).at[:, :N_EXP].set(w_router.astype(F32))
    wr_hi = wr.astype(BF16)
    wr_lo = (wr - wr_hi.astype(F32)).astype(BF16)
    br = jnp.full((1, 128), -1e30, F32).at[0, :N_EXP].set(b_router.astype(F32))
    hw, ri, rw, cnt = _router(tok, sc1p, sh, wr_hi, wr_lo, br, nctxb)
    top_idx, rank, top_w = ri[:, :TOP_K], ri[:, TOP_K:2 * TOP_K], rw[:, :TOP_K]
    counts = cnt[0, :N_EXP].astype(jnp.int32)
    padded = (counts + TME - 1) // TME * TME
    pad_end = jnp.cumsum(padded)
    pad_start = pad_end - padded
    dest = (pad_start[top_idx] + rank).astype(jnp.int32)
    cap = nt * TOP_K + N_EXP * TME
    nb = cap // TME
    blk_start = jnp.arange(nb, dtype=jnp.int32) * TME
    blk_e = jnp.minimum(jnp.sum((pad_end[None, :] <= blk_start[:, None]).astype(jnp.int32), axis=1), N_EXP - 1)
    n_used = (pad_end[-1] // TME).astype(jnp.int32).reshape(1)
    xd = _dispatch(dest.reshape(-1), hw, cap)
    yd = _experts(layer, blk_e, n_used, xd, w_gate, b_gate, w_up, b_up, w_down, b_down)
    yg = yd.at[dest.T.reshape(-1)].get(mode="promise_in_bounds").reshape(TOP_K, nt, D)
    return _moe_finish(yg, top_w, tok, gate, ln_g, ln_b, nctxb)


def _rope_tables(rows, n_ctx):
    row = jnp.broadcast_to(jnp.arange(rows, dtype=F32)[:, None], (rows, GRID_W)).reshape(-1)
    col = jnp.broadcast_to(jnp.arange(GRID_W, dtype=F32)[None, :], (rows, GRID_W)).reshape(-1)
    theta = ROPE_BASE ** (-jnp.arange(ROPE_F, dtype=F32) / ROPE_F)
    ang = jnp.stack([row[:, None] * theta, col[:, None] * theta], axis=1)
    ang = jnp.stack([ang, ang], axis=2).reshape(rows * GRID_W, HD)
    cos = jnp.concatenate([jnp.ones((n_ctx, HD), F32), jnp.cos(ang)], axis=0)
    sin = jnp.concatenate([jnp.zeros((n_ctx, HD), F32), jnp.sin(ang)], axis=0)
    return jnp.tile(cos, (1, 2)), jnp.tile(sin, (1, 2))


def _rot_cols(w):
    k = w.shape[0]
    wr = w.reshape(k, -1, 2, 2, ROPE_F)
    rot = jnp.stack([-wr[..., 1, :], wr[..., 0, :]], axis=-2)
    return rot.reshape(w.shape)


def kernel(x, c, ctx, c_ctx, ada_w, ada_b, ln_g, ln_b, even_w_in, s5_lam_re, s5_lam_im, s5_log_dt, s5_b_re, s5_b_im, s5_c_re, s5_c_im, s5_d, s5_w_glu, s5_b_glu, fnet_w, fnet_b, even_w_out, odd_w_qkv, odd_w_o, da_lq1, da_lk1, da_lq2, da_lk2, da_subln_g, router_w, router_b, moe_w_gate, moe_b_gate, moe_w_up, moe_b_up, moe_w_down, moe_b_down):
    seq = x.shape[1]
    n_ctx = ctx.shape[1]
    assert x.shape[0] == 1 and n_ctx % TM == 0 and seq % TM == 0 and n_ctx == TM
    nctxb = n_ctx // TM
    nt = n_ctx + seq
    nc = nt // CH
    ncc = n_ctx // CH
    tq_lat = 1024 if seq % 1024 == 0 else TM
    ts_lat = 512 if seq % 1024 == 0 else TM
    assert seq % (2 * ts_lat) == 0

    tok = jnp.concatenate([ctx[0], x[0]], axis=0).astype(F32)
    cond8 = jnp.zeros((8, D), F32).at[0].set(c_ctx.astype(F32)).at[1].set(c[0].astype(F32))
    mods = _ada_mods(cond8, ada_w, ada_b)[:, :2].reshape(DEPTH, 2, 6, 1, D)
    cos, sin = _rope_tables(seq // GRID_W, n_ctx)
    c128, s128 = _cos_sin(FN_C, FN_C, FN_C)

    for l in range(DEPTH):
        i = l // 2
        m = mods[l]
        shift_a, scale_a, gate_a, shift_b, scale_b, gate_b = (m[:, j] for j in range(6))
        lg0, lb0 = ln_g[l, 0].reshape(1, D), ln_b[l, 0].reshape(1, D)
        lg1, lb1 = ln_g[l, 1].reshape(1, D), ln_b[l, 1].reshape(1, D)
        if l % 2 == 0:
            wf = jnp.concatenate([jnp.einsum('ab,gbd->gad', c128, fnet_w[i].astype(F32), precision=HIGHEST),
                                  jnp.einsum('ab,gbd->gad', s128, fnet_w[i].astype(F32), precision=HIGHEST)],
                                 axis=-1).astype(BF16)
            u, p, q = _even_in(tok, 1.0 + scale_a, shift_a, even_w_in[i].astype(BF16), wf, nctxb)
            m1, w2, avec = _s5_matrices(s5_lam_re[i], s5_lam_im[i], s5_log_dt[i], s5_b_re[i], s5_b_im[i],
                                        s5_c_re[i], s5_c_im[i], s5_d[i])
            yi, sfr, sfi, sbr, sbi = _s5_in(u, m1)
            hfr, hfi, hbr, hbi = _s5_scan(sfr, sfi, sbr, sbi, avec, ncc)
            ys = _s5_out(yi, hfr, hfi, hbr, hbi, w2)
            g = _fnet_positions(p, q, fnet_b[i], n_ctx)
            tok = _even_out(ys, g, tok, s5_w_glu[i].astype(BF16), s5_b_glu[i].reshape(1, S5_W).astype(F32),
                            even_w_out[i].astype(BF16), gate_a, lg0, lb0, nctxb)
        else:
            lam_init = 0.8 - 0.6 * math.exp(-0.3 * l)
            lam = (jnp.exp(jnp.sum(da_lq1[i].astype(F32) * da_lk1[i].astype(F32)))
                   - jnp.exp(jnp.sum(da_lq2[i].astype(F32) * da_lk2[i].astype(F32))) + lam_init)
            wq, wk, wv = odd_w_qkv[i][:, :D], odd_w_qkv[i][:, D:2 * D], odd_w_qkv[i][:, 2 * D:]
            w5 = jnp.concatenate([wq, _rot_cols(wq), wk, _rot_cols(wk), wv], axis=1).astype(BF16)
            qb, kb, vb = _qkv(tok, 1.0 + scale_a, shift_a, w5, cos, sin, nctxb)
            lamv = jnp.full((1, VD), lam, F32)
            gsv = (da_subln_g[i].astype(F32) * (1.0 - lam_init)).reshape(1, VD)
            on_lat = _attention(qb[n_ctx:], kb, vb, lamv, gsv, tq_lat, n_ctx, ts_lat)
            on_ctx = _attention(qb[:n_ctx], kb[:n_ctx], vb[:n_ctx], lamv, gsv, n_ctx, n_ctx, 0)
            on = jnp.concatenate([on_ctx, on_lat], axis=0)
            tok = _proj_ln(on, odd_w_o[i].astype(BF16), tok, gate_a, lg0, lb0, nctxb)
        tok = _moe_layer(l, tok, 1.0 + scale_b, shift_b, gate_b, lg1, lb1, router_w[l], router_b[l],
                         moe_w_gate, moe_b_gate, moe_w_up, moe_b_up, moe_w_down, moe_b_down, nctxb)
    return tok[n_ctx:].reshape(1, seq, D).astype(x.dtype)
```

```python
import functools
import math

import jax
import jax.numpy as jnp
from jax import lax
from jax.experimental import pallas as pl
from jax.experimental.pallas import tpu as pltpu

F32 = jnp.float32
BF16 = jnp.bfloat16
HIGHEST = lax.Precision.HIGHEST

D = 1024
DEPTH = 4
GRID_W = 64
S5_W = 512
S5_G = 32
S5_C = 16
S5_N = 64
FN_W = 512
FN_G = 4
FN_C = 128
HEADS = 8
HD = 64
VD = 128
ROPE_BASE = 10000.0
ROPE_F = 16
N_EXP = 32
TOP_K = 4
SWIGLU_LIMIT = 7.0
SWIGLU_ALPHA = 1.702
ALPHA = (2.0 * DEPTH) ** 0.25
LN_EPS = 1e-5
LOG2E = 1.4426950408889634
HI16 = -65536

TM = 256
TME = 512
CH = 16
MIB = 1024 * 1024


def _cp(sem, vmem_mib=48):
    return pltpu.CompilerParams(dimension_semantics=sem, vmem_limit_bytes=vmem_mib * MIB)


def _rowtype(nctxb):
    return lambda i: (jnp.where(i >= nctxb, 1, 0), 0, 0)


def _layer_norm_rows(z, g, b):
    mu = jnp.mean(z, axis=-1, keepdims=True)
    zc = z - mu
    var = jnp.mean(zc * zc, axis=-1, keepdims=True)
    return zc * lax.rsqrt(var + LN_EPS) * g + b


def _ada_kernel(c_ref, w_ref, b_ref, o_ref):
    c = c_ref[...]
    a = c * jax.nn.sigmoid(c)
    o_ref[0] = jnp.dot(a, w_ref[0], preferred_element_type=F32, precision=HIGHEST) + b_ref[0]


def _ada_mods(cond8, ada_w, ada_b):
    tn = 1536
    return pl.pallas_call(
        _ada_kernel,
        grid=(DEPTH, 6 * D // tn),
        in_specs=[pl.BlockSpec((8, D), lambda l, j: (0, 0)),
                  pl.BlockSpec((1, D, tn), lambda l, j: (l, 0, j)),
                  pl.BlockSpec((1, 1, tn), lambda l, j: (l, 0, j))],
        out_specs=pl.BlockSpec((1, 8, tn), lambda l, j: (l, 0, j)),
        out_shape=jax.ShapeDtypeStruct((DEPTH, 8, 6 * D), F32),
        compiler_params=_cp(("arbitrary", "arbitrary")),
        name="ada_mods",
    )(cond8, ada_w, ada_b.reshape(DEPTH, 1, 6 * D))


def _even_in_kernel(x_ref, sc_ref, sh_ref, w_ref, wf_ref, u_ref, p_ref, q_ref):
    h = x_ref[...] * sc_ref[0] + sh_ref[0]
    z = jnp.dot(h.astype(BF16), w_ref[...], preferred_element_type=F32)
    u_ref[...] = z[:, :S5_W]
    for g in range(FN_G):
        f = z[:, S5_W + FN_C * g:S5_W + FN_C * (g + 1)]
        fc = f - jnp.mean(f, axis=-1, keepdims=True)
        fn = fc * lax.rsqrt(jnp.mean(fc * fc, axis=-1, keepdims=True) + LN_EPS)
        pq = jnp.dot(fn.astype(BF16), wf_ref[g], preferred_element_type=F32)
        p_ref[:, FN_C * g:FN_C * (g + 1)] = pq[:, :FN_C]
        q_ref[:, FN_C * g:FN_C * (g + 1)] = pq[:, FN_C:]


def _even_in(tok, sc1p, sh, w_in, wf, nctxb):
    nt = tok.shape[0]
    row = lambda i: (i, 0)
    return pl.pallas_call(
        _even_in_kernel,
        grid=(nt // TM,),
        in_specs=[pl.BlockSpec((TM, D), row),
                  pl.BlockSpec((1, 1, D), _rowtype(nctxb)),
                  pl.BlockSpec((1, 1, D), _rowtype(nctxb)),
                  pl.BlockSpec((D, D), lambda i: (0, 0)),
                  pl.BlockSpec((FN_G, FN_C, 2 * FN_C), lambda i: (0, 0, 0))],
        out_specs=[pl.BlockSpec((TM, S5_W), row),
                   pl.BlockSpec((TM, FN_W), row),
                   pl.BlockSpec((TM, FN_W), row)],
        out_shape=[jax.ShapeDtypeStruct((nt, S5_W), F32),
                   jax.ShapeDtypeStruct((nt, FN_W), F32),
                   jax.ShapeDtypeStruct((nt, FN_W), F32)],
        compiler_params=_cp(("arbitrary",)),
        name="even_in",
    )(tok, sc1p, sh, w_in, wf)


def _lane_iota(shape):
    return lax.broadcasted_iota(jnp.int32, shape, 1)


def _s5_in_kernel(x_ref, m1_ref, yi_ref, sfr_ref, sfi_ref, sbr_ref, sbi_ref):
    nc = x_ref.shape[0] // CH
    lane = _lane_iota((nc, 128)) // 32
    for pp in range(4):
        @pl.when(pl.program_id(1) == pp)
        def _(pp=pp):
            tiles = []
            for q in range(4):
                acc = None
                for r in range(4):
                    xs = x_ref[pl.ds(4 * q + r, nc, stride=CH), :]
                    shift = ((r - pp) * 32) % 128
                    if shift:
                        xs = pltpu.roll(xs, shift, 1)
                    acc = xs if acc is None else jnp.where(lane == r, xs, acc)
                tiles.append(acc.astype(BF16))
            u = jnp.concatenate(tiles, axis=1)
            r_all = jnp.dot(u, m1_ref[0], preferred_element_type=F32)
            yi_ref[0] = r_all[:, :512]
            sfr_ref[...] = r_all[:, 512:640]
            sfi_ref[...] = r_all[:, 640:768]
            sbr_ref[...] = r_all[:, 768:896]
            sbi_ref[...] = r_all[:, 896:1024]


def _s5_in(u, m1):
    nt = u.shape[0]
    nc = nt // CH
    sspec = pl.BlockSpec((nc, 128), lambda i, j: (0, 4 * i + j))
    sshape = jax.ShapeDtypeStruct((nc, 2048), F32)
    return pl.pallas_call(
        _s5_in_kernel,
        grid=(4, 4),
        in_specs=[pl.BlockSpec((nt, 128), lambda i, j: (0, i)),
                  pl.BlockSpec((1, 512, 1024), lambda i, j: (4 * i + j, 0, 0))],
        out_specs=[pl.BlockSpec((1, nc, 512), lambda i, j: (4 * i + j, 0, 0)), sspec, sspec, sspec, sspec],
        out_shape=[jax.ShapeDtypeStruct((16, nc, 512), F32), sshape, sshape, sshape, sshape],
        compiler_params=_cp(("arbitrary", "arbitrary")),
        name="s5_in",
    )(u, m1)


def _s5_scan_kernel(ncc, sfr_ref, sfi_ref, sbr_ref, sbi_ref, a_ref,
                    hfr_ref, hfi_ref, hbr_ref, hbi_ref):
    nc = sfr_ref.shape[0]
    afr, afi, abr, abi = a_ref[0:1, :], a_ref[1:2, :], a_ref[2:3, :], a_ref[3:4, :]

    def body(i, carry):
        fr, fi, br, bi = carry
        jb = jnp.where(i < ncc, ncc - 1 - i, nc - 1 - i + ncc)
        hfr_ref[pl.ds(i, 1), :] = fr
        hfi_ref[pl.ds(i, 1), :] = fi
        hbr_ref[pl.ds(jb, 1), :] = br
        hbi_ref[pl.ds(jb, 1), :] = bi
        sr, si = sfr_ref[pl.ds(i, 1), :], sfi_ref[pl.ds(i, 1), :]
        tr, ti = sbr_ref[pl.ds(jb, 1), :], sbi_ref[pl.ds(jb, 1), :]
        return (afr * fr - afi * fi + sr, afr * fi + afi * fr + si,
                abr * br - abi * bi + tr, abr * bi + abi * br + ti)

    z = jnp.zeros((1, sfr_ref.shape[1]), F32)
    lax.fori_loop(0, nc, body, (z, z, z, z))


def _s5_scan(sfr, sfi, sbr, sbi, avec, ncc):
    nc = sfr.shape[0]
    spec = pl.BlockSpec((nc, 512), lambda i: (0, i))
    shape = jax.ShapeDtypeStruct((nc, 2048), F32)
    return pl.pallas_call(
        functools.partial(_s5_scan_kernel, ncc),
        grid=(4,),
        in_specs=[spec, spec, spec, spec, pl.BlockSpec((8, 512), lambda i: (0, i))],
        out_specs=[spec, spec, spec, spec],
        out_shape=[shape, shape, shape, shape],
        compiler_params=_cp(("arbitrary",)),
        name="s5_scan",
    )(sfr, sfi, sbr, sbi, avec)


def _s5_out_kernel(yi_ref, hfr_ref, hfi_ref, hbr_ref, hbi_ref, w2_ref, o_ref, yp_ref):
    nc = yi_ref.shape[1]
    pp_dyn = pl.program_id(1)
    hcat = jnp.concatenate([hfr_ref[...], hfi_ref[...], hbr_ref[...], hbi_ref[...]], axis=1).astype(BF16)
    yp_ref[pp_dyn] = yi_ref[0] + jnp.dot(hcat, w2_ref[0], preferred_element_type=F32)

    @pl.when(pp_dyn == 3)
    def _():
        lane = _lane_iota((nc, 128)) // 32
        for s in range(CH):
            acc = None
            for pp in range(4):
                ys = yp_ref[pp, :, (s // 4) * 128:(s // 4 + 1) * 128]
                shift = ((pp - s % 4) * 32) % 128
                if shift:
                    ys = pltpu.roll(ys, shift, 1)
                acc = ys if acc is None else jnp.where(lane == pp, ys, acc)
            o_ref[pl.ds(s, nc, stride=CH), :] = acc


def _s5_out(yi, hfr, hfi, hbr, hbi, w2):
    nc = yi.shape[1]
    hspec = pl.BlockSpec((nc, 128), lambda i, j: (0, 4 * i + j))
    return pl.pallas_call(
        _s5_out_kernel,
        grid=(4, 4),
        in_specs=[pl.BlockSpec((1, nc, 512), lambda i, j: (4 * i + j, 0, 0)), hspec, hspec, hspec, hspec,
                  pl.BlockSpec((1, 512, 512), lambda i, j: (4 * i + j, 0, 0))],
        out_specs=pl.BlockSpec((nc * CH, 128), lambda i, j: (0, i)),
        out_shape=jax.ShapeDtypeStruct((nc * CH, S5_W), F32),
        scratch_shapes=[pltpu.VMEM((4, nc, 512), F32)],
        compiler_params=_cp(("arbitrary", "arbitrary")),
        name="s5_out",
    )(yi, hfr, hfi, hbr, hbi, w2)


def _toeplitz_kernel(k_ref, o_ref):
    lane = _lane_iota((32, 128)) // 32
    for s in range(CH):
        for q in range(CH // 4):
            acc = None
            for j in range(4):
                piece = k_ref[0, 4 * q + j - s + CH - 1]
                acc = piece if acc is None else jnp.where(lane == j, piece, acc)
            o_ref[0, s * 32:(s + 1) * 32, q * 128:(q + 1) * 128] = acc.astype(BF16)


def _toeplitz(kblk_rep):
    npair, nlag = kblk_rep.shape[:2]
    return pl.pallas_call(
        _toeplitz_kernel,
        grid=(npair,),
        in_specs=[pl.BlockSpec((1, nlag, 32, 128), lambda p: (p, 0, 0, 0))],
        out_specs=pl.BlockSpec((1, CH * 32, CH * 32), lambda p: (p, 0, 0)),
        out_shape=jax.ShapeDtypeStruct((npair, CH * 32, CH * 32), BF16),
        compiler_params=_cp(("arbitrary",)),
        name="s5_toeplitz",
    )(kblk_rep)


def _s5_matrices(lam_re, lam_im, log_dt, b_re, b_im, c_re, c_im, d_skip):
    lam = lax.complex(lam_re.astype(F32), lam_im.astype(F32))
    dt = jnp.exp(log_dt.astype(F32))[..., None]
    ldt = lam * dt
    lam_bar = jnp.exp(ldt)
    bb = ((lam_bar - 1.0) / lam)[..., None] * lax.complex(b_re.astype(F32), b_im.astype(F32))
    cc = lax.complex(c_re.astype(F32), c_im.astype(F32))
    ks = jnp.arange(CH + 1, dtype=F32)
    pw = jnp.exp(ldt[..., None] * ks)
    kk = jnp.real(jnp.einsum('dgcn,dgnk,dgne->dkgce', cc, pw[..., :CH], bb, precision=HIGHEST))
    k0 = kk[0, 0] + kk[1, 0] + jnp.eye(S5_C, dtype=F32)[None] * d_skip.astype(F32)[:, :, None]
    kall = jnp.concatenate([kk[1, 1:][::-1], k0[None], kk[0, 1:]], axis=0)
    eye2 = jnp.eye(2, dtype=F32)
    kt = kall.transpose(1, 0, 3, 2).reshape(16, 2, 2 * CH - 1, S5_C, S5_C)
    kblk = jnp.einsum('pglcd,gh->plgchd', kt, eye2).reshape(16, 2 * CH - 1, 32, 32)
    tp = _toeplitz(jnp.tile(kblk, (1, 1, 1, 4)))
    w1f = jnp.einsum('gns,gnc->gscn', pw[0][..., :CH][..., ::-1], bb[0])
    w1b = jnp.einsum('gns,gnc->gscn', pw[1][..., :CH], bb[1])

    def pair_cols(w):
        w = w.reshape(16, 2, CH, S5_C, S5_N)
        return jnp.einsum('pgscn,gh->psgchn', w, eye2).reshape(16, 512, 128)

    m1 = jnp.concatenate([tp, pair_cols(jnp.real(w1f)).astype(BF16), pair_cols(jnp.imag(w1f)).astype(BF16),
                          pair_cols(jnp.real(w1b)).astype(BF16), pair_cols(jnp.imag(w1b)).astype(BF16)], axis=-1)
    cpf = jnp.einsum('gcn,gnt->gntc', cc[0], pw[0][..., 1:])
    cpb = jnp.einsum('gcn,gnt->gntc', cc[1], pw[1][..., 1:][..., ::-1])

    def pair_rows(w):
        w = w.reshape(16, 2, S5_N, CH, S5_C)
        return jnp.einsum('pgntc,gh->pgnthc', w, eye2).reshape(16, 128, 512)

    w2 = jnp.concatenate([pair_rows(jnp.real(cpf)), pair_rows(-jnp.imag(cpf)),
                          pair_rows(jnp.real(cpb)), pair_rows(-jnp.imag(cpb))], axis=1)
    a_f, a_b = pw[0][..., CH].reshape(-1), pw[1][..., CH].reshape(-1)
    z = jnp.zeros_like(jnp.real(a_f))
    avec = jnp.stack([jnp.real(a_f), jnp.imag(a_f), jnp.real(a_b), jnp.imag(a_b), z, z, z, z])
    return m1.astype(BF16), w2.astype(BF16), avec


def _dft_lat_kernel(n1, n2, scale, p_ref, q_ref, m_ref, c2_ref, s2_ref, rot_ref, b_ref, o_ref, zr_ref, zi_ref):
    length = n1 * n2
    rr, ri = rot_ref[0], rot_ref[1]

    def phase1(blk, tw):
        b0 = pl.multiple_of(blk * 8, 8)
        pv = p_ref.at[pl.ds(b0, length - n2 + 8), :]
        qv = q_ref.at[pl.ds(b0, length - n2 + 8), :]
        zrv = zr_ref.at[pl.ds(b0, length - n2 + 8), :]
        ziv = zi_ref.at[pl.ds(b0, length - n2 + 8), :]
        twr, twi = tw
        for j in range(8):
            rows = pl.ds(j, n1, stride=n2)
            x = jnp.concatenate([pv[rows, :], qv[rows, :]], axis=0).astype(BF16)
            z = jnp.dot(m_ref[...], x, preferred_element_type=F32)
            zr, zi = z[:n1], z[n1:]
            zrv[rows, :] = zr * twr - zi * twi
            ziv[rows, :] = zr * twi + zi * twr
            twr, twi = twr * rr - twi * ri, twr * ri + twi * rr
        return twr, twi

    lax.fori_loop(0, n2 // 8, phase1, (jnp.ones((n1, 128), F32), jnp.zeros((n1, 128), F32)))

    def phase2(blk, carry):
        k0 = pl.multiple_of(blk * 8, 8)
        ov = o_ref.at[pl.ds(k0, length - n1 + 8), :]
        for j in range(8):
            r0 = pl.multiple_of((k0 + j) * n2, 8)
            zr = zr_ref[pl.ds(r0, n2), :].astype(BF16)
            zi = zi_ref[pl.ds(r0, n2), :].astype(BF16)
            g = (jnp.dot(c2_ref[...], zr, preferred_element_type=F32)
                 + jnp.dot(s2_ref[...], zi, preferred_element_type=F32))
            ov[pl.ds(j, n2, stride=n1), :] = g * scale + b_ref[...]
        return carry

    lax.fori_loop(0, n1 // 8, phase2, 0)


def _dft_lat(p, q, m, c2, s2, rot, bias, n1, n2, scale):
    length, w = p.shape
    col = pl.BlockSpec((length, 128), lambda j: (0, j), pipeline_mode=pl.Buffered(1))
    const = lambda shape: pl.BlockSpec(shape, lambda j: (0,) * len(shape))
    return pl.pallas_call(
        functools.partial(_dft_lat_kernel, n1, n2, scale),
        grid=(w // 128,),
        in_specs=[col, col, const((2 * n1, 2 * n1)), const((n2, n2)), const((n2, n2)), const((2, n1, 128)),
                  pl.BlockSpec((1, 128), lambda j: (0, j))],
        out_specs=pl.BlockSpec((length, 128), lambda j: (0, j)),
        out_shape=jax.ShapeDtypeStruct((length, w), F32),
        scratch_shapes=[pltpu.VMEM((length, 128), F32), pltpu.VMEM((length, 128), F32)],
        compiler_params=_cp(("arbitrary",), 56),
        name="fnet_dft_lat",
    )(p, q, m, c2, s2, rot, bias)


def _dft_small_kernel(scale, c_ref, s_ref, p_ref, q_ref, b_ref, o_ref):
    g = (jnp.dot(c_ref[...], p_ref[...].astype(BF16), preferred_element_type=F32)
         - jnp.dot(s_ref[...], q_ref[...].astype(BF16), preferred_element_type=F32))
    o_ref[...] = g * scale + b_ref[...]


def _dft_small(cm, sm, p, q, bias, scale):
    n, w = p.shape
    full = lambda shape: pl.BlockSpec(shape, lambda i: (0,) * len(shape))
    return pl.pallas_call(
        functools.partial(_dft_small_kernel, scale),
        grid=(1,),
        in_specs=[full((n, n)), full((n, n)), full((n, w)), full((n, w)), full((1, w))],
        out_specs=full((n, w)),
        out_shape=jax.ShapeDtypeStruct((n, w), F32),
        compiler_params=_cp(("arbitrary",)),
        name="fnet_dft_ctx",
    )(cm, sm, p, q, bias)


def _cos_sin(n_rows, n_cols, period):
    ang = (2.0 * math.pi / period) * ((jnp.arange(n_rows)[:, None] * jnp.arange(n_cols)[None, :]) % period).astype(F32)
    return jnp.cos(ang), jnp.sin(ang)


def _fnet_positions(p, q, bias, n_ctx):
    pc, qc, pl_, ql_ = p[:n_ctx], q[:n_ctx], p[n_ctx:], q[n_ctx:]
    length = pl_.shape[0]
    n1 = 1 << ((length.bit_length() - 1) // 2)
    n2 = length // n1
    assert n1 * n2 == length and n1 % 8 == 0
    bias = bias.reshape(1, FN_W).astype(F32)
    cc, sc = _cos_sin(n_ctx, n_ctx, n_ctx)
    g_ctx = _dft_small(cc.astype(BF16), sc.astype(BF16), pc, qc, bias, 1.0 / math.sqrt(FN_C * n_ctx))
    c1, s1 = _cos_sin(n1, n1, n1)
    m = jnp.concatenate([jnp.concatenate([c1, -s1], axis=1),
                         jnp.concatenate([-s1, -c1], axis=1)], axis=0).astype(BF16)
    ang = (2.0 * math.pi / length) * jnp.arange(n1, dtype=F32)
    rot = jnp.stack([jnp.broadcast_to(jnp.cos(ang)[:, None], (n1, 128)),
                     jnp.broadcast_to(-jnp.sin(ang)[:, None], (n1, 128))])
    c2, s2 = _cos_sin(n2, n2, n2)
    g_lat = _dft_lat(pl_, ql_, m, c2.astype(BF16), s2.astype(BF16), rot, bias, n1, n2,
                     1.0 / math.sqrt(FN_C * length))
    return jnp.concatenate([g_ctx, g_lat], axis=0)


def _even_out_kernel(ys_ref, g_ref, x_ref, wglu_ref, bglu_ref, wout_ref, gate_ref, lg_ref, lb_ref, o_ref):
    ys = ys_ref[...]
    c0 = math.sqrt(2.0 / math.pi)
    y = 0.5 * ys * (1.0 + jnp.tanh(c0 * (ys + 0.044715 * (ys * ys * ys))))
    gl = jnp.dot(y.astype(BF16), wglu_ref[...], preferred_element_type=F32) + bglu_ref[...]
    y2 = y * jax.nn.sigmoid(gl)
    m = (jnp.dot(y2.astype(BF16), wout_ref[:S5_W, :], preferred_element_type=F32)
         + jnp.dot(g_ref[...].astype(BF16), wout_ref[S5_W:, :], preferred_element_type=F32))
    z = ALPHA * x_ref[...] + gate_ref[0] * m
    o_ref[...] = _layer_norm_rows(z, lg_ref[...], lb_ref[...])


def _even_out(ys, g, tok, w_glu, b_glu, w_out, gate, ln_g, ln_b, nctxb):
    nt = tok.shape[0]
    row = lambda i: (i, 0)
    const = lambda i: (0, 0)
    return pl.pallas_call(
        _even_out_kernel,
        grid=(nt // TM,),
        in_specs=[pl.BlockSpec((TM, S5_W), row), pl.BlockSpec((TM, FN_W), row), pl.BlockSpec((TM, D), row),
                  pl.BlockSpec((S5_W, S5_W), const), pl.BlockSpec((1, S5_W), const),
                  pl.BlockSpec((D, D), const), pl.BlockSpec((1, 1, D), _rowtype(nctxb)),
                  pl.BlockSpec((1, D), const), pl.BlockSpec((1, D), const)],
        out_specs=pl.BlockSpec((TM, D), row),
        out_shape=jax.ShapeDtypeStruct((nt, D), F32),
        compiler_params=_cp(("arbitrary",)),
        name="even_out",
    )(ys, g, tok, w_glu, b_glu, w_out, gate, ln_g, ln_b)


def _qkv_kernel(x_ref, sc_ref, sh_ref, w_ref, cos_ref, sin_ref, q_ref, k_ref, v_ref):
    h = (x_ref[...] * sc_ref[0] + sh_ref[0]).astype(BF16)
    cos, sin = cos_ref[...], sin_ref[...]
    z = jnp.dot(h, w_ref[...], preferred_element_type=F32)
    for hh in range(HEADS):
        sl = slice(hh * 128, (hh + 1) * 128)
        q = z[:, sl] * cos + z[:, D + hh * 128:D + (hh + 1) * 128] * sin
        k = z[:, 2 * D + hh * 128:2 * D + (hh + 1) * 128] * cos + z[:, 3 * D + hh * 128:3 * D + (hh + 1) * 128] * sin
        q_ref[:, sl] = (q * (HD ** -0.5 * LOG2E)).astype(BF16)
        k_ref[:, sl] = k.astype(BF16)
        v_ref[:, 2 * hh * VD:(2 * hh + 1) * VD] = z[:, 4 * D + hh * VD:4 * D + (hh + 1) * VD].astype(BF16)
        v_ref[:, (2 * hh + 1) * VD:(2 * hh + 2) * VD] = jnp.ones((z.shape[0], VD), BF16)


def _qkv(tok, sc1p, sh, w5, cos, sin, nctxb):
    nt = tok.shape[0]
    row = lambda i: (i, 0)
    o = jax.ShapeDtypeStruct((nt, D), BF16)
    ov = jax.ShapeDtypeStruct((nt, 2 * D), BF16)
    return pl.pallas_call(
        _qkv_kernel,
        grid=(nt // TM,),
        in_specs=[pl.BlockSpec((TM, D), row),
                  pl.BlockSpec((1, 1, D), _rowtype(nctxb)),
                  pl.BlockSpec((1, 1, D), _rowtype(nctxb)),
                  pl.BlockSpec((D, 5 * D), lambda i: (0, 0)),
                  pl.BlockSpec((TM, 128), row), pl.BlockSpec((TM, 128), row)],
        out_specs=[pl.BlockSpec((TM, D), row), pl.BlockSpec((TM, D), row), pl.BlockSpec((TM, 2 * D), row)],
        out_shape=[o, o, ov],
        compiler_params=_cp(("arbitrary",)),
        name="qkv_rope",
    )(tok, sc1p, sh, w5, cos, sin)


def _attn_kernel(n_head, ts, n_pairs, q_ref, k_ref, v_ref, lam_ref, gs_ref, o_ref,
                 qq_ref, sa_ref, sb_ref, m_ref, acc_ref):
    tq = q_ref.shape[0]
    q = q_ref[...]
    lane = lax.broadcasted_iota(jnp.int32, q.shape, 1)
    zero = jnp.zeros_like(q)
    qq_ref[:tq, :] = jnp.where(lane < HD, q, zero)
    qq_ref[tq:, :] = jnp.where(lane >= HD, q, zero)
    m_ref[...] = jnp.full(m_ref.shape, -1e30, F32)
    acc_ref[...] = jnp.zeros(acc_ref.shape, F32)

    def scores(off, size, dst_ref):
        dst_ref[:, :size] = lax.dot_general(qq_ref[...], k_ref[pl.ds(off, size), :],
                                            (((1,), (1,)), ((), ())), preferred_element_type=F32)

    def consume(off, size, src_ref):
        tiles = [src_ref[:, t * 128:(t + 1) * 128] for t in range(size // 128)]
        mx = functools.reduce(jnp.maximum, tiles)
        m_old = m_ref[...]
        m_new = jnp.maximum(m_old, jnp.max(mx, axis=-1, keepdims=True))
        alpha = jnp.exp2(m_old - m_new)
        p = jnp.concatenate([jnp.exp2(t - m_new).astype(BF16) for t in tiles], axis=1)
        pv = jnp.dot(p, v_ref[pl.ds(off, size), :], preferred_element_type=F32)
        acc_ref[...] = jnp.concatenate([alpha, alpha], axis=1) * acc_ref[...] + pv
        m_ref[...] = m_new

    scores(0, n_head, sa_ref)
    if n_pairs:
        scores(n_head, ts, sb_ref)
    consume(0, n_head, sa_ref)

    def pair(i, last):
        off0 = pl.multiple_of(n_head + (2 * i) * ts, 128)
        off1 = pl.multiple_of(n_head + (2 * i + 1) * ts, 128)
        scores(off1, ts, sa_ref)
        consume(off0, ts, sb_ref)
        if not last:
            scores(pl.multiple_of(n_head + (2 * i + 2) * ts, 128), ts, sb_ref)
        consume(off1, ts, sa_ref)

    if n_pairs:
        def body(i, carry):
            pair(2 * i, False)
            pair(2 * i + 1, False)
            return carry
        lax.fori_loop(0, (n_pairs - 1) // 2, body, 0)
        if (n_pairs - 1) % 2:
            pair(n_pairs - 2, False)
        pair(n_pairs - 1, True)

    acc = acc_ref[...]
    o12 = acc[:, :VD] / acc[:, VD:]
    o = o12[:tq] - lam_ref[...] * o12[tq:]
    o = o * lax.rsqrt(jnp.mean(o * o, axis=-1, keepdims=True) + LN_EPS)
    o_ref[...] = (o * gs_ref[...]).astype(BF16)


def _attention(q, k, vext, lamv, gsv, tq, n_head, ts):
    nq, nk = q.shape[0], k.shape[0]
    n_sub = (nk - n_head) // ts if ts else 0
    assert n_head + n_sub * ts == nk and n_sub % 2 == 0 and nq % tq == 0
    const = lambda h, i: (0, 0)
    return pl.pallas_call(
        functools.partial(_attn_kernel, n_head, ts, n_sub // 2),
        grid=(HEADS, nq // tq),
        in_specs=[pl.BlockSpec((tq, 128), lambda h, i: (i, h)),
                  pl.BlockSpec((nk, 128), lambda h, i: (0, h), pipeline_mode=pl.Buffered(1)),
                  pl.BlockSpec((nk, 2 * VD), lambda h, i: (0, h), pipeline_mode=pl.Buffered(1)),
                  pl.BlockSpec((1, 128), const), pl.BlockSpec((1, 128), const)],
        out_specs=pl.BlockSpec((tq, 128), lambda h, i: (i, h)),
        out_shape=jax.ShapeDtypeStruct((nq, D), BF16),
        scratch_shapes=[pltpu.VMEM((2 * tq, 128), BF16),
                        pltpu.VMEM((2 * tq, max(ts, n_head)), F32), pltpu.VMEM((2 * tq, max(ts, n_head)), F32),
                        pltpu.VMEM((2 * tq, 128), F32), pltpu.VMEM((2 * tq, 2 * VD), F32)],
        compiler_params=_cp(("arbitrary", "arbitrary"), 56),
        name="diff_attention",
    )(q, k, vext, lamv, gsv)


def _proj_ln_kernel(a_ref, w_ref, x_ref, gate_ref, lg_ref, lb_ref, o_ref):
    m = jnp.dot(a_ref[...], w_ref[...], preferred_element_type=F32)
    z = ALPHA * x_ref[...] + gate_ref[0] * m
    o_ref[...] = _layer_norm_rows(z, lg_ref[...], lb_ref[...])


def _proj_ln(a, w, tok, gate, ln_g, ln_b, nctxb):
    nt = tok.shape[0]
    row = lambda i: (i, 0)
    const = lambda i: (0, 0)
    return pl.pallas_call(
        _proj_ln_kernel,
        grid=(nt // TM,),
        in_specs=[pl.BlockSpec((TM, D), row), pl.BlockSpec((D, D), const), pl.BlockSpec((TM, D), row),
                  pl.BlockSpec((1, 1, D), _rowtype(nctxb)), pl.BlockSpec((1, D), const), pl.BlockSpec((1, D), const)],
        out_specs=pl.BlockSpec((TM, D), row),
        out_shape=jax.ShapeDtypeStruct((nt, D), F32),
        compiler_params=_cp(("arbitrary",)),
        name="attn_out_ln",
    )(a, w, tok, gate, ln_g, ln_b)


def _router_kernel(x_ref, sc_ref, sh_ref, wh_ref, wl_ref, b_ref, tri_ref, h_ref, ri_ref, rw_ref, cnt_ref,
                   run_ref):
    @pl.when(pl.program_id(0) == 0)
    def _():
        run_ref[...] = jnp.zeros(run_ref.shape, F32)

    h = x_ref[...] * sc_ref[0] + sh_ref[0]
    hh = h.astype(BF16)
    hl = (h - hh.astype(F32)).astype(BF16)
    bits = lax.bitcast_convert_type(hh.astype(F32), jnp.int32)
    h_ref[...] = (bits[:, D // 2:] & HI16) | lax.shift_right_logical(bits[:, :D // 2], 16)
    v = (jnp.dot(hh, wh_ref[...], preferred_element_type=F32)
         + jnp.dot(hh, wl_ref[...], preferred_element_type=F32)
         + jnp.dot(hl, wh_ref[...], preferred_element_type=F32)) + b_ref[...]
    lane = lax.broadcasted_iota(jnp.int32, v.shape, 1)
    lane_f = lane.astype(F32)
    vals, sels, idxs = [], [], []
    for _ in range(TOP_K):
        mk = jnp.max(v, axis=-1, keepdims=True)
        ik = jnp.min(jnp.where(v == mk, lane_f, 128.0), axis=-1, keepdims=True)
        sel = lane_f == ik
        v = jnp.where(sel, -jnp.inf, v)
        vals.append(mk)
        sels.append(sel)
        idxs.append(ik)
    es = [jnp.exp(mk - vals[0]) for mk in vals]
    den = es[0] + es[1] + es[2] + es[3]
    onehot = jnp.zeros(v.shape, F32)
    for sel in sels:
        onehot = jnp.where(sel, 1.0, onehot)
    prefix = jnp.dot(tri_ref[...], onehot.astype(BF16), preferred_element_type=F32)
    rank_all = prefix + run_ref[0:1, :]
    run_ref[0:1, :] = run_ref[0:1, :] + jnp.sum(onehot, axis=0, keepdims=True)
    ri = jnp.zeros(v.shape, F32)
    rw = jnp.zeros(v.shape, F32)
    for k in range(TOP_K):
        rk = jnp.sum(jnp.where(sels[k], rank_all, 0.0), axis=-1, keepdims=True)
        ri = jnp.where(lane == k, idxs[k], ri)
        ri = jnp.where(lane == TOP_K + k, rk, ri)
        rw = jnp.where(lane == k, es[k] / den, rw)
    ri_ref[...] = ri.astype(jnp.int32)
    rw_ref[...] = rw
    cnt_ref[...] = run_ref[...]


def _router(tok, sc1p, sh, w_hi, w_lo, bias, nctxb):
    nt = tok.shape[0]
    row = lambda i: (i, 0)
    const = lambda i: (0, 0)
    tri = (jnp.arange(TM)[:, None] > jnp.arange(TM)[None, :]).astype(BF16)
    return pl.pallas_call(
        _router_kernel,
        grid=(nt // TM,),
        in_specs=[pl.BlockSpec((TM, D), row),
                  pl.BlockSpec((1, 1, D), _rowtype(nctxb)), pl.BlockSpec((1, 1, D), _rowtype(nctxb)),
                  pl.BlockSpec((D, 128), const), pl.BlockSpec((D, 128), const), pl.BlockSpec((1, 128), const),
                  pl.BlockSpec((TM, TM), const)],
        out_specs=[pl.BlockSpec((TM, D // 2), row), pl.BlockSpec((TM, 128), row), pl.BlockSpec((TM, 128), row),
                   pl.BlockSpec((8, 128), const)],
        out_shape=[jax.ShapeDtypeStruct((nt, D // 2), jnp.int32), jax.ShapeDtypeStruct((nt, 128), jnp.int32),
                   jax.ShapeDtypeStruct((nt, 128), F32), jax.ShapeDtypeStruct((8, 128), F32)],
        scratch_shapes=[pltpu.VMEM((8, 128), F32)],
        compiler_params=_cp(("arbitrary",)),
        name="moe_router",
    )(tok, sc1p, sh, w_hi, w_lo, bias, tri)


def _dispatch_kernel(dest_ref, x_ref, init_ref, o_ref, sem):
    del init_ref
    base = pl.program_id(0) * (TM * TOP_K)

    def row_copy(r, k):
        slot = dest_ref[base + r * TOP_K + k]
        return pltpu.make_async_copy(x_ref.at[pl.ds(r, 1), :], o_ref.at[pl.ds(slot, 1), :], sem)

    def issue(r, carry):
        for k in range(TOP_K):
            row_copy(r, k).start(priority=k % 2)
        return carry

    def drain(r, carry):
        for k in range(TOP_K):
            row_copy(r, k).wait()
        return carry

    lax.fori_loop(0, TM, issue, 0, unroll=4)
    lax.fori_loop(0, TM, drain, 0, unroll=4)


def _dispatch(dest_flat, hw, cap):
    nt, w = hw.shape
    return pl.pallas_call(
        _dispatch_kernel,
        grid_spec=pltpu.PrefetchScalarGridSpec(
            num_scalar_prefetch=1,
            grid=(nt // TM,),
            in_specs=[pl.BlockSpec((TM, w), lambda i, d: (i, 0)), pl.BlockSpec(memory_space=pl.ANY)],
            out_specs=pl.BlockSpec(memory_space=pl.ANY),
            scratch_shapes=[pltpu.SemaphoreType.DMA(())]),
        out_shape=jax.ShapeDtypeStruct((cap, w), hw.dtype),
        input_output_aliases={2: 0},
        compiler_params=_cp(("arbitrary",)),
        name="moe_dispatch",
    )(dest_flat, hw, jnp.zeros((cap, w), hw.dtype))


def _expert_kernel(be_ref, nu_ref, x_ref, wg_ref, bg_ref, wu_ref, bu_ref, wd_ref, bd_ref, o_ref,
                   wgb_ref, wub_ref, wdb_ref):
    i = pl.program_id(0)
    prev = be_ref[jnp.maximum(i - 1, 0)]
    new_expert = jnp.logical_or(i == 0, be_ref[i] != prev)

    @pl.when(new_expert)
    def _():
        wgb_ref[...] = wg_ref[...].astype(BF16)
        wub_ref[...] = wu_ref[...].astype(BF16)
        wdb_ref[...] = wd_ref[...].astype(BF16)

    @pl.when(i < nu_ref[0])
    def _():
        xw = x_ref[...]
        x = jnp.concatenate([lax.bitcast_convert_type(lax.shift_left(xw, 16), F32),
                             lax.bitcast_convert_type(xw & HI16, F32)], axis=1).astype(BF16)
        g = jnp.minimum(jnp.dot(x, wgb_ref[...], preferred_element_type=F32) + bg_ref[...], SWIGLU_LIMIT)
        u = jnp.clip(jnp.dot(x, wub_ref[...], preferred_element_type=F32) + bu_ref[...], -SWIGLU_LIMIT, SWIGLU_LIMIT)
        act = g * jax.nn.sigmoid(SWIGLU_ALPHA * g) * (u + 1.0)
        y = jnp.dot(act.astype(BF16), wdb_ref[...], preferred_element_type=F32) + bd_ref[...]
        o_ref[...] = y.astype(BF16)

    @pl.when(i >= nu_ref[0])
    def _():
        o_ref[...] = jnp.zeros(o_ref.shape, BF16)


def _experts(layer, blk_e, n_used, xd, w_gate, b_gate, w_up, b_up, w_down, b_down):
    cap = xd.shape[0]
    wspec = pl.BlockSpec((None, None, D, D), lambda i, be, nu: (layer, be[i], 0, 0))
    bspec = pl.BlockSpec((None, None, 1, D), lambda i, be, nu: (layer, be[i], 0, 0))
    row = pl.BlockSpec((TME, D), lambda i, be, nu: (i, 0))
    xrow = pl.BlockSpec((TME, D // 2), lambda i, be, nu: (i, 0))
    b4 = lambda b: b.reshape(DEPTH, N_EXP, 1, D)
    return pl.pallas_call(
        _expert_kernel,
        grid_spec=pltpu.PrefetchScalarGridSpec(
            num_scalar_prefetch=2,
            grid=(cap // TME,),
            in_specs=[xrow, wspec, bspec, wspec, bspec, wspec, bspec],
            out_specs=row,
            scratch_shapes=[pltpu.VMEM((D, D), BF16)] * 3),
        out_shape=jax.ShapeDtypeStruct((cap, D), BF16),
        compiler_params=_cp(("arbitrary",), 56),
        name="moe_experts",
    )(blk_e, n_used, xd, w_gate, b4(b_gate), w_up, b4(b_up), w_down, b4(b_down))


def _moe_finish_kernel(yg_ref, w_ref, x_ref, gate_ref, lg_ref, lb_ref, o_ref):
    w = w_ref[...]
    y = yg_ref[0].astype(F32) * w[:, 0:1]
    for kk in range(1, TOP_K):
        y = y + yg_ref[kk].astype(F32) * w[:, kk:kk + 1]
    z = ALPHA * x_ref[...] + gate_ref[0] * y
    o_ref[...] = _layer_norm_rows(z, lg_ref[...], lb_ref[...])


def _moe_finish(yg, top_w, tok, gate, ln_g, ln_b, nctxb):
    nt = tok.shape[0]
    row = lambda i: (i, 0)
    const = lambda i: (0, 0)
    return pl.pallas_call(
        _moe_finish_kernel,
        grid=(nt // TM,),
        in_specs=[pl.BlockSpec((TOP_K, TM, D), lambda i: (0, i, 0)), pl.BlockSpec((TM, TOP_K), row),
                  pl.BlockSpec((TM, D), row),
                  pl.BlockSpec((1, 1, D), _rowtype(nctxb)), pl.BlockSpec((1, D), const), pl.BlockSpec((1, D), const)],
        out_specs=pl.BlockSpec((TM, D), row),
        out_shape=jax.ShapeDtypeStruct((nt, D), F32),
        compiler_params=_cp(("arbitrary",)),
        name="moe_finish",
    )(yg, top_w, tok, gate, ln_g, ln_b)


def _moe_layer(layer, tok, sc1p, sh, gate, ln_g, ln_b, w_router, b_router, w_gate, b_gate, w_up, b_up,
               w_down, b_down, nctxb):
    nt = tok.shape[0]
    wr = jnp.zeros((D, 128), F32).at[:, :N_EXP].set(w_router.astype(F32))
    wr_hi = wr.astype(BF16)
    wr_lo = (wr - wr_hi.astype(F32)).astype(BF16)
    br = jnp.full((1, 128), -1e30, F32).at[0, :N_EXP].set(b_router.astype(F32))
    hw, ri, rw, cnt = _router(tok, sc1p, sh, wr_hi, wr_lo, br, nctxb)
    top_idx, rank, top_w = ri[:, :TOP_K], ri[:, TOP_K:2 * TOP_K], rw[:, :TOP_K]
    counts = cnt[0, :N_EXP].astype(jnp.int32)
    padded = (counts + TME - 1) // TME * TME
    pad_end = jnp.cumsum(padded)
    pad_start = pad_end - padded
    dest = (pad_start[top_idx] + rank).astype(jnp.int32)
    cap = nt * TOP_K + N_EXP * TME
    nb = cap // TME
    blk_start = jnp.arange(nb, dtype=jnp.int32) * TME
    blk_e = jnp.minimum(jnp.sum((pad_end[None, :] <= blk_start[:, None]).astype(jnp.int32), axis=1), N_EXP - 1)
    n_used = (pad_end[-1] // TME).astype(jnp.int32).reshape(1)
    xd = _dispatch(dest.reshape(-1), hw, cap)
    yd = _experts(layer, blk_e, n_used, xd, w_gate, b_gate, w_up, b_up, w_down, b_down)
    yg = yd.at[dest.T.reshape(-1)].get(mode="promise_in_bounds").reshape(TOP_K, nt, D)
    return _moe_finish(yg, top_w, tok, gate, ln_g, ln_b, nctxb)


def _rope_tables(rows, n_ctx):
    row = jnp.broadcast_to(jnp.arange(rows, dtype=F32)[:, None], (rows, GRID_W)).reshape(-1)
    col = jnp.broadcast_to(jnp.arange(GRID_W, dtype=F32)[None, :], (rows, GRID_W)).reshape(-1)
    theta = ROPE_BASE ** (-jnp.arange(ROPE_F, dtype=F32) / ROPE_F)
    ang = jnp.stack([row[:, None] * theta, col[:, None] * theta], axis=1)
    ang = jnp.stack([ang, ang], axis=2).reshape(rows * GRID_W, HD)
    cos = jnp.concatenate([jnp.ones((n_ctx, HD), F32), jnp.cos(ang)], axis=0)
    sin = jnp.concatenate([jnp.zeros((n_ctx, HD), F32), jnp.sin(ang)], axis=0)
    return jnp.tile(cos, (1, 2)), jnp.tile(sin, (1, 2))


def _rot_cols(w):
    k = w.shape[0]
    wr = w.reshape(k, -1, 2, 2, ROPE_F)
    rot = jnp.stack([-wr[..., 1, :], wr[..., 0, :]], axis=-2)
    return rot.reshape(w.shape)


def kernel(x, c, ctx, c_ctx, ada_w, ada_b, ln_g, ln_b, even_w_in, s5_lam_re, s5_lam_im, s5_log_dt, s5_b_re, s5_b_im, s5_c_re, s5_c_im, s5_d, s5_w_glu, s5_b_glu, fnet_w, fnet_b, even_w_out, odd_w_qkv, odd_w_o, da_lq1, da_lk1, da_lq2, da_lk2, da_subln_g, router_w, router_b, moe_w_gate, moe_b_gate, moe_w_up, moe_b_up, moe_w_down, moe_b_down):
    seq = x.shape[1]
    n_ctx = ctx.shape[1]
    assert x.shape[0] == 1 and n_ctx % TM == 0 and seq % TM == 0 and n_ctx == TM
    nctxb = n_ctx // TM
    nt = n_ctx + seq
    nc = nt // CH
    ncc = n_ctx // CH
    tq_lat = 1024 if seq % 1024 == 0 else TM
    ts_lat = 512 if seq % 1024 == 0 else TM
    assert seq % (2 * ts_lat) == 0

    tok = jnp.concatenate([ctx[0], x[0]], axis=0).astype(F32)
    cond8 = jnp.zeros((8, D), F32).at[0].set(c_ctx.astype(F32)).at[1].set(c[0].astype(F32))
    mods = _ada_mods(cond8, ada_w, ada_b)[:, :2].reshape(DEPTH, 2, 6, 1, D)
    cos, sin = _rope_tables(seq // GRID_W, n_ctx)
    c128, s128 = _cos_sin(FN_C, FN_C, FN_C)

    for l in range(DEPTH):
        i = l // 2
        m = mods[l]
        shift_a, scale_a, gate_a, shift_b, scale_b, gate_b = (m[:, j] for j in range(6))
        lg0, lb0 = ln_g[l, 0].reshape(1, D), ln_b[l, 0].reshape(1, D)
        lg1, lb1 = ln_g[l, 1].reshape(1, D), ln_b[l, 1].reshape(1, D)
        if l % 2 == 0:
            wf = jnp.concatenate([jnp.einsum('ab,gbd->gad', c128, fnet_w[i].astype(F32), precision=HIGHEST),
                                  jnp.einsum('ab,gbd->gad', s128, fnet_w[i].astype(F32), precision=HIGHEST)],
                                 axis=-1).astype(BF16)
            u, p, q = _even_in(tok, 1.0 + scale_a, shift_a, even_w_in[i].astype(BF16), wf, nctxb)
            m1, w2, avec = _s5_matrices(s5_lam_re[i], s5_lam_im[i], s5_log_dt[i], s5_b_re[i], s5_b_im[i],
                                        s5_c_re[i], s5_c_im[i], s5_d[i])
            yi, sfr, sfi, sbr, sbi = _s5_in(u, m1)
            hfr, hfi, hbr, hbi = _s5_scan(sfr, sfi, sbr, sbi, avec, ncc)
            ys = _s5_out(yi, hfr, hfi, hbr, hbi, w2)
            g = _fnet_positions(p, q, fnet_b[i], n_ctx)
            tok = _even_out(ys, g, tok, s5_w_glu[i].astype(BF16), s5_b_glu[i].reshape(1, S5_W).astype(F32),
                            even_w_out[i].astype(BF16), gate_a, lg0, lb0, nctxb)
        else:
            lam_init = 0.8 - 0.6 * math.exp(-0.3 * l)
            lam = (jnp.exp(jnp.sum(da_lq1[i].astype(F32) * da_lk1[i].astype(F32)))
                   - jnp.exp(jnp.sum(da_lq2[i].astype(F32) * da_lk2[i].astype(F32))) + lam_init)
            wq, wk, wv = odd_w_qkv[i][:, :D], odd_w_qkv[i][:, D:2 * D], odd_w_qkv[i][:, 2 * D:]
            w5 = jnp.concatenate([wq, _rot_cols(wq), wk, _rot_cols(wk), wv], axis=1).astype(BF16)
            qb, kb, vb = _qkv(tok, 1.0 + scale_a, shift_a, w5, cos, sin, nctxb)
            lamv = jnp.full((1, VD), lam, F32)
            gsv = (da_subln_g[i].astype(F32) * (1.0 - lam_init)).reshape(1, VD)
            on_lat = _attention(qb[n_ctx:], kb, vb, lamv, gsv, tq_lat, n_ctx, ts_lat)
            on_ctx = _attention(qb[:n_ctx], kb[:n_ctx], vb[:n_ctx], lamv, gsv, n_ctx, n_ctx, 0)
            on = jnp.concatenate([on_ctx, on_lat], axis=0)
            tok = _proj_ln(on, odd_w_o[i].astype(BF16), tok, gate_a, lg0, lb0, nctxb)
        tok = _moe_layer(l, tok, 1.0 + scale_b, shift_b, gate_b, lg1, lb1, router_w[l], router_b[l],
                         moe_w_gate, moe_b_gate, moe_w_up, moe_b_up, moe_w_down, moe_b_down, nctxb)
    return tok[n_ctx:].reshape(1, seq, D).astype(x.dtype)
```

```python
import functools
import math

import jax
import jax.numpy as jnp
from jax import lax
from jax.experimental import pallas as pl
from jax.experimental.pallas import tpu as pltpu

F32 = jnp.float32
BF16 = jnp.bfloat16
HIGHEST = lax.Precision.HIGHEST

D = 1024
DEPTH = 4
GRID_W = 64
S5_W = 512
S5_G = 32
S5_C = 16
S5_N = 64
FN_W = 512
FN_G = 4
FN_C = 128
HEADS = 8
HD = 64
VD = 128
ROPE_BASE = 10000.0
ROPE_F = 16
N_EXP = 32
TOP_K = 4
SWIGLU_LIMIT = 7.0
SWIGLU_ALPHA = 1.702
ALPHA = (2.0 * DEPTH) ** 0.25
LN_EPS = 1e-5
LOG2E = 1.4426950408889634
HI16 = -65536

TM = 256
TME = 320
CH = 16
MIB = 1024 * 1024


def _cp(sem, vmem_mib=48):
    return pltpu.CompilerParams(dimension_semantics=sem, vmem_limit_bytes=vmem_mib * MIB)


def _rowtype(nctxb):
    return lambda i: (jnp.where(i >= nctxb, 1, 0), 0, 0)


def _layer_norm_rows(z, g, b):
    mu = jnp.mean(z, axis=-1, keepdims=True)
    zc = z - mu
    var = jnp.mean(zc * zc, axis=-1, keepdims=True)
    return zc * lax.rsqrt(var + LN_EPS) * g + b


def _ada_kernel(c_ref, w_ref, b_ref, o_ref):
    c = c_ref[...]
    a = c * jax.nn.sigmoid(c)
    o_ref[0] = jnp.dot(a, w_ref[0], preferred_element_type=F32, precision=HIGHEST) + b_ref[0]


def _ada_mods(cond8, ada_w, ada_b):
    tn = 1536
    return pl.pallas_call(
        _ada_kernel,
        grid=(DEPTH, 6 * D // tn),
        in_specs=[pl.BlockSpec((8, D), lambda l, j: (0, 0)),
                  pl.BlockSpec((1, D, tn), lambda l, j: (l, 0, j)),
                  pl.BlockSpec((1, 1, tn), lambda l, j: (l, 0, j))],
        out_specs=pl.BlockSpec((1, 8, tn), lambda l, j: (l, 0, j)),
        out_shape=jax.ShapeDtypeStruct((DEPTH, 8, 6 * D), F32),
        compiler_params=_cp(("arbitrary", "arbitrary")),
        name="ada_mods",
    )(cond8, ada_w, ada_b.reshape(DEPTH, 1, 6 * D))


def _even_in_kernel(x_ref, sc_ref, sh_ref, w_ref, wf_ref, u_ref, p_ref, q_ref):
    h = x_ref[...] * sc_ref[0] + sh_ref[0]
    z = jnp.dot(h.astype(BF16), w_ref[...], preferred_element_type=F32)
    u_ref[...] = z[:, :S5_W]
    for g in range(FN_G):
        f = z[:, S5_W + FN_C * g:S5_W + FN_C * (g + 1)]
        fc = f - jnp.mean(f, axis=-1, keepdims=True)
        fn = fc * lax.rsqrt(jnp.mean(fc * fc, axis=-1, keepdims=True) + LN_EPS)
        pq = jnp.dot(fn.astype(BF16), wf_ref[g], preferred_element_type=F32)
        p_ref[:, FN_C * g:FN_C * (g + 1)] = pq[:, :FN_C]
        q_ref[:, FN_C * g:FN_C * (g + 1)] = pq[:, FN_C:]


def _even_in(tok, sc1p, sh, w_in, wf, nctxb):
    nt = tok.shape[0]
    row = lambda i: (i, 0)
    return pl.pallas_call(
        _even_in_kernel,
        grid=(nt // TM,),
        in_specs=[pl.BlockSpec((TM, D), row),
                  pl.BlockSpec((1, 1, D), _rowtype(nctxb)),
                  pl.BlockSpec((1, 1, D), _rowtype(nctxb)),
                  pl.BlockSpec((D, D), lambda i: (0, 0)),
                  pl.BlockSpec((FN_G, FN_C, 2 * FN_C), lambda i: (0, 0, 0))],
        out_specs=[pl.BlockSpec((TM, S5_W), row),
                   pl.BlockSpec((TM, FN_W), row),
                   pl.BlockSpec((TM, FN_W), row)],
        out_shape=[jax.ShapeDtypeStruct((nt, S5_W), F32),
                   jax.ShapeDtypeStruct((nt, FN_W), F32),
                   jax.ShapeDtypeStruct((nt, FN_W), F32)],
        compiler_params=_cp(("arbitrary",)),
        name="even_in",
    )(tok, sc1p, sh, w_in, wf)


def _lane_iota(shape):
    return lax.broadcasted_iota(jnp.int32, shape, 1)


def _s5_in_kernel(x_ref, m1_ref, yi_ref, sfr_ref, sfi_ref, sbr_ref, sbi_ref):
    nc = x_ref.shape[0] // CH
    lane = _lane_iota((nc, 128)) // 32
    for pp in range(4):
        @pl.when(pl.program_id(1) == pp)
        def _(pp=pp):
            tiles = []
            for q in range(4):
                acc = None
                for r in range(4):
                    xs = x_ref[pl.ds(4 * q + r, nc, stride=CH), :]
                    shift = ((r - pp) * 32) % 128
                    if shift:
                        xs = pltpu.roll(xs, shift, 1)
                    acc = xs if acc is None else jnp.where(lane == r, xs, acc)
                tiles.append(acc.astype(BF16))
            u = jnp.concatenate(tiles, axis=1)
            r_all = jnp.dot(u, m1_ref[0], preferred_element_type=F32)
            yi_ref[0] = r_all[:, :512]
            sfr_ref[...] = r_all[:, 512:640]
            sfi_ref[...] = r_all[:, 640:768]
            sbr_ref[...] = r_all[:, 768:896]
            sbi_ref[...] = r_all[:, 896:1024]


def _s5_in(u, m1):
    nt = u.shape[0]
    nc = nt // CH
    sspec = pl.BlockSpec((nc, 128), lambda i, j: (0, 4 * i + j))
    sshape = jax.ShapeDtypeStruct((nc, 2048), F32)
    return pl.pallas_call(
        _s5_in_kernel,
        grid=(4, 4),
        in_specs=[pl.BlockSpec((nt, 128), lambda i, j: (0, i)),
                  pl.BlockSpec((1, 512, 1024), lambda i, j: (4 * i + j, 0, 0))],
        out_specs=[pl.BlockSpec((1, nc, 512), lambda i, j: (4 * i + j, 0, 0)), sspec, sspec, sspec, sspec],
        out_shape=[jax.ShapeDtypeStruct((16, nc, 512), F32), sshape, sshape, sshape, sshape],
        compiler_params=_cp(("arbitrary", "arbitrary")),
        name="s5_in",
    )(u, m1)


def _s5_scan_kernel(ncc, sfr_ref, sfi_ref, sbr_ref, sbi_ref, a_ref,
                    hfr_ref, hfi_ref, hbr_ref, hbi_ref):
    nc = sfr_ref.shape[0]
    afr, afi, abr, abi = a_ref[0:1, :], a_ref[1:2, :], a_ref[2:3, :], a_ref[3:4, :]

    def body(i, carry):
        fr, fi, br, bi = carry
        jb = jnp.where(i < ncc, ncc - 1 - i, nc - 1 - i + ncc)
        hfr_ref[pl.ds(i, 1), :] = fr
        hfi_ref[pl.ds(i, 1), :] = fi
        hbr_ref[pl.ds(jb, 1), :] = br
        hbi_ref[pl.ds(jb, 1), :] = bi
        sr, si = sfr_ref[pl.ds(i, 1), :], sfi_ref[pl.ds(i, 1), :]
        tr, ti = sbr_ref[pl.ds(jb, 1), :], sbi_ref[pl.ds(jb, 1), :]
        return (afr * fr - afi * fi + sr, afr * fi + afi * fr + si,
                abr * br - abi * bi + tr, abr * bi + abi * br + ti)

    z = jnp.zeros((1, sfr_ref.shape[1]), F32)
    lax.fori_loop(0, nc, body, (z, z, z, z))


def _s5_scan(sfr, sfi, sbr, sbi, avec, ncc):
    nc = sfr.shape[0]
    spec = pl.BlockSpec((nc, 512), lambda i: (0, i))
    shape = jax.ShapeDtypeStruct((nc, 2048), F32)
    return pl.pallas_call(
        functools.partial(_s5_scan_kernel, ncc),
        grid=(4,),
        in_specs=[spec, spec, spec, spec, pl.BlockSpec((8, 512), lambda i: (0, i))],
        out_specs=[spec, spec, spec, spec],
        out_shape=[shape, shape, shape, shape],
        compiler_params=_cp(("arbitrary",)),
        name="s5_scan",
    )(sfr, sfi, sbr, sbi, avec)


def _s5_out_kernel(yi_ref, hfr_ref, hfi_ref, hbr_ref, hbi_ref, w2_ref, o_ref, yp_ref):
    nc = yi_ref.shape[1]
    pp_dyn = pl.program_id(1)
    hcat = jnp.concatenate([hfr_ref[...], hfi_ref[...], hbr_ref[...], hbi_ref[...]], axis=1).astype(BF16)
    yp_ref[pp_dyn] = yi_ref[0] + jnp.dot(hcat, w2_ref[0], preferred_element_type=F32)

    @pl.when(pp_dyn == 3)
    def _():
        lane = _lane_iota((nc, 128)) // 32
        for s in range(CH):
            acc = None
            for pp in range(4):
                ys = yp_ref[pp, :, (s // 4) * 128:(s // 4 + 1) * 128]
                shift = ((pp - s % 4) * 32) % 128
                if shift:
                    ys = pltpu.roll(ys, shift, 1)
                acc = ys if acc is None else jnp.where(lane == pp, ys, acc)
            o_ref[pl.ds(s, nc, stride=CH), :] = acc


def _s5_out(yi, hfr, hfi, hbr, hbi, w2):
    nc = yi.shape[1]
    hspec = pl.BlockSpec((nc, 128), lambda i, j: (0, 4 * i + j))
    return pl.pallas_call(
        _s5_out_kernel,
        grid=(4, 4),
        in_specs=[pl.BlockSpec((1, nc, 512), lambda i, j: (4 * i + j, 0, 0)), hspec, hspec, hspec, hspec,
                  pl.BlockSpec((1, 512, 512), lambda i, j: (4 * i + j, 0, 0))],
        out_specs=pl.BlockSpec((nc * CH, 128), lambda i, j: (0, i)),
        out_shape=jax.ShapeDtypeStruct((nc * CH, S5_W), F32),
        scratch_shapes=[pltpu.VMEM((4, nc, 512), F32)],
        compiler_params=_cp(("arbitrary", "arbitrary")),
        name="s5_out",
    )(yi, hfr, hfi, hbr, hbi, w2)


def _toeplitz_kernel(k_ref, o_ref):
    lane = _lane_iota((32, 128)) // 32
    for s in range(CH):
        for q in range(CH // 4):
            acc = None
            for j in range(4):
                piece = k_ref[0, 4 * q + j - s + CH - 1]
                acc = piece if acc is None else jnp.where(lane == j, piece, acc)
            o_ref[0, s * 32:(s + 1) * 32, q * 128:(q + 1) * 128] = acc.astype(BF16)


def _toeplitz(kblk_rep):
    npair, nlag = kblk_rep.shape[:2]
    return pl.pallas_call(
        _toeplitz_kernel,
        grid=(npair,),
        in_specs=[pl.BlockSpec((1, nlag, 32, 128), lambda p: (p, 0, 0, 0))],
        out_specs=pl.BlockSpec((1, CH * 32, CH * 32), lambda p: (p, 0, 0)),
        out_shape=jax.ShapeDtypeStruct((npair, CH * 32, CH * 32), BF16),
        compiler_params=_cp(("arbitrary",)),
        name="s5_toeplitz",
    )(kblk_rep)


def _s5_matrices(lam_re, lam_im, log_dt, b_re, b_im, c_re, c_im, d_skip):
    lam = lax.complex(lam_re.astype(F32), lam_im.astype(F32))
    dt = jnp.exp(log_dt.astype(F32))[..., None]
    ldt = lam * dt
    lam_bar = jnp.exp(ldt)
    bb = ((lam_bar - 1.0) / lam)[..., None] * lax.complex(b_re.astype(F32), b_im.astype(F32))
    cc = lax.complex(c_re.astype(F32), c_im.astype(F32))
    ks = jnp.arange(CH + 1, dtype=F32)
    pw = jnp.exp(ldt[..., None] * ks)
    kk = jnp.real(jnp.einsum('dgcn,dgnk,dgne->dkgce', cc, pw[..., :CH], bb, precision=HIGHEST))
    k0 = kk[0, 0] + kk[1, 0] + jnp.eye(S5_C, dtype=F32)[None] * d_skip.astype(F32)[:, :, None]
    kall = jnp.concatenate([kk[1, 1:][::-1], k0[None], kk[0, 1:]], axis=0)
    eye2 = jnp.eye(2, dtype=F32)
    kt = kall.transpose(1, 0, 3, 2).reshape(16, 2, 2 * CH - 1, S5_C, S5_C)
    kblk = jnp.einsum('pglcd,gh->plgchd', kt, eye2).reshape(16, 2 * CH - 1, 32, 32)
    tp = _toeplitz(jnp.tile(kblk, (1, 1, 1, 4)))
    w1f = jnp.einsum('gns,gnc->gscn', pw[0][..., :CH][..., ::-1], bb[0])
    w1b = jnp.einsum('gns,gnc->gscn', pw[1][..., :CH], bb[1])

    def pair_cols(w):
        w = w.reshape(16, 2, CH, S5_C, S5_N)
        return jnp.einsum('pgscn,gh->psgchn', w, eye2).reshape(16, 512, 128)

    m1 = jnp.concatenate([tp, pair_cols(jnp.real(w1f)).astype(BF16), pair_cols(jnp.imag(w1f)).astype(BF16),
                          pair_cols(jnp.real(w1b)).astype(BF16), pair_cols(jnp.imag(w1b)).astype(BF16)], axis=-1)
    cpf = jnp.einsum('gcn,gnt->gntc', cc[0], pw[0][..., 1:])
    cpb = jnp.einsum('gcn,gnt->gntc', cc[1], pw[1][..., 1:][..., ::-1])

    def pair_rows(w):
        w = w.reshape(16, 2, S5_N, CH, S5_C)
        return jnp.einsum('pgntc,gh->pgnthc', w, eye2).reshape(16, 128, 512)

    w2 = jnp.concatenate([pair_rows(jnp.real(cpf)), pair_rows(-jnp.imag(cpf)),
                          pair_rows(jnp.real(cpb)), pair_rows(-jnp.imag(cpb))], axis=1)
    a_f, a_b = pw[0][..., CH].reshape(-1), pw[1][..., CH].reshape(-1)
    z = jnp.zeros_like(jnp.real(a_f))
    avec = jnp.stack([jnp.real(a_f), jnp.imag(a_f), jnp.real(a_b), jnp.imag(a_b), z, z, z, z])
    return m1.astype(BF16), w2.astype(BF16), avec


def _dft_lat_kernel(n1, n2, scale, p_ref, q_ref, m_ref, c2_ref, s2_ref, rot_ref, b_ref, o_ref, zr_ref, zi_ref):
    length = n1 * n2
    rr, ri = rot_ref[0], rot_ref[1]

    def phase1(blk, tw):
        b0 = pl.multiple_of(blk * 8, 8)
        pv = p_ref.at[pl.ds(b0, length - n2 + 8), :]
        qv = q_ref.at[pl.ds(b0, length - n2 + 8), :]
        zrv = zr_ref.at[pl.ds(b0, length - n2 + 8), :]
        ziv = zi_ref.at[pl.ds(b0, length - n2 + 8), :]
        twr, twi = tw
        for j in range(8):
            rows = pl.ds(j, n1, stride=n2)
            x = jnp.concatenate([pv[rows, :], qv[rows, :]], axis=0).astype(BF16)
            z = jnp.dot(m_ref[...], x, preferred_element_type=F32)
            zr, zi = z[:n1], z[n1:]
            zrv[rows, :] = zr * twr - zi * twi
            ziv[rows, :] = zr * twi + zi * twr
            twr, twi = twr * rr - twi * ri, twr * ri + twi * rr
        return twr, twi

    lax.fori_loop(0, n2 // 8, phase1, (jnp.ones((n1, 128), F32), jnp.zeros((n1, 128), F32)))

    def phase2(blk, carry):
        k0 = pl.multiple_of(blk * 8, 8)
        ov = o_ref.at[pl.ds(k0, length - n1 + 8), :]
        for j in range(8):
            r0 = pl.multiple_of((k0 + j) * n2, 8)
            zr = zr_ref[pl.ds(r0, n2), :].astype(BF16)
            zi = zi_ref[pl.ds(r0, n2), :].astype(BF16)
            g = (jnp.dot(c2_ref[...], zr, preferred_element_type=F32)
                 + jnp.dot(s2_ref[...], zi, preferred_element_type=F32))
            ov[pl.ds(j, n2, stride=n1), :] = g * scale + b_ref[...]
        return carry

    lax.fori_loop(0, n1 // 8, phase2, 0)


def _dft_lat(p, q, m, c2, s2, rot, bias, n1, n2, scale):
    length, w = p.shape
    col = pl.BlockSpec((length, 128), lambda j: (0, j), pipeline_mode=pl.Buffered(1))
    const = lambda shape: pl.BlockSpec(shape, lambda j: (0,) * len(shape))
    return pl.pallas_call(
        functools.partial(_dft_lat_kernel, n1, n2, scale),
        grid=(w // 128,),
        in_specs=[col, col, const((2 * n1, 2 * n1)), const((n2, n2)), const((n2, n2)), const((2, n1, 128)),
                  pl.BlockSpec((1, 128), lambda j: (0, j))],
        out_specs=pl.BlockSpec((length, 128), lambda j: (0, j)),
        out_shape=jax.ShapeDtypeStruct((length, w), F32),
        scratch_shapes=[pltpu.VMEM((length, 128), F32), pltpu.VMEM((length, 128), F32)],
        compiler_params=_cp(("arbitrary",), 56),
        name="fnet_dft_lat",
    )(p, q, m, c2, s2, rot, bias)


def _dft_small_kernel(scale, c_ref, s_ref, p_ref, q_ref, b_ref, o_ref):
    g = (jnp.dot(c_ref[...], p_ref[...].astype(BF16), preferred_element_type=F32)
         - jnp.dot(s_ref[...], q_ref[...].astype(BF16), preferred_element_type=F32))
    o_ref[...] = g * scale + b_ref[...]


def _dft_small(cm, sm, p, q, bias, scale):
    n, w = p.shape
    full = lambda shape: pl.BlockSpec(shape, lambda i: (0,) * len(shape))
    return pl.pallas_call(
        functools.partial(_dft_small_kernel, scale),
        grid=(1,),
        in_specs=[full((n, n)), full((n, n)), full((n, w)), full((n, w)), full((1, w))],
        out_specs=full((n, w)),
        out_shape=jax.ShapeDtypeStruct((n, w), F32),
        compiler_params=_cp(("arbitrary",)),
        name="fnet_dft_ctx",
    )(cm, sm, p, q, bias)


def _cos_sin(n_rows, n_cols, period):
    ang = (2.0 * math.pi / period) * ((jnp.arange(n_rows)[:, None] * jnp.arange(n_cols)[None, :]) % period).astype(F32)
    return jnp.cos(ang), jnp.sin(ang)


def _fnet_positions(p, q, bias, n_ctx):
    pc, qc, pl_, ql_ = p[:n_ctx], q[:n_ctx], p[n_ctx:], q[n_ctx:]
    length = pl_.shape[0]
    n1 = 1 << ((length.bit_length() - 1) // 2)
    n2 = length // n1
    assert n1 * n2 == length and n1 % 8 == 0
    bias = bias.reshape(1, FN_W).astype(F32)
    cc, sc = _cos_sin(n_ctx, n_ctx, n_ctx)
    g_ctx = _dft_small(cc.astype(BF16), sc.astype(BF16), pc, qc, bias, 1.0 / math.sqrt(FN_C * n_ctx))
    c1, s1 = _cos_sin(n1, n1, n1)
    m = jnp.concatenate([jnp.concatenate([c1, -s1], axis=1),
                         jnp.concatenate([-s1, -c1], axis=1)], axis=0).astype(BF16)
    ang = (2.0 * math.pi / length) * jnp.arange(n1, dtype=F32)
    rot = jnp.stack([jnp.broadcast_to(jnp.cos(ang)[:, None], (n1, 128)),
                     jnp.broadcast_to(-jnp.sin(ang)[:, None], (n1, 128))])
    c2, s2 = _cos_sin(n2, n2, n2)
    g_lat = _dft_lat(pl_, ql_, m, c2.astype(BF16), s2.astype(BF16), rot, bias, n1, n2,
                     1.0 / math.sqrt(FN_C * length))
    return jnp.concatenate([g_ctx, g_lat], axis=0)


def _even_out_kernel(ys_ref, g_ref, x_ref, wglu_ref, bglu_ref, wout_ref, gate_ref, lg_ref, lb_ref, o_ref):
    ys = ys_ref[...]
    c0 = math.sqrt(2.0 / math.pi)
    y = 0.5 * ys * (1.0 + jnp.tanh(c0 * (ys + 0.044715 * (ys * ys * ys))))
    gl = jnp.dot(y.astype(BF16), wglu_ref[...], preferred_element_type=F32) + bglu_ref[...]
    y2 = y * jax.nn.sigmoid(gl)
    m = (jnp.dot(y2.astype(BF16), wout_ref[:S5_W, :], preferred_element_type=F32)
         + jnp.dot(g_ref[...].astype(BF16), wout_ref[S5_W:, :], preferred_element_type=F32))
    z = ALPHA * x_ref[...] + gate_ref[0] * m
    o_ref[...] = _layer_norm_rows(z, lg_ref[...], lb_ref[...])


def _even_out(ys, g, tok, w_glu, b_glu, w_out, gate, ln_g, ln_b, nctxb):
    nt = tok.shape[0]
    row = lambda i: (i, 0)
    const = lambda i: (0, 0)
    return pl.pallas_call(
        _even_out_kernel,
        grid=(nt // TM,),
        in_specs=[pl.BlockSpec((TM, S5_W), row), pl.BlockSpec((TM, FN_W), row), pl.BlockSpec((TM, D), row),
                  pl.BlockSpec((S5_W, S5_W), const), pl.BlockSpec((1, S5_W), const),
                  pl.BlockSpec((D, D), const), pl.BlockSpec((1, 1, D), _rowtype(nctxb)),
                  pl.BlockSpec((1, D), const), pl.BlockSpec((1, D), const)],
        out_specs=pl.BlockSpec((TM, D), row),
        out_shape=jax.ShapeDtypeStruct((nt, D), F32),
        compiler_params=_cp(("arbitrary",)),
        name="even_out",
    )(ys, g, tok, w_glu, b_glu, w_out, gate, ln_g, ln_b)


def _qkv_kernel(x_ref, sc_ref, sh_ref, w_ref, cos_ref, sin_ref, q_ref, k_ref, v_ref):
    h = (x_ref[...] * sc_ref[0] + sh_ref[0]).astype(BF16)
    cos, sin = cos_ref[...], sin_ref[...]
    z = jnp.dot(h, w_ref[...], preferred_element_type=F32)
    for hh in range(HEADS):
        sl = slice(hh * 128, (hh + 1) * 128)
        q = z[:, sl] * cos + z[:, D + hh * 128:D + (hh + 1) * 128] * sin
        k = z[:, 2 * D + hh * 128:2 * D + (hh + 1) * 128] * cos + z[:, 3 * D + hh * 128:3 * D + (hh + 1) * 128] * sin
        q_ref[:, sl] = (q * (HD ** -0.5 * LOG2E)).astype(BF16)
        k_ref[:, sl] = k.astype(BF16)
        v_ref[:, 2 * hh * VD:(2 * hh + 1) * VD] = z[:, 4 * D + hh * VD:4 * D + (hh + 1) * VD].astype(BF16)
        v_ref[:, (2 * hh + 1) * VD:(2 * hh + 2) * VD] = jnp.ones((z.shape[0], VD), BF16)


def _qkv(tok, sc1p, sh, w5, cos, sin, nctxb):
    nt = tok.shape[0]
    row = lambda i: (i, 0)
    o = jax.ShapeDtypeStruct((nt, D), BF16)
    ov = jax.ShapeDtypeStruct((nt, 2 * D), BF16)
    return pl.pallas_call(
        _qkv_kernel,
        grid=(nt // TM,),
        in_specs=[pl.BlockSpec((TM, D), row),
                  pl.BlockSpec((1, 1, D), _rowtype(nctxb)),
                  pl.BlockSpec((1, 1, D), _rowtype(nctxb)),
                  pl.BlockSpec((D, 5 * D), lambda i: (0, 0)),
                  pl.BlockSpec((TM, 128), row), pl.BlockSpec((TM, 128), row)],
        out_specs=[pl.BlockSpec((TM, D), row), pl.BlockSpec((TM, D), row), pl.BlockSpec((TM, 2 * D), row)],
        out_shape=[o, o, ov],
        compiler_params=_cp(("arbitrary",)),
        name="qkv_rope",
    )(tok, sc1p, sh, w5, cos, sin)


def _attn_kernel(n_head, ts, n_pairs, q_ref, k_ref, v_ref, lam_ref, gs_ref, o_ref,
                 qq_ref, sa_ref, sb_ref, m_ref, acc_ref):
    tq = q_ref.shape[0]
    q = q_ref[...]
    lane = lax.broadcasted_iota(jnp.int32, q.shape, 1)
    zero = jnp.zeros_like(q)
    qq_ref[:tq, :] = jnp.where(lane < HD, q, zero)
    qq_ref[tq:, :] = jnp.where(lane >= HD, q, zero)
    m_ref[...] = jnp.full(m_ref.shape, -1e30, F32)
    acc_ref[...] = jnp.zeros(acc_ref.shape, F32)

    def scores(off, size, dst_ref):
        dst_ref[:, :size] = lax.dot_general(qq_ref[...], k_ref[pl.ds(off, size), :],
                                            (((1,), (1,)), ((), ())), preferred_element_type=F32)

    def consume(off, size, src_ref):
        tiles = [src_ref[:, t * 128:(t + 1) * 128] for t in range(size // 128)]
        mx = functools.reduce(jnp.maximum, tiles)
        m_old = m_ref[...]
        m_new = jnp.maximum(m_old, jnp.max(mx, axis=-1, keepdims=True))
        alpha = jnp.exp2(m_old - m_new)
        p = jnp.concatenate([jnp.exp2(t - m_new).astype(BF16) for t in tiles], axis=1)
        pv = jnp.dot(p, v_ref[pl.ds(off, size), :], preferred_element_type=F32)
        acc_ref[...] = jnp.concatenate([alpha, alpha], axis=1) * acc_ref[...] + pv
        m_ref[...] = m_new

    scores(0, n_head, sa_ref)
    if n_pairs:
        scores(n_head, ts, sb_ref)
    consume(0, n_head, sa_ref)

    def pair(i, last):
        off0 = pl.multiple_of(n_head + (2 * i) * ts, 128)
        off1 = pl.multiple_of(n_head + (2 * i + 1) * ts, 128)
        scores(off1, ts, sa_ref)
        consume(off0, ts, sb_ref)
        if not last:
            scores(pl.multiple_of(n_head + (2 * i + 2) * ts, 128), ts, sb_ref)
        consume(off1, ts, sa_ref)

    if n_pairs:
        def body(i, carry):
            pair(2 * i, False)
            pair(2 * i + 1, False)
            return carry
        lax.fori_loop(0, (n_pairs - 1) // 2, body, 0)
        if (n_pairs - 1) % 2:
            pair(n_pairs - 2, False)
        pair(n_pairs - 1, True)

    acc = acc_ref[...]
    o12 = acc[:, :VD] / acc[:, VD:]
    o = o12[:tq] - lam_ref[...] * o12[tq:]
    o = o * lax.rsqrt(jnp.mean(o * o, axis=-1, keepdims=True) + LN_EPS)
    o_ref[...] = (o * gs_ref[...]).astype(BF16)


def _attention(q, k, vext, lamv, gsv, tq, n_head, ts):
    nq, nk = q.shape[0], k.shape[0]
    n_sub = (nk - n_head) // ts if ts else 0
    assert n_head + n_sub * ts == nk and n_sub % 2 == 0 and nq % tq == 0
    const = lambda h, i: (0, 0)
    return pl.pallas_call(
        functools.partial(_attn_kernel, n_head, ts, n_sub // 2),
        grid=(HEADS, nq // tq),
        in_specs=[pl.BlockSpec((tq, 128), lambda h, i: (i, h)),
                  pl.BlockSpec((nk, 128), lambda h, i: (0, h), pipeline_mode=pl.Buffered(1)),
                  pl.BlockSpec((nk, 2 * VD), lambda h, i: (0, h), pipeline_mode=pl.Buffered(1)),
                  pl.BlockSpec((1, 128), const), pl.BlockSpec((1, 128), const)],
        out_specs=pl.BlockSpec((tq, 128), lambda h, i: (i, h)),
        out_shape=jax.ShapeDtypeStruct((nq, D), BF16),
        scratch_shapes=[pltpu.VMEM((2 * tq, 128), BF16),
                        pltpu.VMEM((2 * tq, max(ts, n_head)), F32), pltpu.VMEM((2 * tq, max(ts, n_head)), F32),
                        pltpu.VMEM((2 * tq, 128), F32), pltpu.VMEM((2 * tq, 2 * VD), F32)],
        compiler_params=_cp(("arbitrary", "arbitrary"), 56),
        name="diff_attention",
    )(q, k, vext, lamv, gsv)


def _proj_ln_kernel(a_ref, w_ref, x_ref, gate_ref, lg_ref, lb_ref, o_ref):
    m = jnp.dot(a_ref[...], w_ref[...], preferred_element_type=F32)
    z = ALPHA * x_ref[...] + gate_ref[0] * m
    o_ref[...] = _layer_norm_rows(z, lg_ref[...], lb_ref[...])


def _proj_ln(a, w, tok, gate, ln_g, ln_b, nctxb):
    nt = tok.shape[0]
    row = lambda i: (i, 0)
    const = lambda i: (0, 0)
    return pl.pallas_call(
        _proj_ln_kernel,
        grid=(nt // TM,),
        in_specs=[pl.BlockSpec((TM, D), row), pl.BlockSpec((D, D), const), pl.BlockSpec((TM, D), row),
                  pl.BlockSpec((1, 1, D), _rowtype(nctxb)), pl.BlockSpec((1, D), const), pl.BlockSpec((1, D), const)],
        out_specs=pl.BlockSpec((TM, D), row),
        out_shape=jax.ShapeDtypeStruct((nt, D), F32),
        compiler_params=_cp(("arbitrary",)),
        name="attn_out_ln",
    )(a, w, tok, gate, ln_g, ln_b)


def _router_kernel(x_ref, sc_ref, sh_ref, wh_ref, wl_ref, b_ref, tri_ref, h_ref, ri_ref, rw_ref, cnt_ref,
                   run_ref):
    @pl.when(pl.program_id(0) == 0)
    def _():
        run_ref[...] = jnp.zeros(run_ref.shape, F32)

    h = x_ref[...] * sc_ref[0] + sh_ref[0]
    hh = h.astype(BF16)
    hl = (h - hh.astype(F32)).astype(BF16)
    bits = lax.bitcast_convert_type(hh.astype(F32), jnp.int32)
    h_ref[...] = (bits[:, D // 2:] & HI16) | lax.shift_right_logical(bits[:, :D // 2], 16)
    v = (jnp.dot(hh, wh_ref[...], preferred_element_type=F32)
         + jnp.dot(hh, wl_ref[...], preferred_element_type=F32)
         + jnp.dot(hl, wh_ref[...], preferred_element_type=F32)) + b_ref[...]
    lane = lax.broadcasted_iota(jnp.int32, v.shape, 1)
    lane_f = lane.astype(F32)
    vals, sels, idxs = [], [], []
    for _ in range(TOP_K):
        mk = jnp.max(v, axis=-1, keepdims=True)
        ik = jnp.min(jnp.where(v == mk, lane_f, 128.0), axis=-1, keepdims=True)
        sel = lane_f == ik
        v = jnp.where(sel, -jnp.inf, v)
        vals.append(mk)
        sels.append(sel)
        idxs.append(ik)
    es = [jnp.exp(mk - vals[0]) for mk in vals]
    den = es[0] + es[1] + es[2] + es[3]
    onehot = jnp.zeros(v.shape, F32)
    for sel in sels:
        onehot = jnp.where(sel, 1.0, onehot)
    prefix = jnp.dot(tri_ref[...], onehot.astype(BF16), preferred_element_type=F32)
    rank_all = prefix + run_ref[0:1, :]
    run_ref[0:1, :] = run_ref[0:1, :] + jnp.sum(onehot, axis=0, keepdims=True)
    ri = jnp.zeros(v.shape, F32)
    rw = jnp.zeros(v.shape, F32)
    for k in range(TOP_K):
        rk = jnp.sum(jnp.where(sels[k], rank_all, 0.0), axis=-1, keepdims=True)
        ri = jnp.where(lane == k, idxs[k], ri)
        ri = jnp.where(lane == TOP_K + k, rk, ri)
        rw = jnp.where(lane == k, es[k] / den, rw)
    ri_ref[...] = ri.T[:8, :].astype(jnp.int32)
    rw_ref[...] = rw
    cnt_ref[...] = run_ref[...]


def _router(tok, sc1p, sh, w_hi, w_lo, bias, nctxb):
    nt = tok.shape[0]
    row = lambda i: (i, 0)
    const = lambda i: (0, 0)
    tri = (jnp.arange(TM)[:, None] > jnp.arange(TM)[None, :]).astype(BF16)
    return pl.pallas_call(
        _router_kernel,
        grid=(nt // TM,),
        in_specs=[pl.BlockSpec((TM, D), row),
                  pl.BlockSpec((1, 1, D), _rowtype(nctxb)), pl.BlockSpec((1, 1, D), _rowtype(nctxb)),
                  pl.BlockSpec((D, 128), const), pl.BlockSpec((D, 128), const), pl.BlockSpec((1, 128), const),
                  pl.BlockSpec((TM, TM), const)],
        out_specs=[pl.BlockSpec((TM, D // 2), row), pl.BlockSpec((8, TM), lambda i: (0, i)),
                   pl.BlockSpec((TM, 128), row),
                   pl.BlockSpec((8, 128), const)],
        out_shape=[jax.ShapeDtypeStruct((nt, D // 2), jnp.int32), jax.ShapeDtypeStruct((8, nt), jnp.int32),
                   jax.ShapeDtypeStruct((nt, 128), F32), jax.ShapeDtypeStruct((8, 128), F32)],
        scratch_shapes=[pltpu.VMEM((8, 128), F32)],
        compiler_params=_cp(("arbitrary",)),
        name="moe_router",
    )(tok, sc1p, sh, w_hi, w_lo, bias, tri)


def _dispatch_kernel(dest_ref, x_ref, init_ref, o_ref, sem):
    del init_ref
    nt = dest_ref.shape[0] // TOP_K
    base = pl.program_id(0) * TM

    def row_copy(r, k):
        slot = dest_ref[k * nt + base + r]
        return pltpu.make_async_copy(x_ref.at[pl.ds(r, 1), :], o_ref.at[pl.ds(slot, 1), :], sem)

    def issue(r, carry):
        for k in range(TOP_K):
            row_copy(r, k).start(priority=k % 2)
        return carry

    def drain(r, carry):
        for k in range(TOP_K):
            row_copy(r, k).wait()
        return carry

    lax.fori_loop(0, TM, issue, 0, unroll=4)
    lax.fori_loop(0, TM, drain, 0, unroll=4)


def _dispatch(dest_flat, hw, cap):
    nt, w = hw.shape
    return pl.pallas_call(
        _dispatch_kernel,
        grid_spec=pltpu.PrefetchScalarGridSpec(
            num_scalar_prefetch=1,
            grid=(nt // TM,),
            in_specs=[pl.BlockSpec((TM, w), lambda i, d: (i, 0)), pl.BlockSpec(memory_space=pl.ANY)],
            out_specs=pl.BlockSpec(memory_space=pl.ANY),
            scratch_shapes=[pltpu.SemaphoreType.DMA(())]),
        out_shape=jax.ShapeDtypeStruct((cap, w), hw.dtype),
        input_output_aliases={2: 0},
        compiler_params=_cp(("arbitrary",)),
        name="moe_dispatch",
    )(dest_flat, hw, jnp.zeros((cap, w), hw.dtype))


def _expert_kernel(be_ref, nu_ref, x_ref, wg_ref, bg_ref, wu_ref, bu_ref, wd_ref, bd_ref, o_ref,
                   wgb_ref, wub_ref, wdb_ref):
    i = pl.program_id(0)
    prev = be_ref[jnp.maximum(i - 1, 0)]
    new_expert = jnp.logical_or(i == 0, be_ref[i] != prev)

    @pl.when(new_expert)
    def _():
        wgb_ref[...] = wg_ref[...].astype(BF16)
        wub_ref[...] = wu_ref[...].astype(BF16)
        wdb_ref[...] = wd_ref[...].astype(BF16)

    @pl.when(i < nu_ref[0])
    def _():
        xw = x_ref[...]
        x = jnp.concatenate([lax.bitcast_convert_type(lax.shift_left(xw, 16), F32),
                             lax.bitcast_convert_type(xw & HI16, F32)], axis=1).astype(BF16)
        g = jnp.minimum(jnp.dot(x, wgb_ref[...], preferred_element_type=F32) + bg_ref[...], SWIGLU_LIMIT)
        u = jnp.clip(jnp.dot(x, wub_ref[...], preferred_element_type=F32) + bu_ref[...], -SWIGLU_LIMIT, SWIGLU_LIMIT)
        act = g * jax.nn.sigmoid(SWIGLU_ALPHA * g) * (u + 1.0)
        y = jnp.dot(act.astype(BF16), wdb_ref[...], preferred_element_type=F32) + bd_ref[...]
        o_ref[...] = y.astype(BF16)

    @pl.when(i >= nu_ref[0])
    def _():
        o_ref[...] = jnp.zeros(o_ref.shape, BF16)


def _experts(layer, blk_e, n_used, xd, w_gate, b_gate, w_up, b_up, w_down, b_down):
    cap = xd.shape[0]
    wspec = pl.BlockSpec((None, None, D, D), lambda i, be, nu: (layer, be[i], 0, 0))
    bspec = pl.BlockSpec((None, None, 1, D), lambda i, be, nu: (layer, be[i], 0, 0))
    row = pl.BlockSpec((TME, D), lambda i, be, nu: (i, 0))
    xrow = pl.BlockSpec((TME, D // 2), lambda i, be, nu: (i, 0))
    b4 = lambda b: b.reshape(DEPTH, N_EXP, 1, D)
    return pl.pallas_call(
        _expert_kernel,
        grid_spec=pltpu.PrefetchScalarGridSpec(
            num_scalar_prefetch=2,
            grid=(cap // TME,),
            in_specs=[xrow, wspec, bspec, wspec, bspec, wspec, bspec],
            out_specs=row,
            scratch_shapes=[pltpu.VMEM((D, D), BF16)] * 3),
        out_shape=jax.ShapeDtypeStruct((cap, D), BF16),
        compiler_params=_cp(("arbitrary",), 56),
        name="moe_experts",
    )(blk_e, n_used, xd, w_gate, b4(b_gate), w_up, b4(b_up), w_down, b4(b_down))


def _moe_finish_kernel(yg_ref, w_ref, x_ref, gate_ref, lg_ref, lb_ref, o_ref):
    w = w_ref[...]
    y = yg_ref[0].astype(F32) * w[:, 0:1]
    for kk in range(1, TOP_K):
        y = y + yg_ref[kk].astype(F32) * w[:, kk:kk + 1]
    z = ALPHA * x_ref[...] + gate_ref[0] * y
    o_ref[...] = _layer_norm_rows(z, lg_ref[...], lb_ref[...])


def _moe_finish(yg, top_w, tok, gate, ln_g, ln_b, nctxb):
    nt = tok.shape[0]
    row = lambda i: (i, 0)
    const = lambda i: (0, 0)
    return pl.pallas_call(
        _moe_finish_kernel,
        grid=(nt // TM,),
        in_specs=[pl.BlockSpec((TOP_K, TM, D), lambda i: (0, i, 0)), pl.BlockSpec((TM, TOP_K), row),
                  pl.BlockSpec((TM, D), row),
                  pl.BlockSpec((1, 1, D), _rowtype(nctxb)), pl.BlockSpec((1, D), const), pl.BlockSpec((1, D), const)],
        out_specs=pl.BlockSpec((TM, D), row),
        out_shape=jax.ShapeDtypeStruct((nt, D), F32),
        compiler_params=_cp(("arbitrary",)),
        name="moe_finish",
    )(yg, top_w, tok, gate, ln_g, ln_b)


def _moe_layer(layer, tok, sc1p, sh, gate, ln_g, ln_b, w_router, b_router, w_gate, b_gate, w_up, b_up,
               w_down, b_down, nctxb):
    nt = tok.shape[0]
    wr = jnp.zeros((D, 128), F32).at[:, :N_EXP].set(w_router.astype(F32))
    wr_hi = wr.astype(BF16)
    wr_lo = (wr - wr_hi.astype(F32)).astype(BF16)
    br = jnp.full((1, 128), -1e30, F32).at[0, :N_EXP].set(b_router.astype(F32))
    hw, ri, rw, cnt = _router(tok, sc1p, sh, wr_hi, wr_lo, br, nctxb)
    top_idx, rank, top_w = ri[:TOP_K], ri[TOP_K:2 * TOP_K], rw[:, :TOP_K]
    counts = cnt[0, :N_EXP].astype(jnp.int32)
    padded = (counts + TME - 1) // TME * TME
    pad_end = jnp.cumsum(padded)
    pad_start = pad_end - padded
    dest = (pad_start[top_idx] + rank).astype(jnp.int32).reshape(-1)
    cap = -(-(nt * TOP_K + N_EXP * TME) // TME) * TME
    nb = cap // TME
    blk_start = jnp.arange(nb, dtype=jnp.int32) * TME
    blk_e = jnp.minimum(jnp.sum((pad_end[None, :] <= blk_start[:, None]).astype(jnp.int32), axis=1), N_EXP - 1)
    n_used = (pad_end[-1] // TME).astype(jnp.int32).reshape(1)
    xd = _dispatch(dest, hw, cap)
    yd = _experts(layer, blk_e, n_used, xd, w_gate, b_gate, w_up, b_up, w_down, b_down)
    yg = yd.at[dest].get(mode="promise_in_bounds").reshape(TOP_K, nt, D)
    return _moe_finish(yg, top_w, tok, gate, ln_g, ln_b, nctxb)


def _rope_tables(rows, n_ctx):
    row = jnp.broadcast_to(jnp.arange(rows, dtype=F32)[:, None], (rows, GRID_W)).reshape(-1)
    col = jnp.broadcast_to(jnp.arange(GRID_W, dtype=F32)[None, :], (rows, GRID_W)).reshape(-1)
    theta = ROPE_BASE ** (-jnp.arange(ROPE_F, dtype=F32) / ROPE_F)
    ang = jnp.stack([row[:, None] * theta, col[:, None] * theta], axis=1)
    ang = jnp.stack([ang, ang], axis=2).reshape(rows * GRID_W, HD)
    cos = jnp.concatenate([jnp.ones((n_ctx, HD), F32), jnp.cos(ang)], axis=0)
    sin = jnp.concatenate([jnp.zeros((n_ctx, HD), F32), jnp.sin(ang)], axis=0)
    return jnp.tile(cos, (1, 2)), jnp.tile(sin, (1, 2))


def _rot_cols(w):
    k = w.shape[0]
    wr = w.reshape(k, -1, 2, 2, ROPE_F)
    rot = jnp.stack([-wr[..., 1, :], wr[..., 0, :]], axis=-2)
    return rot.reshape(w.shape)


def kernel(x, c, ctx, c_ctx, ada_w, ada_b, ln_g, ln_b, even_w_in, s5_lam_re, s5_lam_im, s5_log_dt, s5_b_re, s5_b_im, s5_c_re, s5_c_im, s5_d, s5_w_glu, s5_b_glu, fnet_w, fnet_b, even_w_out, odd_w_qkv, odd_w_o, da_lq1, da_lk1, da_lq2, da_lk2, da_subln_g, router_w, router_b, moe_w_gate, moe_b_gate, moe_w_up, moe_b_up, moe_w_down, moe_b_down):
    seq = x.shape[1]
    n_ctx = ctx.shape[1]
    assert x.shape[0] == 1 and n_ctx % TM == 0 and seq % TM == 0 and n_ctx == TM
    nctxb = n_ctx // TM
    nt = n_ctx + seq
    nc = nt // CH
    ncc = n_ctx // CH
    tq_lat = 1024 if seq % 1024 == 0 else TM
    ts_lat = 512 if seq % 1024 == 0 else TM
    assert seq % (2 * ts_lat) == 0

    tok = jnp.concatenate([ctx[0], x[0]], axis=0).astype(F32)
    cond8 = jnp.zeros((8, D), F32).at[0].set(c_ctx.astype(F32)).at[1].set(c[0].astype(F32))
    mods = _ada_mods(cond8, ada_w, ada_b)[:, :2].reshape(DEPTH, 2, 6, 1, D)
    cos, sin = _rope_tables(seq // GRID_W, n_ctx)
    c128, s128 = _cos_sin(FN_C, FN_C, FN_C)

    for l in range(DEPTH):
        i = l // 2
        m = mods[l]
        shift_a, scale_a, gate_a, shift_b, scale_b, gate_b = (m[:, j] for j in range(6))
        lg0, lb0 = ln_g[l, 0].reshape(1, D), ln_b[l, 0].reshape(1, D)
        lg1, lb1 = ln_g[l, 1].reshape(1, D), ln_b[l, 1].reshape(1, D)
        if l % 2 == 0:
            wf = jnp.concatenate([jnp.einsum('ab,gbd->gad', c128, fnet_w[i].astype(F32), precision=HIGHEST),
                                  jnp.einsum('ab,gbd->gad', s128, fnet_w[i].astype(F32), precision=HIGHEST)],
                                 axis=-1).astype(BF16)
            u, p, q = _even_in(tok, 1.0 + scale_a, shift_a, even_w_in[i].astype(BF16), wf, nctxb)
            m1, w2, avec = _s5_matrices(s5_lam_re[i], s5_lam_im[i], s5_log_dt[i], s5_b_re[i], s5_b_im[i],
                                        s5_c_re[i], s5_c_im[i], s5_d[i])
            yi, sfr, sfi, sbr, sbi = _s5_in(u, m1)
            hfr, hfi, hbr, hbi = _s5_scan(sfr, sfi, sbr, sbi, avec, ncc)
            ys = _s5_out(yi, hfr, hfi, hbr, hbi, w2)
            g = _fnet_positions(p, q, fnet_b[i], n_ctx)
            tok = _even_out(ys, g, tok, s5_w_glu[i].astype(BF16), s5_b_glu[i].reshape(1, S5_W).astype(F32),
                            even_w_out[i].astype(BF16), gate_a, lg0, lb0, nctxb)
        else:
            lam_init = 0.8 - 0.6 * math.exp(-0.3 * l)
            lam = (jnp.exp(jnp.sum(da_lq1[i].astype(F32) * da_lk1[i].astype(F32)))
                   - jnp.exp(jnp.sum(da_lq2[i].astype(F32) * da_lk2[i].astype(F32))) + lam_init)
            wq, wk, wv = odd_w_qkv[i][:, :D], odd_w_qkv[i][:, D:2 * D], odd_w_qkv[i][:, 2 * D:]
            w5 = jnp.concatenate([wq, _rot_cols(wq), wk, _rot_cols(wk), wv], axis=1).astype(BF16)
            qb, kb, vb = _qkv(tok, 1.0 + scale_a, shift_a, w5, cos, sin, nctxb)
            lamv = jnp.full((1, VD), lam, F32)
            gsv = (da_subln_g[i].astype(F32) * (1.0 - lam_init)).reshape(1, VD)
            on_lat = _attention(qb[n_ctx:], kb, vb, lamv, gsv, tq_lat, n_ctx, ts_lat)
            on_ctx = _attention(qb[:n_ctx], kb[:n_ctx], vb[:n_ctx], lamv, gsv, n_ctx, n_ctx, 0)
            on = jnp.concatenate([on_ctx, on_lat], axis=0)
            tok = _proj_ln(on, odd_w_o[i].astype(BF16), tok, gate_a, lg0, lb0, nctxb)
        tok = _moe_layer(l, tok, 1.0 + scale_b, shift_b, gate_b, lg1, lb1, router_w[l], router_b[l],
                         moe_w_gate, moe_b_gate, moe_w_up, moe_b_up, moe_w_down, moe_b_down, nctxb)
    return tok[n_ctx:].reshape(1, seq, D).astype(x.dtype)
```

```python
import functools
import math

import jax
import jax.numpy as jnp
from jax import lax
from jax.experimental import pallas as pl
from jax.experimental.pallas import tpu as pltpu

F32 = jnp.float32
BF16 = jnp.bfloat16
HIGHEST = lax.Precision.HIGHEST

D = 1024
DEPTH = 4
GRID_W = 64
S5_W = 512
S5_G = 32
S5_C = 16
S5_N = 64
FN_W = 512
FN_G = 4
FN_C = 128
HEADS = 8
HD = 64
VD = 128
ROPE_BASE = 10000.0
ROPE_F = 16
N_EXP = 32
TOP_K = 4
SWIGLU_LIMIT = 7.0
SWIGLU_ALPHA = 1.702
ALPHA = (2.0 * DEPTH) ** 0.25
LN_EPS = 1e-5
LOG2E = 1.4426950408889634
HI16 = -65536

TM = 256
TME = 736
CH = 16
MIB = 1024 * 1024


def _cp(sem, vmem_mib=48):
    return pltpu.CompilerParams(dimension_semantics=sem, vmem_limit_bytes=vmem_mib * MIB)


def _rowtype(nctxb):
    return lambda i: (jnp.where(i >= nctxb, 1, 0), 0, 0)


def _layer_norm_rows(z, g, b):
    mu = jnp.mean(z, axis=-1, keepdims=True)
    zc = z - mu
    var = jnp.mean(zc * zc, axis=-1, keepdims=True)
    return zc * lax.rsqrt(var + LN_EPS) * g + b


def _ada_kernel(c_ref, w_ref, b_ref, o_ref):
    c = c_ref[...]
    a = c * jax.nn.sigmoid(c)
    o_ref[0] = jnp.dot(a, w_ref[0], preferred_element_type=F32, precision=HIGHEST) + b_ref[0]


def _ada_mods(cond8, ada_w, ada_b):
    tn = 1536
    return pl.pallas_call(
        _ada_kernel,
        grid=(DEPTH, 6 * D // tn),
        in_specs=[pl.BlockSpec((8, D), lambda l, j: (0, 0)),
                  pl.BlockSpec((1, D, tn), lambda l, j: (l, 0, j)),
                  pl.BlockSpec((1, 1, tn), lambda l, j: (l, 0, j))],
        out_specs=pl.BlockSpec((1, 8, tn), lambda l, j: (l, 0, j)),
        out_shape=jax.ShapeDtypeStruct((DEPTH, 8, 6 * D), F32),
        compiler_params=_cp(("arbitrary", "arbitrary")),
        name="ada_mods",
    )(cond8, ada_w, ada_b.reshape(DEPTH, 1, 6 * D))


def _even_in_kernel(x_ref, sc_ref, sh_ref, w_ref, wf_ref, u_ref, p_ref, q_ref):
    h = x_ref[...] * sc_ref[0] + sh_ref[0]
    z = jnp.dot(h.astype(BF16), w_ref[...], preferred_element_type=F32)
    u_ref[...] = z[:, :S5_W]
    for g in range(FN_G):
        f = z[:, S5_W + FN_C * g:S5_W + FN_C * (g + 1)]
        fc = f - jnp.mean(f, axis=-1, keepdims=True)
        fn = fc * lax.rsqrt(jnp.mean(fc * fc, axis=-1, keepdims=True) + LN_EPS)
        pq = jnp.dot(fn.astype(BF16), wf_ref[g], preferred_element_type=F32)
        p_ref[:, FN_C * g:FN_C * (g + 1)] = pq[:, :FN_C]
        q_ref[:, FN_C * g:FN_C * (g + 1)] = pq[:, FN_C:]


def _even_in(tok, sc1p, sh, w_in, wf, nctxb):
    nt = tok.shape[0]
    row = lambda i: (i, 0)
    return pl.pallas_call(
        _even_in_kernel,
        grid=(nt // TM,),
        in_specs=[pl.BlockSpec((TM, D), row),
                  pl.BlockSpec((1, 1, D), _rowtype(nctxb)),
                  pl.BlockSpec((1, 1, D), _rowtype(nctxb)),
                  pl.BlockSpec((D, D), lambda i: (0, 0)),
                  pl.BlockSpec((FN_G, FN_C, 2 * FN_C), lambda i: (0, 0, 0))],
        out_specs=[pl.BlockSpec((TM, S5_W), row),
                   pl.BlockSpec((TM, FN_W), row),
                   pl.BlockSpec((TM, FN_W), row)],
        out_shape=[jax.ShapeDtypeStruct((nt, S5_W), F32),
                   jax.ShapeDtypeStruct((nt, FN_W), F32),
                   jax.ShapeDtypeStruct((nt, FN_W), F32)],
        compiler_params=_cp(("arbitrary",)),
        name="even_in",
    )(tok, sc1p, sh, w_in, wf)


def _lane_iota(shape):
    return lax.broadcasted_iota(jnp.int32, shape, 1)


def _s5_in_kernel(x_ref, m1_ref, yi_ref, sfr_ref, sfi_ref, sbr_ref, sbi_ref):
    nc = x_ref.shape[0] // CH
    lane = _lane_iota((nc, 128)) // 32
    for pp in range(4):
        @pl.when(pl.program_id(1) == pp)
        def _(pp=pp):
            tiles = []
            for q in range(4):
                acc = None
                for r in range(4):
                    xs = x_ref[pl.ds(4 * q + r, nc, stride=CH), :]
                    shift = ((r - pp) * 32) % 128
                    if shift:
                        xs = pltpu.roll(xs, shift, 1)
                    acc = xs if acc is None else jnp.where(lane == r, xs, acc)
                tiles.append(acc.astype(BF16))
            u = jnp.concatenate(tiles, axis=1)
            r_all = jnp.dot(u, m1_ref[0], preferred_element_type=F32)
            yi_ref[0] = r_all[:, :512]
            sfr_ref[...] = r_all[:, 512:640]
            sfi_ref[...] = r_all[:, 640:768]
            sbr_ref[...] = r_all[:, 768:896]
            sbi_ref[...] = r_all[:, 896:1024]


def _s5_in(u, m1):
    nt = u.shape[0]
    nc = nt // CH
    sspec = pl.BlockSpec((nc, 128), lambda i, j: (0, 4 * i + j))
    sshape = jax.ShapeDtypeStruct((nc, 2048), F32)
    return pl.pallas_call(
        _s5_in_kernel,
        grid=(4, 4),
        in_specs=[pl.BlockSpec((nt, 128), lambda i, j: (0, i)),
                  pl.BlockSpec((1, 512, 1024), lambda i, j: (4 * i + j, 0, 0))],
        out_specs=[pl.BlockSpec((1, nc, 512), lambda i, j: (4 * i + j, 0, 0)), sspec, sspec, sspec, sspec],
        out_shape=[jax.ShapeDtypeStruct((16, nc, 512), F32), sshape, sshape, sshape, sshape],
        compiler_params=_cp(("arbitrary", "arbitrary")),
        name="s5_in",
    )(u, m1)


def _s5_scan_kernel(ncc, sfr_ref, sfi_ref, sbr_ref, sbi_ref, a_ref,
                    hfr_ref, hfi_ref, hbr_ref, hbi_ref):
    nc = sfr_ref.shape[0]
    afr, afi, abr, abi = a_ref[0:1, :], a_ref[1:2, :], a_ref[2:3, :], a_ref[3:4, :]

    def body(i, carry):
        fr, fi, br, bi = carry
        jb = jnp.where(i < ncc, ncc - 1 - i, nc - 1 - i + ncc)
        hfr_ref[pl.ds(i, 1), :] = fr
        hfi_ref[pl.ds(i, 1), :] = fi
        hbr_ref[pl.ds(jb, 1), :] = br
        hbi_ref[pl.ds(jb, 1), :] = bi
        sr, si = sfr_ref[pl.ds(i, 1), :], sfi_ref[pl.ds(i, 1), :]
        tr, ti = sbr_ref[pl.ds(jb, 1), :], sbi_ref[pl.ds(jb, 1), :]
        return (afr * fr - afi * fi + sr, afr * fi + afi * fr + si,
                abr * br - abi * bi + tr, abr * bi + abi * br + ti)

    z = jnp.zeros((1, sfr_ref.shape[1]), F32)
    lax.fori_loop(0, nc, body, (z, z, z, z))


def _s5_scan(sfr, sfi, sbr, sbi, avec, ncc):
    nc = sfr.shape[0]
    spec = pl.BlockSpec((nc, 512), lambda i: (0, i))
    shape = jax.ShapeDtypeStruct((nc, 2048), F32)
    return pl.pallas_call(
        functools.partial(_s5_scan_kernel, ncc),
        grid=(4,),
        in_specs=[spec, spec, spec, spec, pl.BlockSpec((8, 512), lambda i: (0, i))],
        out_specs=[spec, spec, spec, spec],
        out_shape=[shape, shape, shape, shape],
        compiler_params=_cp(("arbitrary",)),
        name="s5_scan",
    )(sfr, sfi, sbr, sbi, avec)


def _s5_out_kernel(yi_ref, hfr_ref, hfi_ref, hbr_ref, hbi_ref, w2_ref, o_ref, yp_ref):
    nc = yi_ref.shape[1]
    pp_dyn = pl.program_id(1)
    hcat = jnp.concatenate([hfr_ref[...], hfi_ref[...], hbr_ref[...], hbi_ref[...]], axis=1).astype(BF16)
    yp_ref[pp_dyn] = yi_ref[0] + jnp.dot(hcat, w2_ref[0], preferred_element_type=F32)

    @pl.when(pp_dyn == 3)
    def _():
        lane = _lane_iota((nc, 128)) // 32
        for s in range(CH):
            acc = None
            for pp in range(4):
                ys = yp_ref[pp, :, (s // 4) * 128:(s // 4 + 1) * 128]
                shift = ((pp - s % 4) * 32) % 128
                if shift:
                    ys = pltpu.roll(ys, shift, 1)
                acc = ys if acc is None else jnp.where(lane == pp, ys, acc)
            o_ref[pl.ds(s, nc, stride=CH), :] = acc


def _s5_out(yi, hfr, hfi, hbr, hbi, w2):
    nc = yi.shape[1]
    hspec = pl.BlockSpec((nc, 128), lambda i, j: (0, 4 * i + j))
    return pl.pallas_call(
        _s5_out_kernel,
        grid=(4, 4),
        in_specs=[pl.BlockSpec((1, nc, 512), lambda i, j: (4 * i + j, 0, 0)), hspec, hspec, hspec, hspec,
                  pl.BlockSpec((1, 512, 512), lambda i, j: (4 * i + j, 0, 0))],
        out_specs=pl.BlockSpec((nc * CH, 128), lambda i, j: (0, i)),
        out_shape=jax.ShapeDtypeStruct((nc * CH, S5_W), F32),
        scratch_shapes=[pltpu.VMEM((4, nc, 512), F32)],
        compiler_params=_cp(("arbitrary", "arbitrary")),
        name="s5_out",
    )(yi, hfr, hfi, hbr, hbi, w2)


def _toeplitz_kernel(k_ref, o_ref):
    lane = _lane_iota((32, 128)) // 32
    for s in range(CH):
        for q in range(CH // 4):
            acc = None
            for j in range(4):
                piece = k_ref[0, 4 * q + j - s + CH - 1]
                acc = piece if acc is None else jnp.where(lane == j, piece, acc)
            o_ref[0, s * 32:(s + 1) * 32, q * 128:(q + 1) * 128] = acc.astype(BF16)


def _toeplitz(kblk_rep):
    npair, nlag = kblk_rep.shape[:2]
    return pl.pallas_call(
        _toeplitz_kernel,
        grid=(npair,),
        in_specs=[pl.BlockSpec((1, nlag, 32, 128), lambda p: (p, 0, 0, 0))],
        out_specs=pl.BlockSpec((1, CH * 32, CH * 32), lambda p: (p, 0, 0)),
        out_shape=jax.ShapeDtypeStruct((npair, CH * 32, CH * 32), BF16),
        compiler_params=_cp(("arbitrary",)),
        name="s5_toeplitz",
    )(kblk_rep)


def _s5_matrices(lam_re, lam_im, log_dt, b_re, b_im, c_re, c_im, d_skip):
    lam = lax.complex(lam_re.astype(F32), lam_im.astype(F32))
    dt = jnp.exp(log_dt.astype(F32))[..., None]
    ldt = lam * dt
    lam_bar = jnp.exp(ldt)
    bb = ((lam_bar - 1.0) / lam)[..., None] * lax.complex(b_re.astype(F32), b_im.astype(F32))
    cc = lax.complex(c_re.astype(F32), c_im.astype(F32))
    ks = jnp.arange(CH + 1, dtype=F32)
    pw = jnp.exp(ldt[..., None] * ks)
    kk = jnp.real(jnp.einsum('dgcn,dgnk,dgne->dkgce', cc, pw[..., :CH], bb, precision=HIGHEST))
    k0 = kk[0, 0] + kk[1, 0] + jnp.eye(S5_C, dtype=F32)[None] * d_skip.astype(F32)[:, :, None]
    kall = jnp.concatenate([kk[1, 1:][::-1], k0[None], kk[0, 1:]], axis=0)
    eye2 = jnp.eye(2, dtype=F32)
    kt = kall.transpose(1, 0, 3, 2).reshape(16, 2, 2 * CH - 1, S5_C, S5_C)
    kblk = jnp.einsum('pglcd,gh->plgchd', kt, eye2).reshape(16, 2 * CH - 1, 32, 32)
    tp = _toeplitz(jnp.tile(kblk, (1, 1, 1, 4)))
    w1f = jnp.einsum('gns,gnc->gscn', pw[0][..., :CH][..., ::-1], bb[0])
    w1b = jnp.einsum('gns,gnc->gscn', pw[1][..., :CH], bb[1])

    def pair_cols(w):
        w = w.reshape(16, 2, CH, S5_C, S5_N)
        return jnp.einsum('pgscn,gh->psgchn', w, eye2).reshape(16, 512, 128)

    m1 = jnp.concatenate([tp, pair_cols(jnp.real(w1f)).astype(BF16), pair_cols(jnp.imag(w1f)).astype(BF16),
                          pair_cols(jnp.real(w1b)).astype(BF16), pair_cols(jnp.imag(w1b)).astype(BF16)], axis=-1)
    cpf = jnp.einsum('gcn,gnt->gntc', cc[0], pw[0][..., 1:])
    cpb = jnp.einsum('gcn,gnt->gntc', cc[1], pw[1][..., 1:][..., ::-1])

    def pair_rows(w):
        w = w.reshape(16, 2, S5_N, CH, S5_C)
        return jnp.einsum('pgntc,gh->pgnthc', w, eye2).reshape(16, 128, 512)

    w2 = jnp.concatenate([pair_rows(jnp.real(cpf)), pair_rows(-jnp.imag(cpf)),
                          pair_rows(jnp.real(cpb)), pair_rows(-jnp.imag(cpb))], axis=1)
    a_f, a_b = pw[0][..., CH].reshape(-1), pw[1][..., CH].reshape(-1)
    z = jnp.zeros_like(jnp.real(a_f))
    avec = jnp.stack([jnp.real(a_f), jnp.imag(a_f), jnp.real(a_b), jnp.imag(a_b), z, z, z, z])
    return m1.astype(BF16), w2.astype(BF16), avec


def _dft_lat_kernel(n1, n2, scale, p_ref, q_ref, m_ref, c2_ref, s2_ref, rot_ref, b_ref, o_ref, zr_ref, zi_ref):
    length = n1 * n2
    rr, ri = rot_ref[0], rot_ref[1]

    def phase1(blk, tw):
        b0 = pl.multiple_of(blk * 8, 8)
        pv = p_ref.at[pl.ds(b0, length - n2 + 8), :]
        qv = q_ref.at[pl.ds(b0, length - n2 + 8), :]
        zrv = zr_ref.at[pl.ds(b0, length - n2 + 8), :]
        ziv = zi_ref.at[pl.ds(b0, length - n2 + 8), :]
        twr, twi = tw
        for j in range(8):
            rows = pl.ds(j, n1, stride=n2)
            x = jnp.concatenate([pv[rows, :], qv[rows, :]], axis=0).astype(BF16)
            z = jnp.dot(m_ref[...], x, preferred_element_type=F32)
            zr, zi = z[:n1], z[n1:]
            zrv[rows, :] = zr * twr - zi * twi
            ziv[rows, :] = zr * twi + zi * twr
            twr, twi = twr * rr - twi * ri, twr * ri + twi * rr
        return twr, twi

    lax.fori_loop(0, n2 // 8, phase1, (jnp.ones((n1, 128), F32), jnp.zeros((n1, 128), F32)))

    def phase2(blk, carry):
        k0 = pl.multiple_of(blk * 8, 8)
        ov = o_ref.at[pl.ds(k0, length - n1 + 8), :]
        for j in range(8):
            r0 = pl.multiple_of((k0 + j) * n2, 8)
            zr = zr_ref[pl.ds(r0, n2), :].astype(BF16)
            zi = zi_ref[pl.ds(r0, n2), :].astype(BF16)
            g = (jnp.dot(c2_ref[...], zr, preferred_element_type=F32)
                 + jnp.dot(s2_ref[...], zi, preferred_element_type=F32))
            ov[pl.ds(j, n2, stride=n1), :] = g * scale + b_ref[...]
        return carry

    lax.fori_loop(0, n1 // 8, phase2, 0)


def _dft_lat(p, q, m, c2, s2, rot, bias, n1, n2, scale):
    length, w = p.shape
    col = pl.BlockSpec((length, 128), lambda j: (0, j), pipeline_mode=pl.Buffered(1))
    const = lambda shape: pl.BlockSpec(shape, lambda j: (0,) * len(shape))
    return pl.pallas_call(
        functools.partial(_dft_lat_kernel, n1, n2, scale),
        grid=(w // 128,),
        in_specs=[col, col, const((2 * n1, 2 * n1)), const((n2, n2)), const((n2, n2)), const((2, n1, 128)),
                  pl.BlockSpec((1, 128), lambda j: (0, j))],
        out_specs=pl.BlockSpec((length, 128), lambda j: (0, j)),
        out_shape=jax.ShapeDtypeStruct((length, w), F32),
        scratch_shapes=[pltpu.VMEM((length, 128), F32), pltpu.VMEM((length, 128), F32)],
        compiler_params=_cp(("arbitrary",), 56),
        name="fnet_dft_lat",
    )(p, q, m, c2, s2, rot, bias)


def _dft_small_kernel(scale, c_ref, s_ref, p_ref, q_ref, b_ref, o_ref):
    g = (jnp.dot(c_ref[...], p_ref[...].astype(BF16), preferred_element_type=F32)
         - jnp.dot(s_ref[...], q_ref[...].astype(BF16), preferred_element_type=F32))
    o_ref[...] = g * scale + b_ref[...]


def _dft_small(cm, sm, p, q, bias, scale):
    n, w = p.shape
    full = lambda shape: pl.BlockSpec(shape, lambda i: (0,) * len(shape))
    return pl.pallas_call(
        functools.partial(_dft_small_kernel, scale),
        grid=(1,),
        in_specs=[full((n, n)), full((n, n)), full((n, w)), full((n, w)), full((1, w))],
        out_specs=full((n, w)),
        out_shape=jax.ShapeDtypeStruct((n, w), F32),
        compiler_params=_cp(("arbitrary",)),
        name="fnet_dft_ctx",
    )(cm, sm, p, q, bias)


def _cos_sin(n_rows, n_cols, period):
    ang = (2.0 * math.pi / period) * ((jnp.arange(n_rows)[:, None] * jnp.arange(n_cols)[None, :]) % period).astype(F32)
    return jnp.cos(ang), jnp.sin(ang)


def _fnet_positions(p, q, bias, n_ctx):
    pc, qc, pl_, ql_ = p[:n_ctx], q[:n_ctx], p[n_ctx:], q[n_ctx:]
    length = pl_.shape[0]
    n1 = 1 << ((length.bit_length() - 1) // 2)
    n2 = length // n1
    assert n1 * n2 == length and n1 % 8 == 0
    bias = bias.reshape(1, FN_W).astype(F32)
    cc, sc = _cos_sin(n_ctx, n_ctx, n_ctx)
    g_ctx = _dft_small(cc.astype(BF16), sc.astype(BF16), pc, qc, bias, 1.0 / math.sqrt(FN_C * n_ctx))
    c1, s1 = _cos_sin(n1, n1, n1)
    m = jnp.concatenate([jnp.concatenate([c1, -s1], axis=1),
                         jnp.concatenate([-s1, -c1], axis=1)], axis=0).astype(BF16)
    ang = (2.0 * math.pi / length) * jnp.arange(n1, dtype=F32)
    rot = jnp.stack([jnp.broadcast_to(jnp.cos(ang)[:, None], (n1, 128)),
                     jnp.broadcast_to(-jnp.sin(ang)[:, None], (n1, 128))])
    c2, s2 = _cos_sin(n2, n2, n2)
    g_lat = _dft_lat(pl_, ql_, m, c2.astype(BF16), s2.astype(BF16), rot, bias, n1, n2,
                     1.0 / math.sqrt(FN_C * length))
    return jnp.concatenate([g_ctx, g_lat], axis=0)


def _even_out_kernel(ys_ref, g_ref, x_ref, wglu_ref, bglu_ref, wout_ref, gate_ref, lg_ref, lb_ref, o_ref):
    ys = ys_ref[...]
    c0 = math.sqrt(2.0 / math.pi)
    y = 0.5 * ys * (1.0 + jnp.tanh(c0 * (ys + 0.044715 * (ys * ys * ys))))
    gl = jnp.dot(y.astype(BF16), wglu_ref[...], preferred_element_type=F32) + bglu_ref[...]
    y2 = y * jax.nn.sigmoid(gl)
    m = (jnp.dot(y2.astype(BF16), wout_ref[:S5_W, :], preferred_element_type=F32)
         + jnp.dot(g_ref[...].astype(BF16), wout_ref[S5_W:, :], preferred_element_type=F32))
    z = ALPHA * x_ref[...] + gate_ref[0] * m
    o_ref[...] = _layer_norm_rows(z, lg_ref[...], lb_ref[...])


def _even_out(ys, g, tok, w_glu, b_glu, w_out, gate, ln_g, ln_b, nctxb):
    nt = tok.shape[0]
    row = lambda i: (i, 0)
    const = lambda i: (0, 0)
    return pl.pallas_call(
        _even_out_kernel,
        grid=(nt // TM,),
        in_specs=[pl.BlockSpec((TM, S5_W), row), pl.BlockSpec((TM, FN_W), row), pl.BlockSpec((TM, D), row),
                  pl.BlockSpec((S5_W, S5_W), const), pl.BlockSpec((1, S5_W), const),
                  pl.BlockSpec((D, D), const), pl.BlockSpec((1, 1, D), _rowtype(nctxb)),
                  pl.BlockSpec((1, D), const), pl.BlockSpec((1, D), const)],
        out_specs=pl.BlockSpec((TM, D), row),
        out_shape=jax.ShapeDtypeStruct((nt, D), F32),
        compiler_params=_cp(("arbitrary",)),
        name="even_out",
    )(ys, g, tok, w_glu, b_glu, w_out, gate, ln_g, ln_b)


def _qkv_kernel(x_ref, sc_ref, sh_ref, w_ref, cos_ref, sin_ref, q_ref, k_ref, v_ref):
    h = (x_ref[...] * sc_ref[0] + sh_ref[0]).astype(BF16)
    cos, sin = cos_ref[...], sin_ref[...]
    z = jnp.dot(h, w_ref[...], preferred_element_type=F32)
    for hh in range(HEADS):
        sl = slice(hh * 128, (hh + 1) * 128)
        q = z[:, sl] * cos + z[:, D + hh * 128:D + (hh + 1) * 128] * sin
        k = z[:, 2 * D + hh * 128:2 * D + (hh + 1) * 128] * cos + z[:, 3 * D + hh * 128:3 * D + (hh + 1) * 128] * sin
        q_ref[:, sl] = (q * (HD ** -0.5 * LOG2E)).astype(BF16)
        k_ref[:, sl] = k.astype(BF16)
        v_ref[:, 2 * hh * VD:(2 * hh + 1) * VD] = z[:, 4 * D + hh * VD:4 * D + (hh + 1) * VD].astype(BF16)
        v_ref[:, (2 * hh + 1) * VD:(2 * hh + 2) * VD] = jnp.ones((z.shape[0], VD), BF16)


def _qkv(tok, sc1p, sh, w5, cos, sin, nctxb):
    nt = tok.shape[0]
    row = lambda i: (i, 0)
    o = jax.ShapeDtypeStruct((nt, D), BF16)
    ov = jax.ShapeDtypeStruct((nt, 2 * D), BF16)
    return pl.pallas_call(
        _qkv_kernel,
        grid=(nt // TM,),
        in_specs=[pl.BlockSpec((TM, D), row),
                  pl.BlockSpec((1, 1, D), _rowtype(nctxb)),
                  pl.BlockSpec((1, 1, D), _rowtype(nctxb)),
                  pl.BlockSpec((D, 5 * D), lambda i: (0, 0)),
                  pl.BlockSpec((TM, 128), row), pl.BlockSpec((TM, 128), row)],
        out_specs=[pl.BlockSpec((TM, D), row), pl.BlockSpec((TM, D), row), pl.BlockSpec((TM, 2 * D), row)],
        out_shape=[o, o, ov],
        compiler_params=_cp(("arbitrary",)),
        name="qkv_rope",
    )(tok, sc1p, sh, w5, cos, sin)


def _attn_kernel(n_head, ts, n_pairs, q_ref, k_ref, v_ref, lam_ref, gs_ref, o_ref,
                 qq_ref, sa_ref, sb_ref, m_ref, acc_ref):
    tq = q_ref.shape[0]
    q = q_ref[...]
    lane = lax.broadcasted_iota(jnp.int32, q.shape, 1)
    zero = jnp.zeros_like(q)
    qq_ref[:tq, :] = jnp.where(lane < HD, q, zero)
    qq_ref[tq:, :] = jnp.where(lane >= HD, q, zero)
    m_ref[...] = jnp.full(m_ref.shape, -1e30, F32)
    acc_ref[...] = jnp.zeros(acc_ref.shape, F32)

    def scores(off, size, dst_ref):
        dst_ref[:, :size] = lax.dot_general(qq_ref[...], k_ref[pl.ds(off, size), :],
                                            (((1,), (1,)), ((), ())), preferred_element_type=F32)

    def consume(off, size, src_ref):
        tiles = [src_ref[:, t * 128:(t + 1) * 128] for t in range(size // 128)]
        mx = functools.reduce(jnp.maximum, tiles)
        m_old = m_ref[...]
        m_new = jnp.maximum(m_old, jnp.max(mx, axis=-1, keepdims=True))
        alpha = jnp.exp2(m_old - m_new)
        p = jnp.concatenate([jnp.exp2(t - m_new).astype(BF16) for t in tiles], axis=1)
        pv = jnp.dot(p, v_ref[pl.ds(off, size), :], preferred_element_type=F32)
        acc_ref[...] = jnp.concatenate([alpha, alpha], axis=1) * acc_ref[...] + pv
        m_ref[...] = m_new

    scores(0, n_head, sa_ref)
    if n_pairs:
        scores(n_head, ts, sb_ref)
    consume(0, n_head, sa_ref)

    def pair(i, last):
        off0 = pl.multiple_of(n_head + (2 * i) * ts, 128)
        off1 = pl.multiple_of(n_head + (2 * i + 1) * ts, 128)
        scores(off1, ts, sa_ref)
        consume(off0, ts, sb_ref)
        if not last:
            scores(pl.multiple_of(n_head + (2 * i + 2) * ts, 128), ts, sb_ref)
        consume(off1, ts, sa_ref)

    if n_pairs:
        def body(i, carry):
            pair(2 * i, False)
            pair(2 * i + 1, False)
            return carry
        lax.fori_loop(0, (n_pairs - 1) // 2, body, 0)
        if (n_pairs - 1) % 2:
            pair(n_pairs - 2, False)
        pair(n_pairs - 1, True)

    acc = acc_ref[...]
    o12 = acc[:, :VD] / acc[:, VD:]
    o = o12[:tq] - lam_ref[...] * o12[tq:]
    o = o * lax.rsqrt(jnp.mean(o * o, axis=-1, keepdims=True) + LN_EPS)
    o_ref[...] = (o * gs_ref[...]).astype(BF16)


def _attention(q, k, vext, lamv, gsv, tq, n_head, ts):
    nq, nk = q.shape[0], k.shape[0]
    n_sub = (nk - n_head) // ts if ts else 0
    assert n_head + n_sub * ts == nk and n_sub % 2 == 0 and nq % tq == 0
    const = lambda h, i: (0, 0)
    return pl.pallas_call(
        functools.partial(_attn_kernel, n_head, ts, n_sub // 2),
        grid=(HEADS, nq // tq),
        in_specs=[pl.BlockSpec((tq, 128), lambda h, i: (i, h)),
                  pl.BlockSpec((nk, 128), lambda h, i: (0, h), pipeline_mode=pl.Buffered(1)),
                  pl.BlockSpec((nk, 2 * VD), lambda h, i: (0, h), pipeline_mode=pl.Buffered(1)),
                  pl.BlockSpec((1, 128), const), pl.BlockSpec((1, 128), const)],
        out_specs=pl.BlockSpec((tq, 128), lambda h, i: (i, h)),
        out_shape=jax.ShapeDtypeStruct((nq, D), BF16),
        scratch_shapes=[pltpu.VMEM((2 * tq, 128), BF16),
                        pltpu.VMEM((2 * tq, max(ts, n_head)), F32), pltpu.VMEM((2 * tq, max(ts, n_head)), F32),
                        pltpu.VMEM((2 * tq, 128), F32), pltpu.VMEM((2 * tq, 2 * VD), F32)],
        compiler_params=_cp(("arbitrary", "arbitrary"), 56),
        name="diff_attention",
    )(q, k, vext, lamv, gsv)


def _proj_ln_kernel(a_ref, w_ref, x_ref, gate_ref, lg_ref, lb_ref, o_ref):
    m = jnp.dot(a_ref[...], w_ref[...], preferred_element_type=F32)
    z = ALPHA * x_ref[...] + gate_ref[0] * m
    o_ref[...] = _layer_norm_rows(z, lg_ref[...], lb_ref[...])


def _proj_ln(a, w, tok, gate, ln_g, ln_b, nctxb):
    nt = tok.shape[0]
    row = lambda i: (i, 0)
    const = lambda i: (0, 0)
    return pl.pallas_call(
        _proj_ln_kernel,
        grid=(nt // TM,),
        in_specs=[pl.BlockSpec((TM, D), row), pl.BlockSpec((D, D), const), pl.BlockSpec((TM, D), row),
                  pl.BlockSpec((1, 1, D), _rowtype(nctxb)), pl.BlockSpec((1, D), const), pl.BlockSpec((1, D), const)],
        out_specs=pl.BlockSpec((TM, D), row),
        out_shape=jax.ShapeDtypeStruct((nt, D), F32),
        compiler_params=_cp(("arbitrary",)),
        name="attn_out_ln",
    )(a, w, tok, gate, ln_g, ln_b)


def _router_kernel(x_ref, sc_ref, sh_ref, wh_ref, wl_ref, b_ref, tri_ref, h_ref, ri_ref, rw_ref, cnt_ref,
                   run_ref):
    @pl.when(pl.program_id(0) == 0)
    def _():
        run_ref[...] = jnp.zeros(run_ref.shape, F32)

    h = x_ref[...] * sc_ref[0] + sh_ref[0]
    hh = h.astype(BF16)
    hl = (h - hh.astype(F32)).astype(BF16)
    bits = lax.bitcast_convert_type(hh.astype(F32), jnp.int32)
    h_ref[...] = (bits[:, D // 2:] & HI16) | lax.shift_right_logical(bits[:, :D // 2], 16)
    v = (jnp.dot(hh, wh_ref[...], preferred_element_type=F32)
         + jnp.dot(hh, wl_ref[...], preferred_element_type=F32)
         + jnp.dot(hl, wh_ref[...], preferred_element_type=F32)) + b_ref[...]
    lane = lax.broadcasted_iota(jnp.int32, v.shape, 1)
    lane_f = lane.astype(F32)
    vals, sels, idxs = [], [], []
    for _ in range(TOP_K):
        mk = jnp.max(v, axis=-1, keepdims=True)
        ik = jnp.min(jnp.where(v == mk, lane_f, 128.0), axis=-1, keepdims=True)
        sel = lane_f == ik
        v = jnp.where(sel, -jnp.inf, v)
        vals.append(mk)
        sels.append(sel)
        idxs.append(ik)
    es = [jnp.exp(mk - vals[0]) for mk in vals]
    den = es[0] + es[1] + es[2] + es[3]
    onehot = jnp.zeros(v.shape, F32)
    for sel in sels:
        onehot = jnp.where(sel, 1.0, onehot)
    prefix = jnp.dot(tri_ref[...], onehot.astype(BF16), preferred_element_type=F32)
    rank_all = prefix + run_ref[0:1, :]
    run_ref[0:1, :] = run_ref[0:1, :] + jnp.sum(onehot, axis=0, keepdims=True)
    ri = jnp.zeros(v.shape, F32)
    rw = jnp.zeros(v.shape, F32)
    for k in range(TOP_K):
        rk = jnp.sum(jnp.where(sels[k], rank_all, 0.0), axis=-1, keepdims=True)
        ri = jnp.where(lane == k, idxs[k], ri)
        ri = jnp.where(lane == TOP_K + k, rk, ri)
        rw = jnp.where(lane == k, es[k] / den, rw)
    ri_ref[...] = ri.T[:8, :].astype(jnp.int32)
    rw_ref[...] = rw
    cnt_ref[...] = run_ref[...]


def _router(tok, sc1p, sh, w_hi, w_lo, bias, nctxb):
    nt = tok.shape[0]
    row = lambda i: (i, 0)
    const = lambda i: (0, 0)
    tri = (jnp.arange(TM)[:, None] > jnp.arange(TM)[None, :]).astype(BF16)
    return pl.pallas_call(
        _router_kernel,
        grid=(nt // TM,),
        in_specs=[pl.BlockSpec((TM, D), row),
                  pl.BlockSpec((1, 1, D), _rowtype(nctxb)), pl.BlockSpec((1, 1, D), _rowtype(nctxb)),
                  pl.BlockSpec((D, 128), const), pl.BlockSpec((D, 128), const), pl.BlockSpec((1, 128), const),
                  pl.BlockSpec((TM, TM), const)],
        out_specs=[pl.BlockSpec((TM, D // 2), row), pl.BlockSpec((8, TM), lambda i: (0, i)),
                   pl.BlockSpec((TM, 128), row),
                   pl.BlockSpec((8, 128), const)],
        out_shape=[jax.ShapeDtypeStruct((nt, D // 2), jnp.int32), jax.ShapeDtypeStruct((8, nt), jnp.int32),
                   jax.ShapeDtypeStruct((nt, 128), F32), jax.ShapeDtypeStruct((8, 128), F32)],
        scratch_shapes=[pltpu.VMEM((8, 128), F32)],
        compiler_params=_cp(("arbitrary",)),
        name="moe_router",
    )(tok, sc1p, sh, w_hi, w_lo, bias, tri)


def _dispatch_kernel(dest_ref, x_ref, init_ref, o_ref, sem):
    del init_ref
    nt = dest_ref.shape[0] // TOP_K
    base = pl.program_id(0) * TM

    def row_copy(r, k):
        slot = dest_ref[k * nt + base + r]
        return pltpu.make_async_copy(x_ref.at[pl.ds(r, 1), :], o_ref.at[pl.ds(slot, 1), :], sem)

    def issue(r, carry):
        for k in range(TOP_K):
            row_copy(r, k).start(priority=k % 2)
        return carry

    def drain(r, carry):
        for k in range(TOP_K):
            row_copy(r, k).wait()
        return carry

    lax.fori_loop(0, TM, issue, 0, unroll=4)
    lax.fori_loop(0, TM, drain, 0, unroll=4)


def _dispatch(dest_flat, hw, cap):
    nt, w = hw.shape
    return pl.pallas_call(
        _dispatch_kernel,
        grid_spec=pltpu.PrefetchScalarGridSpec(
            num_scalar_prefetch=1,
            grid=(nt // TM,),
            in_specs=[pl.BlockSpec((TM, w), lambda i, d: (i, 0)), pl.BlockSpec(memory_space=pl.ANY)],
            out_specs=pl.BlockSpec(memory_space=pl.ANY),
            scratch_shapes=[pltpu.SemaphoreType.DMA(())]),
        out_shape=jax.ShapeDtypeStruct((cap, w), hw.dtype),
        input_output_aliases={2: 0},
        compiler_params=_cp(("arbitrary",)),
        name="moe_dispatch",
    )(dest_flat, hw, jnp.zeros((cap, w), hw.dtype))


def _expert_kernel(be_ref, nu_ref, x_ref, wg_ref, bg_ref, wu_ref, bu_ref, wd_ref, bd_ref, o_ref,
                   wgb_ref, wub_ref, wdb_ref):
    i = pl.program_id(0)
    prev = be_ref[jnp.maximum(i - 1, 0)]
    new_expert = jnp.logical_or(i == 0, be_ref[i] != prev)

    @pl.when(new_expert)
    def _():
        wgb_ref[...] = wg_ref[...].astype(BF16)
        wub_ref[...] = wu_ref[...].astype(BF16)
        wdb_ref[...] = wd_ref[...].astype(BF16)

    @pl.when(i < nu_ref[0])
    def _():
        xw = x_ref[...]
        x = jnp.concatenate([lax.bitcast_convert_type(lax.shift_left(xw, 16), F32),
                             lax.bitcast_convert_type(xw & HI16, F32)], axis=1).astype(BF16)
        g = jnp.minimum(jnp.dot(x, wgb_ref[...], preferred_element_type=F32) + bg_ref[...], SWIGLU_LIMIT)
        u = jnp.clip(jnp.dot(x, wub_ref[...], preferred_element_type=F32) + bu_ref[...], -SWIGLU_LIMIT, SWIGLU_LIMIT)
        act = g * jax.nn.sigmoid(SWIGLU_ALPHA * g) * (u + 1.0)
        y = jnp.dot(act.astype(BF16), wdb_ref[...], preferred_element_type=F32) + bd_ref[...]
        o_ref[...] = y.astype(BF16)

    @pl.when(i >= nu_ref[0])
    def _():
        o_ref[...] = jnp.zeros(o_ref.shape, BF16)


def _experts(layer, blk_e, n_used, xd, w_gate, b_gate, w_up, b_up, w_down, b_down):
    cap = xd.shape[0]
    wspec = pl.BlockSpec((None, None, D, D), lambda i, be, nu: (layer, be[i], 0, 0))
    bspec = pl.BlockSpec((None, None, 1, D), lambda i, be, nu: (layer, be[i], 0, 0))
    row = pl.BlockSpec((TME, D), lambda i, be, nu: (i, 0))
    xrow = pl.BlockSpec((TME, D // 2), lambda i, be, nu: (i, 0))
    b4 = lambda b: b.reshape(DEPTH, N_EXP, 1, D)
    return pl.pallas_call(
        _expert_kernel,
        grid_spec=pltpu.PrefetchScalarGridSpec(
            num_scalar_prefetch=2,
            grid=(cap // TME,),
            in_specs=[xrow, wspec, bspec, wspec, bspec, wspec, bspec],
            out_specs=row,
            scratch_shapes=[pltpu.VMEM((D, D), BF16)] * 3),
        out_shape=jax.ShapeDtypeStruct((cap, D), BF16),
        compiler_params=_cp(("arbitrary",), 56),
        name="moe_experts",
    )(blk_e, n_used, xd, w_gate, b4(b_gate), w_up, b4(b_up), w_down, b4(b_down))


def _moe_finish_kernel(yg_ref, w_ref, x_ref, gate_ref, lg_ref, lb_ref, o_ref):
    w = w_ref[...]
    y = yg_ref[0].astype(F32) * w[:, 0:1]
    for kk in range(1, TOP_K):
        y = y + yg_ref[kk].astype(F32) * w[:, kk:kk + 1]
    z = ALPHA * x_ref[...] + gate_ref[0] * y
    o_ref[...] = _layer_norm_rows(z, lg_ref[...], lb_ref[...])


def _moe_finish(yg, top_w, tok, gate, ln_g, ln_b, nctxb):
    nt = tok.shape[0]
    row = lambda i: (i, 0)
    const = lambda i: (0, 0)
    return pl.pallas_call(
        _moe_finish_kernel,
        grid=(nt // TM,),
        in_specs=[pl.BlockSpec((TOP_K, TM, D), lambda i: (0, i, 0)), pl.BlockSpec((TM, TOP_K), row),
                  pl.BlockSpec((TM, D), row),
                  pl.BlockSpec((1, 1, D), _rowtype(nctxb)), pl.BlockSpec((1, D), const), pl.BlockSpec((1, D), const)],
        out_specs=pl.BlockSpec((TM, D), row),
        out_shape=jax.ShapeDtypeStruct((nt, D), F32),
        compiler_params=_cp(("arbitrary",)),
        name="moe_finish",
    )(yg, top_w, tok, gate, ln_g, ln_b)


def _moe_layer(layer, tok, sc1p, sh, gate, ln_g, ln_b, w_router, b_router, w_gate, b_gate, w_up, b_up,
               w_down, b_down, nctxb):
    nt = tok.shape[0]
    wr = jnp.zeros((D, 128), F32).at[:, :N_EXP].set(w_router.astype(F32))
    wr_hi = wr.astype(BF16)
    wr_lo = (wr - wr_hi.astype(F32)).astype(BF16)
    br = jnp.full((1, 128), -1e30, F32).at[0, :N_EXP].set(b_router.astype(F32))
    hw, ri, rw, cnt = _router(tok, sc1p, sh, wr_hi, wr_lo, br, nctxb)
    top_idx, rank, top_w = ri[:TOP_K], ri[TOP_K:2 * TOP_K], rw[:, :TOP_K]
    counts = cnt[0, :N_EXP].astype(jnp.int32)
    padded = (counts + TME - 1) // TME * TME
    pad_end = jnp.cumsum(padded)
    pad_start = pad_end - padded
    dest = rank
    for e in range(N_EXP):
        dest = dest + jnp.where(top_idx == e, pad_start[e], 0)
    dest = dest.astype(jnp.int32).reshape(-1)
    cap = -(-(nt * TOP_K + N_EXP * TME) // TME) * TME
    nb = cap // TME
    blk_start = jnp.arange(nb, dtype=jnp.int32) * TME
    blk_e = jnp.minimum(jnp.sum((pad_end[None, :] <= blk_start[:, None]).astype(jnp.int32), axis=1), N_EXP - 1)
    n_used = (pad_end[-1] // TME).astype(jnp.int32).reshape(1)
    xd = _dispatch(dest, hw, cap)
    yd = _experts(layer, blk_e, n_used, xd, w_gate, b_gate, w_up, b_up, w_down, b_down)
    yg = yd.at[dest].get(mode="promise_in_bounds").reshape(TOP_K, nt, D)
    return _moe_finish(yg, top_w, tok, gate, ln_g, ln_b, nctxb)


def _rope_tables(rows, n_ctx):
    row = jnp.broadcast_to(jnp.arange(rows, dtype=F32)[:, None], (rows, GRID_W)).reshape(-1)
    col = jnp.broadcast_to(jnp.arange(GRID_W, dtype=F32)[None, :], (rows, GRID_W)).reshape(-1)
    theta = ROPE_BASE ** (-jnp.arange(ROPE_F, dtype=F32) / ROPE_F)
    ang = jnp.stack([row[:, None] * theta, col[:, None] * theta], axis=1)
    ang = jnp.stack([ang, ang], axis=2).reshape(rows * GRID_W, HD)
    cos = jnp.concatenate([jnp.ones((n_ctx, HD), F32), jnp.cos(ang)], axis=0)
    sin = jnp.concatenate([jnp.zeros((n_ctx, HD), F32), jnp.sin(ang)], axis=0)
    return jnp.tile(cos, (1, 2)), jnp.tile(sin, (1, 2))


def _rot_cols(w):
    k = w.shape[0]
    wr = w.reshape(k, -1, 2, 2, ROPE_F)
    rot = jnp.stack([-wr[..., 1, :], wr[..., 0, :]], axis=-2)
    return rot.reshape(w.shape)


def kernel(x, c, ctx, c_ctx, ada_w, ada_b, ln_g, ln_b, even_w_in, s5_lam_re, s5_lam_im, s5_log_dt, s5_b_re, s5_b_im, s5_c_re, s5_c_im, s5_d, s5_w_glu, s5_b_glu, fnet_w, fnet_b, even_w_out, odd_w_qkv, odd_w_o, da_lq1, da_lk1, da_lq2, da_lk2, da_subln_g, router_w, router_b, moe_w_gate, moe_b_gate, moe_w_up, moe_b_up, moe_w_down, moe_b_down):
    seq = x.shape[1]
    n_ctx = ctx.shape[1]
    assert x.shape[0] == 1 and n_ctx % TM == 0 and seq % TM == 0 and n_ctx == TM
    nctxb = n_ctx // TM
    nt = n_ctx + seq
    nc = nt // CH
    ncc = n_ctx // CH
    tq_lat = 1024 if seq % 1024 == 0 else TM
    ts_lat = 512 if seq % 1024 == 0 else TM
    assert seq % (2 * ts_lat) == 0

    tok = jnp.concatenate([ctx[0], x[0]], axis=0).astype(F32)
    cond8 = jnp.zeros((8, D), F32).at[0].set(c_ctx.astype(F32)).at[1].set(c[0].astype(F32))
    mods = _ada_mods(cond8, ada_w, ada_b)[:, :2].reshape(DEPTH, 2, 6, 1, D)
    cos, sin = _rope_tables(seq // GRID_W, n_ctx)
    c128, s128 = _cos_sin(FN_C, FN_C, FN_C)

    for l in range(DEPTH):
        i = l // 2
        m = mods[l]
        shift_a, scale_a, gate_a, shift_b, scale_b, gate_b = (m[:, j] for j in range(6))
        lg0, lb0 = ln_g[l, 0].reshape(1, D), ln_b[l, 0].reshape(1, D)
        lg1, lb1 = ln_g[l, 1].reshape(1, D), ln_b[l, 1].reshape(1, D)
        if l % 2 == 0:
            wf = jnp.concatenate([jnp.einsum('ab,gbd->gad', c128, fnet_w[i].astype(F32), precision=HIGHEST),
                                  jnp.einsum('ab,gbd->gad', s128, fnet_w[i].astype(F32), precision=HIGHEST)],
                                 axis=-1).astype(BF16)
            u, p, q = _even_in(tok, 1.0 + scale_a, shift_a, even_w_in[i].astype(BF16), wf, nctxb)
            m1, w2, avec = _s5_matrices(s5_lam_re[i], s5_lam_im[i], s5_log_dt[i], s5_b_re[i], s5_b_im[i],
                                        s5_c_re[i], s5_c_im[i], s5_d[i])
            yi, sfr, sfi, sbr, sbi = _s5_in(u, m1)
            hfr, hfi, hbr, hbi = _s5_scan(sfr, sfi, sbr, sbi, avec, ncc)
            ys = _s5_out(yi, hfr, hfi, hbr, hbi, w2)
            g = _fnet_positions(p, q, fnet_b[i], n_ctx)
            tok = _even_out(ys, g, tok, s5_w_glu[i].astype(BF16), s5_b_glu[i].reshape(1, S5_W).astype(F32),
                            even_w_out[i].astype(BF16), gate_a, lg0, lb0, nctxb)
        else:
            lam_init = 0.8 - 0.6 * math.exp(-0.3 * l)
            lam = (jnp.exp(jnp.sum(da_lq1[i].astype(F32) * da_lk1[i].astype(F32)))
                   - jnp.exp(jnp.sum(da_lq2[i].astype(F32) * da_lk2[i].astype(F32))) + lam_init)
            wq, wk, wv = odd_w_qkv[i][:, :D], odd_w_qkv[i][:, D:2 * D], odd_w_qkv[i][:, 2 * D:]
            w5 = jnp.concatenate([wq, _rot_cols(wq), wk, _rot_cols(wk), wv], axis=1).astype(BF16)
            qb, kb, vb = _qkv(tok, 1.0 + scale_a, shift_a, w5, cos, sin, nctxb)
            lamv = jnp.full((1, VD), lam, F32)
            gsv = (da_subln_g[i].astype(F32) * (1.0 - lam_init)).reshape(1, VD)
            on_lat = _attention(qb[n_ctx:], kb, vb, lamv, gsv, tq_lat, n_ctx, ts_lat)
            on_ctx = _attention(qb[:n_ctx], kb[:n_ctx], vb[:n_ctx], lamv, gsv, n_ctx, n_ctx, 0)
            on = jnp.concatenate([on_ctx, on_lat], axis=0)
            tok = _proj_ln(on, odd_w_o[i].astype(BF16), tok, gate_a, lg0, lb0, nctxb)
        tok = _moe_layer(l, tok, 1.0 + scale_b, shift_b, gate_b, lg1, lb1, router_w[l], router_b[l],
                         moe_w_gate, moe_b_gate, moe_w_up, moe_b_up, moe_w_down, moe_b_down, nctxb)
    return tok[n_ctx:].reshape(1, seq, D).astype(x.dtype)
```

```python
import functools
import math

import jax
import jax.numpy as jnp
from jax import lax
from jax.experimental import pallas as pl
from jax.experimental.pallas import tpu as pltpu

F32 = jnp.float32
BF16 = jnp.bfloat16
HIGHEST = lax.Precision.HIGHEST

D = 1024
DEPTH = 4
GRID_W = 64
S5_W = 512
S5_G = 32
S5_C = 16
S5_N = 64
FN_W = 512
FN_G = 4
FN_C = 128
HEADS = 8
HD = 64
VD = 128
ROPE_BASE = 10000.0
ROPE_F = 16
N_EXP = 32
TOP_K = 4
SWIGLU_LIMIT = 7.0
SWIGLU_ALPHA = 1.702
ALPHA = (2.0 * DEPTH) ** 0.25
LN_EPS = 1e-5
LOG2E = 1.4426950408889634
HI16 = -65536

TM = 256
TME = 736
CH = 16
MIB = 1024 * 1024


def _cp(sem, vmem_mib=48):
    return pltpu.CompilerParams(dimension_semantics=sem, vmem_limit_bytes=vmem_mib * MIB)


def _rowtype(nctxb):
    return lambda i: (jnp.where(i >= nctxb, 1, 0), 0, 0)


def _layer_norm_rows(z, g, b):
    mu = jnp.mean(z, axis=-1, keepdims=True)
    zc = z - mu
    var = jnp.mean(zc * zc, axis=-1, keepdims=True)
    return zc * lax.rsqrt(var + LN_EPS) * g + b


def _ada_kernel(c_ref, w_ref, b_ref, o_ref):
    c = c_ref[...]
    a = c * jax.nn.sigmoid(c)
    o_ref[0] = jnp.dot(a, w_ref[0], preferred_element_type=F32, precision=HIGHEST) + b_ref[0]


def _ada_mods(cond8, ada_w, ada_b):
    tn = 1536
    return pl.pallas_call(
        _ada_kernel,
        grid=(DEPTH, 6 * D // tn),
        in_specs=[pl.BlockSpec((8, D), lambda l, j: (0, 0)),
                  pl.BlockSpec((1, D, tn), lambda l, j: (l, 0, j)),
                  pl.BlockSpec((1, 1, tn), lambda l, j: (l, 0, j))],
        out_specs=pl.BlockSpec((1, 8, tn), lambda l, j: (l, 0, j)),
        out_shape=jax.ShapeDtypeStruct((DEPTH, 8, 6 * D), F32),
        compiler_params=_cp(("arbitrary", "arbitrary")),
        name="ada_mods",
    )(cond8, ada_w, ada_b.reshape(DEPTH, 1, 6 * D))


def _even_in_kernel(x_ref, sc_ref, sh_ref, w_ref, wf_ref, u_ref, p_ref, q_ref):
    h = x_ref[...] * sc_ref[0] + sh_ref[0]
    z = jnp.dot(h.astype(BF16), w_ref[...], preferred_element_type=F32)
    u_ref[...] = z[:, :S5_W]
    for g in range(FN_G):
        f = z[:, S5_W + FN_C * g:S5_W + FN_C * (g + 1)]
        fc = f - jnp.mean(f, axis=-1, keepdims=True)
        fn = fc * lax.rsqrt(jnp.mean(fc * fc, axis=-1, keepdims=True) + LN_EPS)
        pq = jnp.dot(fn.astype(BF16), wf_ref[g], preferred_element_type=F32)
        p_ref[:, FN_C * g:FN_C * (g + 1)] = pq[:, :FN_C]
        q_ref[:, FN_C * g:FN_C * (g + 1)] = pq[:, FN_C:]


def _even_in(tok, sc1p, sh, w_in, wf, nctxb):
    nt = tok.shape[0]
    row = lambda i: (i, 0)
    return pl.pallas_call(
        _even_in_kernel,
        grid=(nt // TM,),
        in_specs=[pl.BlockSpec((TM, D), row),
                  pl.BlockSpec((1, 1, D), _rowtype(nctxb)),
                  pl.BlockSpec((1, 1, D), _rowtype(nctxb)),
                  pl.BlockSpec((D, D), lambda i: (0, 0)),
                  pl.BlockSpec((FN_G, FN_C, 2 * FN_C), lambda i: (0, 0, 0))],
        out_specs=[pl.BlockSpec((TM, S5_W), row),
                   pl.BlockSpec((TM, FN_W), row),
                   pl.BlockSpec((TM, FN_W), row)],
        out_shape=[jax.ShapeDtypeStruct((nt, S5_W), F32),
                   jax.ShapeDtypeStruct((nt, FN_W), F32),
                   jax.ShapeDtypeStruct((nt, FN_W), F32)],
        compiler_params=_cp(("arbitrary",)),
        name="even_in",
    )(tok, sc1p, sh, w_in, wf)


def _lane_iota(shape):
    return lax.broadcasted_iota(jnp.int32, shape, 1)


def _s5_in_kernel(x_ref, m1_ref, yi_ref, sfr_ref, sfi_ref, sbr_ref, sbi_ref):
    nc = x_ref.shape[0] // CH
    lane = _lane_iota((nc, 128)) // 32
    for pp in range(4):
        @pl.when(pl.program_id(1) == pp)
        def _(pp=pp):
            tiles = []
            for q in range(4):
                acc = None
                for r in range(4):
                    xs = x_ref[pl.ds(4 * q + r, nc, stride=CH), :]
                    shift = ((r - pp) * 32) % 128
                    if shift:
                        xs = pltpu.roll(xs, shift, 1)
                    acc = xs if acc is None else jnp.where(lane == r, xs, acc)
                tiles.append(acc.astype(BF16))
            u = jnp.concatenate(tiles, axis=1)
            r_all = jnp.dot(u, m1_ref[0], preferred_element_type=F32)
            yi_ref[0] = r_all[:, :512]
            sfr_ref[...] = r_all[:, 512:640]
            sfi_ref[...] = r_all[:, 640:768]
            sbr_ref[...] = r_all[:, 768:896]
            sbi_ref[...] = r_all[:, 896:1024]


def _s5_in(u, m1):
    nt = u.shape[0]
    nc = nt // CH
    sspec = pl.BlockSpec((nc, 128), lambda i, j: (0, 4 * i + j))
    sshape = jax.ShapeDtypeStruct((nc, 2048), F32)
    return pl.pallas_call(
        _s5_in_kernel,
        grid=(4, 4),
        in_specs=[pl.BlockSpec((nt, 128), lambda i, j: (0, i)),
                  pl.BlockSpec((1, 512, 1024), lambda i, j: (4 * i + j, 0, 0))],
        out_specs=[pl.BlockSpec((1, nc, 512), lambda i, j: (4 * i + j, 0, 0)), sspec, sspec, sspec, sspec],
        out_shape=[jax.ShapeDtypeStruct((16, nc, 512), F32), sshape, sshape, sshape, sshape],
        compiler_params=_cp(("arbitrary", "arbitrary")),
        name="s5_in",
    )(u, m1)


def _s5_scan_kernel(ncc, sfr_ref, sfi_ref, sbr_ref, sbi_ref, a_ref,
                    hfr_ref, hfi_ref, hbr_ref, hbi_ref):
    nc = sfr_ref.shape[0]
    afr, afi, abr, abi = a_ref[0:1, :], a_ref[1:2, :], a_ref[2:3, :], a_ref[3:4, :]

    def body(i, carry):
        fr, fi, br, bi = carry
        jb = jnp.where(i < ncc, ncc - 1 - i, nc - 1 - i + ncc)
        hfr_ref[pl.ds(i, 1), :] = fr
        hfi_ref[pl.ds(i, 1), :] = fi
        hbr_ref[pl.ds(jb, 1), :] = br
        hbi_ref[pl.ds(jb, 1), :] = bi
        sr, si = sfr_ref[pl.ds(i, 1), :], sfi_ref[pl.ds(i, 1), :]
        tr, ti = sbr_ref[pl.ds(jb, 1), :], sbi_ref[pl.ds(jb, 1), :]
        return (afr * fr - afi * fi + sr, afr * fi + afi * fr + si,
                abr * br - abi * bi + tr, abr * bi + abi * br + ti)

    z = jnp.zeros((1, sfr_ref.shape[1]), F32)
    lax.fori_loop(0, nc, body, (z, z, z, z))


def _s5_scan(sfr, sfi, sbr, sbi, avec, ncc):
    nc = sfr.shape[0]
    spec = pl.BlockSpec((nc, 512), lambda i: (0, i))
    shape = jax.ShapeDtypeStruct((nc, 2048), F32)
    return pl.pallas_call(
        functools.partial(_s5_scan_kernel, ncc),
        grid=(4,),
        in_specs=[spec, spec, spec, spec, pl.BlockSpec((8, 512), lambda i: (0, i))],
        out_specs=[spec, spec, spec, spec],
        out_shape=[shape, shape, shape, shape],
        compiler_params=_cp(("arbitrary",)),
        name="s5_scan",
    )(sfr, sfi, sbr, sbi, avec)


def _s5_out_kernel(yi_ref, hfr_ref, hfi_ref, hbr_ref, hbi_ref, w2_ref, o_ref, yp_ref):
    nc = yi_ref.shape[1]
    pp_dyn = pl.program_id(1)
    hcat = jnp.concatenate([hfr_ref[...], hfi_ref[...], hbr_ref[...], hbi_ref[...]], axis=1).astype(BF16)
    yp_ref[pp_dyn] = yi_ref[0] + jnp.dot(hcat, w2_ref[0], preferred_element_type=F32)

    @pl.when(pp_dyn == 3)
    def _():
        lane = _lane_iota((nc, 128)) // 32
        for s in range(CH):
            acc = None
            for pp in range(4):
                ys = yp_ref[pp, :, (s // 4) * 128:(s // 4 + 1) * 128]
                shift = ((pp - s % 4) * 32) % 128
                if shift:
                    ys = pltpu.roll(ys, shift, 1)
                acc = ys if acc is None else jnp.where(lane == pp, ys, acc)
            o_ref[pl.ds(s, nc, stride=CH), :] = acc


def _s5_out(yi, hfr, hfi, hbr, hbi, w2):
    nc = yi.shape[1]
    hspec = pl.BlockSpec((nc, 128), lambda i, j: (0, 4 * i + j))
    return pl.pallas_call(
        _s5_out_kernel,
        grid=(4, 4),
        in_specs=[pl.BlockSpec((1, nc, 512), lambda i, j: (4 * i + j, 0, 0)), hspec, hspec, hspec, hspec,
                  pl.BlockSpec((1, 512, 512), lambda i, j: (4 * i + j, 0, 0))],
        out_specs=pl.BlockSpec((nc * CH, 128), lambda i, j: (0, i)),
        out_shape=jax.ShapeDtypeStruct((nc * CH, S5_W), F32),
        scratch_shapes=[pltpu.VMEM((4, nc, 512), F32)],
        compiler_params=_cp(("arbitrary", "arbitrary")),
        name="s5_out",
    )(yi, hfr, hfi, hbr, hbi, w2)


def _toeplitz_kernel(k_ref, o_ref):
    lane = _lane_iota((32, 128)) // 32
    for s in range(CH):
        for q in range(CH // 4):
            acc = None
            for j in range(4):
                piece = k_ref[0, 4 * q + j - s + CH - 1]
                acc = piece if acc is None else jnp.where(lane == j, piece, acc)
            o_ref[0, s * 32:(s + 1) * 32, q * 128:(q + 1) * 128] = acc.astype(BF16)


def _toeplitz(kblk_rep):
    npair, nlag = kblk_rep.shape[:2]
    return pl.pallas_call(
        _toeplitz_kernel,
        grid=(npair,),
        in_specs=[pl.BlockSpec((1, nlag, 32, 128), lambda p: (p, 0, 0, 0))],
        out_specs=pl.BlockSpec((1, CH * 32, CH * 32), lambda p: (p, 0, 0)),
        out_shape=jax.ShapeDtypeStruct((npair, CH * 32, CH * 32), BF16),
        compiler_params=_cp(("arbitrary",)),
        name="s5_toeplitz",
    )(kblk_rep)


def _s5_matrices(lam_re, lam_im, log_dt, b_re, b_im, c_re, c_im, d_skip):
    lam = lax.complex(lam_re.astype(F32), lam_im.astype(F32))
    dt = jnp.exp(log_dt.astype(F32))[..., None]
    ldt = lam * dt
    lam_bar = jnp.exp(ldt)
    bb = ((lam_bar - 1.0) / lam)[..., None] * lax.complex(b_re.astype(F32), b_im.astype(F32))
    cc = lax.complex(c_re.astype(F32), c_im.astype(F32))
    ks = jnp.arange(CH + 1, dtype=F32)
    pw = jnp.exp(ldt[..., None] * ks)
    kk = jnp.real(jnp.einsum('dgcn,dgnk,dgne->dkgce', cc, pw[..., :CH], bb, precision=HIGHEST))
    k0 = kk[0, 0] + kk[1, 0] + jnp.eye(S5_C, dtype=F32)[None] * d_skip.astype(F32)[:, :, None]
    kall = jnp.concatenate([kk[1, 1:][::-1], k0[None], kk[0, 1:]], axis=0)
    eye2 = jnp.eye(2, dtype=F32)
    kt = kall.transpose(1, 0, 3, 2).reshape(16, 2, 2 * CH - 1, S5_C, S5_C)
    kblk = jnp.einsum('pglcd,gh->plgchd', kt, eye2).reshape(16, 2 * CH - 1, 32, 32)
    tp = _toeplitz(jnp.tile(kblk, (1, 1, 1, 4)))
    w1f = jnp.einsum('gns,gnc->gscn', pw[0][..., :CH][..., ::-1], bb[0])
    w1b = jnp.einsum('gns,gnc->gscn', pw[1][..., :CH], bb[1])

    def pair_cols(w):
        w = w.reshape(16, 2, CH, S5_C, S5_N)
        return jnp.einsum('pgscn,gh->psgchn', w, eye2).reshape(16, 512, 128)

    m1 = jnp.concatenate([tp, pair_cols(jnp.real(w1f)).astype(BF16), pair_cols(jnp.imag(w1f)).astype(BF16),
                          pair_cols(jnp.real(w1b)).astype(BF16), pair_cols(jnp.imag(w1b)).astype(BF16)], axis=-1)
    cpf = jnp.einsum('gcn,gnt->gntc', cc[0], pw[0][..., 1:])
    cpb = jnp.einsum('gcn,gnt->gntc', cc[1], pw[1][..., 1:][..., ::-1])

    def pair_rows(w):
        w = w.reshape(16, 2, S5_N, CH, S5_C)
        return jnp.einsum('pgntc,gh->pgnthc', w, eye2).reshape(16, 128, 512)

    w2 = jnp.concatenate([pair_rows(jnp.real(cpf)), pair_rows(-jnp.imag(cpf)),
                          pair_rows(jnp.real(cpb)), pair_rows(-jnp.imag(cpb))], axis=1)
    a_f, a_b = pw[0][..., CH].reshape(-1), pw[1][..., CH].reshape(-1)
    z = jnp.zeros_like(jnp.real(a_f))
    avec = jnp.stack([jnp.real(a_f), jnp.imag(a_f), jnp.real(a_b), jnp.imag(a_b), z, z, z, z])
    return m1.astype(BF16), w2.astype(BF16), avec


def _dft_lat_kernel(n1, n2, scale, p_ref, q_ref, m_ref, c2_ref, s2_ref, rot_ref, b_ref, o_ref, zr_ref, zi_ref):
    length = n1 * n2
    rr, ri = rot_ref[0], rot_ref[1]

    def phase1(blk, tw):
        b0 = pl.multiple_of(blk * 8, 8)
        pv = p_ref.at[pl.ds(b0, length - n2 + 8), :]
        qv = q_ref.at[pl.ds(b0, length - n2 + 8), :]
        zrv = zr_ref.at[pl.ds(b0, length - n2 + 8), :]
        ziv = zi_ref.at[pl.ds(b0, length - n2 + 8), :]
        twr, twi = tw
        for j in range(0, 8, 2):
            rows = [pl.ds(j + d, n1, stride=n2) for d in range(2)]
            x = jnp.concatenate(
                [jnp.concatenate([pv[rows[0], :], pv[rows[1], :]], axis=1),
                 jnp.concatenate([qv[rows[0], :], qv[rows[1], :]], axis=1)], axis=0).astype(BF16)
            z = jnp.dot(m_ref[...], x, preferred_element_type=F32)
            for d in range(2):
                zr, zi = z[:n1, d * 128:(d + 1) * 128], z[n1:, d * 128:(d + 1) * 128]
                zrv[rows[d], :] = zr * twr - zi * twi
                ziv[rows[d], :] = zr * twi + zi * twr
                twr, twi = twr * rr - twi * ri, twr * ri + twi * rr
        return twr, twi

    lax.fori_loop(0, n2 // 8, phase1, (jnp.ones((n1, 128), F32), jnp.zeros((n1, 128), F32)))

    def phase2(blk, carry):
        k0 = pl.multiple_of(blk * 8, 8)
        ov = o_ref.at[pl.ds(k0, length - n1 + 8), :]
        for j in range(0, 8, 2):
            r0 = pl.multiple_of((k0 + j) * n2, 8)
            r1 = pl.multiple_of((k0 + j + 1) * n2, 8)
            zr = jnp.concatenate([zr_ref[pl.ds(r0, n2), :], zr_ref[pl.ds(r1, n2), :]], axis=1).astype(BF16)
            zi = jnp.concatenate([zi_ref[pl.ds(r0, n2), :], zi_ref[pl.ds(r1, n2), :]], axis=1).astype(BF16)
            g = (jnp.dot(c2_ref[...], zr, preferred_element_type=F32)
                 + jnp.dot(s2_ref[...], zi, preferred_element_type=F32))
            for d in range(2):
                ov[pl.ds(j + d, n2, stride=n1), :] = g[:, d * 128:(d + 1) * 128] * scale + b_ref[...]
        return carry

    lax.fori_loop(0, n1 // 8, phase2, 0)


def _dft_lat(p, q, m, c2, s2, rot, bias, n1, n2, scale):
    length, w = p.shape
    col = pl.BlockSpec((length, 128), lambda j: (0, j), pipeline_mode=pl.Buffered(1))
    const = lambda shape: pl.BlockSpec(shape, lambda j: (0,) * len(shape))
    return pl.pallas_call(
        functools.partial(_dft_lat_kernel, n1, n2, scale),
        grid=(w // 128,),
        in_specs=[col, col, const((2 * n1, 2 * n1)), const((n2, n2)), const((n2, n2)), const((2, n1, 128)),
                  pl.BlockSpec((1, 128), lambda j: (0, j))],
        out_specs=pl.BlockSpec((length, 128), lambda j: (0, j)),
        out_shape=jax.ShapeDtypeStruct((length, w), F32),
        scratch_shapes=[pltpu.VMEM((length, 128), F32), pltpu.VMEM((length, 128), F32)],
        compiler_params=_cp(("arbitrary",), 56),
        name="fnet_dft_lat",
    )(p, q, m, c2, s2, rot, bias)


def _dft_small_kernel(scale, c_ref, s_ref, p_ref, q_ref, b_ref, o_ref):
    g = (jnp.dot(c_ref[...], p_ref[...].astype(BF16), preferred_element_type=F32)
         - jnp.dot(s_ref[...], q_ref[...].astype(BF16), preferred_element_type=F32))
    o_ref[...] = g * scale + b_ref[...]


def _dft_small(cm, sm, p, q, bias, scale):
    n, w = p.shape
    full = lambda shape: pl.BlockSpec(shape, lambda i: (0,) * len(shape))
    return pl.pallas_call(
        functools.partial(_dft_small_kernel, scale),
        grid=(1,),
        in_specs=[full((n, n)), full((n, n)), full((n, w)), full((n, w)), full((1, w))],
        out_specs=full((n, w)),
        out_shape=jax.ShapeDtypeStruct((n, w), F32),
        compiler_params=_cp(("arbitrary",)),
        name="fnet_dft_ctx",
    )(cm, sm, p, q, bias)


def _cos_sin(n_rows, n_cols, period):
    ang = (2.0 * math.pi / period) * ((jnp.arange(n_rows)[:, None] * jnp.arange(n_cols)[None, :]) % period).astype(F32)
    return jnp.cos(ang), jnp.sin(ang)


def _fnet_positions(p, q, bias, n_ctx):
    pc, qc, pl_, ql_ = p[:n_ctx], q[:n_ctx], p[n_ctx:], q[n_ctx:]
    length = pl_.shape[0]
    n1 = 1 << ((length.bit_length() - 1) // 2)
    n2 = length // n1
    assert n1 * n2 == length and n1 % 8 == 0
    bias = bias.reshape(1, FN_W).astype(F32)
    cc, sc = _cos_sin(n_ctx, n_ctx, n_ctx)
    g_ctx = _dft_small(cc.astype(BF16), sc.astype(BF16), pc, qc, bias, 1.0 / math.sqrt(FN_C * n_ctx))
    c1, s1 = _cos_sin(n1, n1, n1)
    m = jnp.concatenate([jnp.concatenate([c1, -s1], axis=1),
                         jnp.concatenate([-s1, -c1], axis=1)], axis=0).astype(BF16)
    ang = (2.0 * math.pi / length) * jnp.arange(n1, dtype=F32)
    rot = jnp.stack([jnp.broadcast_to(jnp.cos(ang)[:, None], (n1, 128)),
                     jnp.broadcast_to(-jnp.sin(ang)[:, None], (n1, 128))])
    c2, s2 = _cos_sin(n2, n2, n2)
    g_lat = _dft_lat(pl_, ql_, m, c2.astype(BF16), s2.astype(BF16), rot, bias, n1, n2,
                     1.0 / math.sqrt(FN_C * length))
    return jnp.concatenate([g_ctx, g_lat], axis=0)


def _even_out_kernel(ys_ref, g_ref, x_ref, wglu_ref, bglu_ref, wout_ref, gate_ref, lg_ref, lb_ref, o_ref):
    ys = ys_ref[...]
    c0 = math.sqrt(2.0 / math.pi)
    y = 0.5 * ys * (1.0 + jnp.tanh(c0 * (ys + 0.044715 * (ys * ys * ys))))
    gl = jnp.dot(y.astype(BF16), wglu_ref[...], preferred_element_type=F32) + bglu_ref[...]
    y2 = y * jax.nn.sigmoid(gl)
    m = (jnp.dot(y2.astype(BF16), wout_ref[:S5_W, :], preferred_element_type=F32)
         + jnp.dot(g_ref[...].astype(BF16), wout_ref[S5_W:, :], preferred_element_type=F32))
    z = ALPHA * x_ref[...] + gate_ref[0] * m
    o_ref[...] = _layer_norm_rows(z, lg_ref[...], lb_ref[...])


def _even_out(ys, g, tok, w_glu, b_glu, w_out, gate, ln_g, ln_b, nctxb):
    nt = tok.shape[0]
    row = lambda i: (i, 0)
    const = lambda i: (0, 0)
    return pl.pallas_call(
        _even_out_kernel,
        grid=(nt // TM,),
        in_specs=[pl.BlockSpec((TM, S5_W), row), pl.BlockSpec((TM, FN_W), row), pl.BlockSpec((TM, D), row),
                  pl.BlockSpec((S5_W, S5_W), const), pl.BlockSpec((1, S5_W), const),
                  pl.BlockSpec((D, D), const), pl.BlockSpec((1, 1, D), _rowtype(nctxb)),
                  pl.BlockSpec((1, D), const), pl.BlockSpec((1, D), const)],
        out_specs=pl.BlockSpec((TM, D), row),
        out_shape=jax.ShapeDtypeStruct((nt, D), F32),
        compiler_params=_cp(("arbitrary",)),
        name="even_out",
    )(ys, g, tok, w_glu, b_glu, w_out, gate, ln_g, ln_b)


def _qkv_kernel(x_ref, sc_ref, sh_ref, w_ref, cos_ref, sin_ref, q_ref, k_ref, v_ref):
    h = (x_ref[...] * sc_ref[0] + sh_ref[0]).astype(BF16)
    cos, sin = cos_ref[...], sin_ref[...]
    z = jnp.dot(h, w_ref[...], preferred_element_type=F32)
    first_half = (lax.broadcasted_iota(jnp.int32, cos.shape, 1) % (2 * ROPE_F)) < ROPE_F

    def rotate_half(t):
        return jnp.where(first_half, -pltpu.roll(t, 128 - ROPE_F, 1), pltpu.roll(t, ROPE_F, 1))

    for hh in range(HEADS):
        sl = slice(hh * 128, (hh + 1) * 128)
        zq, zk = z[:, sl], z[:, D + hh * 128:D + (hh + 1) * 128]
        q = zq * cos + rotate_half(zq) * sin
        k = zk * cos + rotate_half(zk) * sin
        q_ref[:, sl] = (q * (HD ** -0.5 * LOG2E)).astype(BF16)
        k_ref[:, sl] = k.astype(BF16)
        v_ref[:, 2 * hh * VD:(2 * hh + 1) * VD] = z[:, 2 * D + hh * VD:2 * D + (hh + 1) * VD].astype(BF16)
        v_ref[:, (2 * hh + 1) * VD:(2 * hh + 2) * VD] = jnp.ones((z.shape[0], VD), BF16)


def _qkv(tok, sc1p, sh, w5, cos, sin, nctxb):
    nt = tok.shape[0]
    row = lambda i: (i, 0)
    o = jax.ShapeDtypeStruct((nt, D), BF16)
    ov = jax.ShapeDtypeStruct((nt, 2 * D), BF16)
    return pl.pallas_call(
        _qkv_kernel,
        grid=(nt // TM,),
        in_specs=[pl.BlockSpec((TM, D), row),
                  pl.BlockSpec((1, 1, D), _rowtype(nctxb)),
                  pl.BlockSpec((1, 1, D), _rowtype(nctxb)),
                  pl.BlockSpec((D, 3 * D), lambda i: (0, 0)),
                  pl.BlockSpec((TM, 128), row), pl.BlockSpec((TM, 128), row)],
        out_specs=[pl.BlockSpec((TM, D), row), pl.BlockSpec((TM, D), row), pl.BlockSpec((TM, 2 * D), row)],
        out_shape=[o, o, ov],
        compiler_params=_cp(("arbitrary",)),
        name="qkv_rope",
    )(tok, sc1p, sh, w5, cos, sin)


def _attn_kernel(n_head, ts, n_pairs, q_ref, k_ref, v_ref, lam_ref, gs_ref, o_ref,
                 qq_ref, sa_ref, sb_ref, m_ref, acc_ref):
    tq = q_ref.shape[0]
    q = q_ref[...]
    lane = lax.broadcasted_iota(jnp.int32, q.shape, 1)
    zero = jnp.zeros_like(q)
    qq_ref[:tq, :] = jnp.where(lane < HD, q, zero)
    qq_ref[tq:, :] = jnp.where(lane >= HD, q, zero)
    m_ref[...] = jnp.full(m_ref.shape, -1e30, F32)
    acc_ref[...] = jnp.zeros(acc_ref.shape, F32)

    def scores(off, size, dst_ref):
        dst_ref[:, :size] = lax.dot_general(qq_ref[...], k_ref[pl.ds(off, size), :],
                                            (((1,), (1,)), ((), ())), preferred_element_type=F32)

    def consume(off, size, src_ref):
        tiles = [src_ref[:, t * 128:(t + 1) * 128] for t in range(size // 128)]
        mx = functools.reduce(jnp.maximum, tiles)
        m_old = m_ref[...]
        m_new = jnp.maximum(m_old, jnp.max(mx, axis=-1, keepdims=True))
        alpha = jnp.exp2(m_old - m_new)
        p = jnp.concatenate([jnp.exp2(t - m_new).astype(BF16) for t in tiles], axis=1)
        pv = jnp.dot(p, v_ref[pl.ds(off, size), :], preferred_element_type=F32)
        acc_ref[...] = jnp.concatenate([alpha, alpha], axis=1) * acc_ref[...] + pv
        m_ref[...] = m_new

    scores(0, n_head, sa_ref)
    if n_pairs:
        scores(n_head, ts, sb_ref)
    consume(0, n_head, sa_ref)

    def pair(i, last):
        off0 = pl.multiple_of(n_head + (2 * i) * ts, 128)
        off1 = pl.multiple_of(n_head + (2 * i + 1) * ts, 128)
        scores(off1, ts, sa_ref)
        consume(off0, ts, sb_ref)
        if not last:
            scores(pl.multiple_of(n_head + (2 * i + 2) * ts, 128), ts, sb_ref)
        consume(off1, ts, sa_ref)

    if n_pairs:
        def body(i, carry):
            pair(2 * i, False)
            pair(2 * i + 1, False)
            return carry
        lax.fori_loop(0, (n_pairs - 1) // 2, body, 0)
        if (n_pairs - 1) % 2:
            pair(n_pairs - 2, False)
        pair(n_pairs - 1, True)

    acc = acc_ref[...]
    o12 = acc[:, :VD] / acc[:, VD:]
    o = o12[:tq] - lam_ref[...] * o12[tq:]
    o = o * lax.rsqrt(jnp.mean(o * o, axis=-1, keepdims=True) + LN_EPS)
    o_ref[...] = (o * gs_ref[...]).astype(BF16)


def _attention(q, k, vext, lamv, gsv, tq, n_head, ts):
    nq, nk = q.shape[0], k.shape[0]
    n_sub = (nk - n_head) // ts if ts else 0
    assert n_head + n_sub * ts == nk and n_sub % 2 == 0 and nq % tq == 0
    const = lambda h, i: (0, 0)
    return pl.pallas_call(
        functools.partial(_attn_kernel, n_head, ts, n_sub // 2),
        grid=(HEADS, nq // tq),
        in_specs=[pl.BlockSpec((tq, 128), lambda h, i: (i, h)),
                  pl.BlockSpec((nk, 128), lambda h, i: (0, h), pipeline_mode=pl.Buffered(1)),
                  pl.BlockSpec((nk, 2 * VD), lambda h, i: (0, h), pipeline_mode=pl.Buffered(1)),
                  pl.BlockSpec((1, 128), const), pl.BlockSpec((1, 128), const)],
        out_specs=pl.BlockSpec((tq, 128), lambda h, i: (i, h)),
        out_shape=jax.ShapeDtypeStruct((nq, D), BF16),
        scratch_shapes=[pltpu.VMEM((2 * tq, 128), BF16),
                        pltpu.VMEM((2 * tq, max(ts, n_head)), F32), pltpu.VMEM((2 * tq, max(ts, n_head)), F32),
                        pltpu.VMEM((2 * tq, 128), F32), pltpu.VMEM((2 * tq, 2 * VD), F32)],
        compiler_params=_cp(("arbitrary", "arbitrary"), 56),
        name="diff_attention",
    )(q, k, vext, lamv, gsv)


def _proj_ln_kernel(a_ref, w_ref, x_ref, gate_ref, lg_ref, lb_ref, o_ref):
    m = jnp.dot(a_ref[...], w_ref[...], preferred_element_type=F32)
    z = ALPHA * x_ref[...] + gate_ref[0] * m
    o_ref[...] = _layer_norm_rows(z, lg_ref[...], lb_ref[...])


def _proj_ln(a, w, tok, gate, ln_g, ln_b, nctxb):
    nt = tok.shape[0]
    row = lambda i: (i, 0)
    const = lambda i: (0, 0)
    return pl.pallas_call(
        _proj_ln_kernel,
        grid=(nt // TM,),
        in_specs=[pl.BlockSpec((TM, D), row), pl.BlockSpec((D, D), const), pl.BlockSpec((TM, D), row),
                  pl.BlockSpec((1, 1, D), _rowtype(nctxb)), pl.BlockSpec((1, D), const), pl.BlockSpec((1, D), const)],
        out_specs=pl.BlockSpec((TM, D), row),
        out_shape=jax.ShapeDtypeStruct((nt, D), F32),
        compiler_params=_cp(("arbitrary",)),
        name="attn_out_ln",
    )(a, w, tok, gate, ln_g, ln_b)


def _router_kernel(x_ref, sc_ref, sh_ref, wh_ref, wl_ref, b_ref, tri_ref, h_ref, ri_ref, rw_ref, cnt_ref,
                   run_ref):
    @pl.when(pl.program_id(0) == 0)
    def _():
        run_ref[...] = jnp.zeros(run_ref.shape, F32)

    h = x_ref[...] * sc_ref[0] + sh_ref[0]
    hh = h.astype(BF16)
    hl = (h - hh.astype(F32)).astype(BF16)
    bits = lax.bitcast_convert_type(hh.astype(F32), jnp.int32)
    h_ref[...] = (bits[:, D // 2:] & HI16) | lax.shift_right_logical(bits[:, :D // 2], 16)
    v = (jnp.dot(hh, wh_ref[...], preferred_element_type=F32)
         + jnp.dot(hh, wl_ref[...], preferred_element_type=F32)
         + jnp.dot(hl, wh_ref[...], preferred_element_type=F32)) + b_ref[...]
    lane = lax.broadcasted_iota(jnp.int32, v.shape, 1)
    lane_f = lane.astype(F32)
    vals, sels, idxs = [], [], []
    for _ in range(TOP_K):
        mk = jnp.max(v, axis=-1, keepdims=True)
        ik = jnp.min(jnp.where(v == mk, lane_f, 128.0), axis=-1, keepdims=True)
        sel = lane_f == ik
        v = jnp.where(sel, -jnp.inf, v)
        vals.append(mk)
        sels.append(sel)
        idxs.append(ik)
    es = [jnp.exp(mk - vals[0]) for mk in vals]
    den = es[0] + es[1] + es[2] + es[3]
    onehot = jnp.zeros(v.shape, F32)
    for sel in sels:
        onehot = jnp.where(sel, 1.0, onehot)
    prefix = jnp.dot(tri_ref[...], onehot.astype(BF16), preferred_element_type=F32)
    rank_all = prefix + run_ref[0:1, :]
    run_ref[0:1, :] = run_ref[0:1, :] + jnp.sum(onehot, axis=0, keepdims=True)
    ri = jnp.zeros(v.shape, F32)
    rw = jnp.zeros(v.shape, F32)
    for k in range(TOP_K):
        rk = jnp.sum(jnp.where(sels[k], rank_all, 0.0), axis=-1, keepdims=True)
        ri = jnp.where(lane == k, idxs[k], ri)
        ri = jnp.where(lane == TOP_K + k, rk, ri)
        rw = jnp.where(lane == k, es[k] / den, rw)
    ri_ref[...] = ri.T[:8, :].astype(jnp.int32)
    rw_ref[...] = rw
    cnt_ref[...] = run_ref[...]


def _router(tok, sc1p, sh, w_hi, w_lo, bias, nctxb):
    nt = tok.shape[0]
    row = lambda i: (i, 0)
    const = lambda i: (0, 0)
    tri = (jnp.arange(TM)[:, None] > jnp.arange(TM)[None, :]).astype(BF16)
    return pl.pallas_call(
        _router_kernel,
        grid=(nt // TM,),
        in_specs=[pl.BlockSpec((TM, D), row),
                  pl.BlockSpec((1, 1, D), _rowtype(nctxb)), pl.BlockSpec((1, 1, D), _rowtype(nctxb)),
                  pl.BlockSpec((D, 128), const), pl.BlockSpec((D, 128), const), pl.BlockSpec((1, 128), const),
                  pl.BlockSpec((TM, TM), const)],
        out_specs=[pl.BlockSpec((TM, D // 2), row), pl.BlockSpec((8, TM), lambda i: (0, i)),
                   pl.BlockSpec((TM, 128), row),
                   pl.BlockSpec((8, 128), const)],
        out_shape=[jax.ShapeDtypeStruct((nt, D // 2), jnp.int32), jax.ShapeDtypeStruct((8, nt), jnp.int32),
                   jax.ShapeDtypeStruct((nt, 128), F32), jax.ShapeDtypeStruct((8, 128), F32)],
        scratch_shapes=[pltpu.VMEM((8, 128), F32)],
        compiler_params=_cp(("arbitrary",)),
        name="moe_router",
    )(tok, sc1p, sh, w_hi, w_lo, bias, tri)


def _dispatch_kernel(dest_ref, x_ref, init_ref, o_ref, sem):
    del init_ref
    nt = dest_ref.shape[0] // TOP_K
    base = pl.program_id(0) * TM

    def row_copy(r, k):
        slot = dest_ref[k * nt + base + r]
        return pltpu.make_async_copy(x_ref.at[pl.ds(r, 1), :], o_ref.at[pl.ds(slot, 1), :], sem)

    def issue(r, carry):
        for k in range(TOP_K):
            row_copy(r, k).start(priority=k % 2)
        return carry

    def drain(r, carry):
        for k in range(TOP_K):
            row_copy(r, k).wait()
        return carry

    lax.fori_loop(0, TM, issue, 0, unroll=4)
    lax.fori_loop(0, TM, drain, 0, unroll=4)


def _dispatch(dest_flat, hw, init):
    nt, w = hw.shape
    cap = init.shape[0]
    return pl.pallas_call(
        _dispatch_kernel,
        grid_spec=pltpu.PrefetchScalarGridSpec(
            num_scalar_prefetch=1,
            grid=(nt // TM,),
            in_specs=[pl.BlockSpec((TM, w), lambda i, d: (i, 0)), pl.BlockSpec(memory_space=pl.ANY)],
            out_specs=pl.BlockSpec(memory_space=pl.ANY),
            scratch_shapes=[pltpu.SemaphoreType.DMA(())]),
        out_shape=jax.ShapeDtypeStruct((cap, w), hw.dtype),
        input_output_aliases={2: 0},
        compiler_params=_cp(("arbitrary",)),
        name="moe_dispatch",
    )(dest_flat, hw, init)


def _expert_kernel(be_ref, nu_ref, x_ref, wg_ref, bg_ref, wu_ref, bu_ref, wd_ref, bd_ref, o_ref,
                   wgb_ref, wub_ref, wdb_ref):
    i = pl.program_id(0)
    prev = be_ref[jnp.maximum(i - 1, 0)]
    new_expert = jnp.logical_or(i == 0, be_ref[i] != prev)

    @pl.when(new_expert)
    def _():
        wgb_ref[...] = wg_ref[...].astype(BF16)
        wub_ref[...] = wu_ref[...].astype(BF16)
        wdb_ref[...] = wd_ref[...].astype(BF16)

    @pl.when(i < nu_ref[0])
    def _():
        xw = x_ref[...]
        x = jnp.concatenate([lax.bitcast_convert_type(lax.shift_left(xw, 16), F32),
                             lax.bitcast_convert_type(xw & HI16, F32)], axis=1).astype(BF16)
        g = jnp.minimum(jnp.dot(x, wgb_ref[...], preferred_element_type=F32) + bg_ref[...], SWIGLU_LIMIT)
        u = jnp.clip(jnp.dot(x, wub_ref[...], preferred_element_type=F32) + bu_ref[...], -SWIGLU_LIMIT, SWIGLU_LIMIT)
        act = g * jax.nn.sigmoid(SWIGLU_ALPHA * g) * (u + 1.0)
        y = jnp.dot(act.astype(BF16), wdb_ref[...], preferred_element_type=F32) + bd_ref[...]
        o_ref[...] = y.astype(BF16)

    @pl.when(i >= nu_ref[0])
    def _():
        o_ref[...] = jnp.zeros(o_ref.shape, BF16)


def _experts(layer, blk_e, n_used, xd, w_gate, b_gate, w_up, b_up, w_down, b_down):
    cap = xd.shape[0]
    wspec = pl.BlockSpec((None, None, D, D), lambda i, be, nu: (layer, be[i], 0, 0))
    bspec = pl.BlockSpec((None, None, 1, D), lambda i, be, nu: (layer, be[i], 0, 0))
    row = pl.BlockSpec((TME, D), lambda i, be, nu: (i, 0))
    xrow = pl.BlockSpec((TME, D // 2), lambda i, be, nu: (i, 0))
    b4 = lambda b: b.reshape(DEPTH, N_EXP, 1, D)
    return pl.pallas_call(
        _expert_kernel,
        grid_spec=pltpu.PrefetchScalarGridSpec(
            num_scalar_prefetch=2,
            grid=(cap // TME,),
            in_specs=[xrow, wspec, bspec, wspec, bspec, wspec, bspec],
            out_specs=row,
            scratch_shapes=[pltpu.VMEM((D, D), BF16)] * 3),
        out_shape=jax.ShapeDtypeStruct((cap, D), BF16),
        compiler_params=_cp(("arbitrary",), 56),
        name="moe_experts",
    )(blk_e, n_used, xd, w_gate, b4(b_gate), w_up, b4(b_up), w_down, b4(b_down))


def _moe_finish_kernel(yg_ref, w_ref, x_ref, gate_ref, lg_ref, lb_ref, o_ref):
    w = w_ref[...]
    y = yg_ref[0].astype(F32) * w[:, 0:1]
    for kk in range(1, TOP_K):
        y = y + yg_ref[kk].astype(F32) * w[:, kk:kk + 1]
    z = ALPHA * x_ref[...] + gate_ref[0] * y
    o_ref[...] = _layer_norm_rows(z, lg_ref[...], lb_ref[...])


def _moe_finish(yg, top_w, tok, gate, ln_g, ln_b, nctxb):
    nt = tok.shape[0]
    row = lambda i: (i, 0)
    const = lambda i: (0, 0)
    return pl.pallas_call(
        _moe_finish_kernel,
        grid=(nt // TM,),
        in_specs=[pl.BlockSpec((TOP_K, TM, D), lambda i: (0, i, 0)), pl.BlockSpec((TM, TOP_K), row),
                  pl.BlockSpec((TM, D), row),
                  pl.BlockSpec((1, 1, D), _rowtype(nctxb)), pl.BlockSpec((1, D), const), pl.BlockSpec((1, D), const)],
        out_specs=pl.BlockSpec((TM, D), row),
        out_shape=jax.ShapeDtypeStruct((nt, D), F32),
        compiler_params=_cp(("arbitrary",)),
        name="moe_finish",
    )(yg, top_w, tok, gate, ln_g, ln_b)


def _moe_layer(layer, tok, xd_prev, sc1p, sh, gate, ln_g, ln_b, w_router, b_router, w_gate, b_gate, w_up, b_up,
               w_down, b_down, nctxb):
    nt = tok.shape[0]
    wr = jnp.zeros((D, 128), F32).at[:, :N_EXP].set(w_router.astype(F32))
    wr_hi = wr.astype(BF16)
    wr_lo = (wr - wr_hi.astype(F32)).astype(BF16)
    br = jnp.full((1, 128), -1e30, F32).at[0, :N_EXP].set(b_router.astype(F32))
    hw, ri, rw, cnt = _router(tok, sc1p, sh, wr_hi, wr_lo, br, nctxb)
    top_idx, rank, top_w = ri[:TOP_K], ri[TOP_K:2 * TOP_K], rw[:, :TOP_K]
    counts = cnt[0, :N_EXP].astype(jnp.int32)
    padded = (counts + TME - 1) // TME * TME
    pad_end = jnp.cumsum(padded)
    pad_start = pad_end - padded
    dest = rank
    for e in range(N_EXP):
        dest = dest + jnp.where(top_idx == e, pad_start[e], 0)
    dest = dest.astype(jnp.int32).reshape(-1)
    cap = -(-(nt * TOP_K + N_EXP * TME) // TME) * TME
    nb = cap // TME
    blk_start = jnp.arange(nb, dtype=jnp.int32) * TME
    blk_e = jnp.minimum(jnp.sum((pad_end[None, :] <= blk_start[:, None]).astype(jnp.int32), axis=1), N_EXP - 1)
    n_used = (pad_end[-1] // TME).astype(jnp.int32).reshape(1)
    xd = _dispatch(dest, hw, jnp.zeros((cap, D // 2), jnp.int32) if xd_prev is None else xd_prev)
    yd = _experts(layer, blk_e, n_used, xd, w_gate, b_gate, w_up, b_up, w_down, b_down)
    yg = yd.at[dest].get(mode="promise_in_bounds").reshape(TOP_K, nt, D)
    return _moe_finish(yg, top_w, tok, gate, ln_g, ln_b, nctxb), xd


def _rope_tables(rows, n_ctx):
    row = jnp.broadcast_to(jnp.arange(rows, dtype=F32)[:, None], (rows, GRID_W)).reshape(-1)
    col = jnp.broadcast_to(jnp.arange(GRID_W, dtype=F32)[None, :], (rows, GRID_W)).reshape(-1)
    theta = ROPE_BASE ** (-jnp.arange(ROPE_F, dtype=F32) / ROPE_F)
    ang = jnp.stack([row[:, None] * theta, col[:, None] * theta], axis=1)
    ang = jnp.stack([ang, ang], axis=2).reshape(rows * GRID_W, HD)
    cos = jnp.concatenate([jnp.ones((n_ctx, HD), F32), jnp.cos(ang)], axis=0)
    sin = jnp.concatenate([jnp.zeros((n_ctx, HD), F32), jnp.sin(ang)], axis=0)
    return jnp.tile(cos, (1, 2)), jnp.tile(sin, (1, 2))


def kernel(x, c, ctx, c_ctx, ada_w, ada_b, ln_g, ln_b, even_w_in, s5_lam_re, s5_lam_im, s5_log_dt, s5_b_re, s5_b_im, s5_c_re, s5_c_im, s5_d, s5_w_glu, s5_b_glu, fnet_w, fnet_b, even_w_out, odd_w_qkv, odd_w_o, da_lq1, da_lk1, da_lq2, da_lk2, da_subln_g, router_w, router_b, moe_w_gate, moe_b_gate, moe_w_up, moe_b_up, moe_w_down, moe_b_down):
    seq = x.shape[1]
    n_ctx = ctx.shape[1]
    assert x.shape[0] == 1 and n_ctx % TM == 0 and seq % TM == 0 and n_ctx == TM
    nctxb = n_ctx // TM
    nt = n_ctx + seq
    nc = nt // CH
    ncc = n_ctx // CH
    tq_lat = 1024 if seq % 1024 == 0 else TM
    ts_lat = 512 if seq % 1024 == 0 else TM
    assert seq % (2 * ts_lat) == 0

    tok = jnp.concatenate([ctx[0], x[0]], axis=0).astype(F32)
    cond8 = jnp.zeros((8, D), F32).at[0].set(c_ctx.astype(F32)).at[1].set(c[0].astype(F32))
    mods = _ada_mods(cond8, ada_w, ada_b)[:, :2].reshape(DEPTH, 2, 6, 1, D)
    cos, sin = _rope_tables(seq // GRID_W, n_ctx)
    c128, s128 = _cos_sin(FN_C, FN_C, FN_C)
    xd = None

    for l in range(DEPTH):
        i = l // 2
        m = mods[l]
        shift_a, scale_a, gate_a, shift_b, scale_b, gate_b = (m[:, j] for j in range(6))
        lg0, lb0 = ln_g[l, 0].reshape(1, D), ln_b[l, 0].reshape(1, D)
        lg1, lb1 = ln_g[l, 1].reshape(1, D), ln_b[l, 1].reshape(1, D)
        if l % 2 == 0:
            wf = jnp.concatenate([jnp.einsum('ab,gbd->gad', c128, fnet_w[i].astype(F32), precision=HIGHEST),
                                  jnp.einsum('ab,gbd->gad', s128, fnet_w[i].astype(F32), precision=HIGHEST)],
                                 axis=-1).astype(BF16)
            u, p, q = _even_in(tok, 1.0 + scale_a, shift_a, even_w_in[i].astype(BF16), wf, nctxb)
            m1, w2, avec = _s5_matrices(s5_lam_re[i], s5_lam_im[i], s5_log_dt[i], s5_b_re[i], s5_b_im[i],
                                        s5_c_re[i], s5_c_im[i], s5_d[i])
            yi, sfr, sfi, sbr, sbi = _s5_in(u, m1)
            hfr, hfi, hbr, hbi = _s5_scan(sfr, sfi, sbr, sbi, avec, ncc)
            ys = _s5_out(yi, hfr, hfi, hbr, hbi, w2)
            g = _fnet_positions(p, q, fnet_b[i], n_ctx)
            tok = _even_out(ys, g, tok, s5_w_glu[i].astype(BF16), s5_b_glu[i].reshape(1, S5_W).astype(F32),
                            even_w_out[i].astype(BF16), gate_a, lg0, lb0, nctxb)
        else:
            lam_init = 0.8 - 0.6 * math.exp(-0.3 * l)
            lam = (jnp.exp(jnp.sum(da_lq1[i].astype(F32) * da_lk1[i].astype(F32)))
                   - jnp.exp(jnp.sum(da_lq2[i].astype(F32) * da_lk2[i].astype(F32))) + lam_init)
            qb, kb, vb = _qkv(tok, 1.0 + scale_a, shift_a, odd_w_qkv[i].astype(BF16), cos, sin, nctxb)
            lamv = jnp.full((1, VD), lam, F32)
            gsv = (da_subln_g[i].astype(F32) * (1.0 - lam_init)).reshape(1, VD)
            on_lat = _attention(qb[n_ctx:], kb, vb, lamv, gsv, tq_lat, n_ctx, ts_lat)
            on_ctx = _attention(qb[:n_ctx], kb[:n_ctx], vb[:n_ctx], lamv, gsv, n_ctx, n_ctx, 0)
            on = jnp.concatenate([on_ctx, on_lat], axis=0)
            tok = _proj_ln(on, odd_w_o[i].astype(BF16), tok, gate_a, lg0, lb0, nctxb)
        tok, xd = _moe_layer(l, tok, xd, 1.0 + scale_b, shift_b, gate_b, lg1, lb1, router_w[l], router_b[l],
                             moe_w_gate, moe_b_gate, moe_w_up, moe_b_up, moe_w_down, moe_b_down, nctxb)
    return tok[n_ctx:].reshape(1, seq, D).astype(x.dtype)
```
